```python
import math
import jax, jax.numpy as jnp
from jax import lax
import numpy as np

D_MODEL = 2048
BATCH = 8
SEQ = 2048
DEPTH = 2

EPS = 1e-6
GRID_W = 64
HEAD_DIM = 128
ATTN_HEADS = D_MODEL // 256
KV_HEADS = max(1, ATTN_HEADS // 4)
ATTN_WIDTH = ATTN_HEADS * HEAD_DIM
KV_WIDTH = KV_HEADS * HEAD_DIM
Q_BLOCK = 128
ROPE_THETA = 10000.0
ROPE_HALF = HEAD_DIM // 2
LRU_WIDTH = D_MODEL // 2
LRU_BLOCKS = 8
LRU_BLOCK = LRU_WIDTH // LRU_BLOCKS
LRU_C = 8.0
CONV_W = 4
CONV_PAD = (2, 1)
EVEN_IN = ATTN_WIDTH + 2 * KV_WIDTH + ATTN_WIDTH + 2 * LRU_WIDTH
EVEN_MIX = ATTN_WIDTH + LRU_WIDTH
MLSTM_HEADS = 8
MLSTM_V_DIM = D_MODEL // MLSTM_HEADS
MLSTM_QK_DIM = MLSTM_V_DIM // 2
MLSTM_WIDTH = MLSTM_HEADS * MLSTM_V_DIM
MLSTM_QK_WIDTH = MLSTM_HEADS * MLSTM_QK_DIM
MLSTM_CHUNK = 128
N_GATE_SETS = 4
ODD_IN = 2 * MLSTM_QK_WIDTH + 3 * MLSTM_WIDTH + N_GATE_SETS * MLSTM_HEADS
N_EVEN = (DEPTH + 1) // 2
N_ODD = DEPTH // 2

kernel_name = 'hybrid_gqa_rglru_mlstm_encoder'


def rmsnorm(x, g):
    xf = x.astype(jnp.float32)
    y = xf * lax.rsqrt(jnp.mean(xf * xf, axis=-1, keepdims=True) + EPS) * g.astype(jnp.float32)
    return y.astype(x.dtype)


def split_at(t, sizes):
    idx = np.cumsum(sizes)[:-1].tolist()
    return jnp.split(t, idx, axis=-1)


def axial_angles(seq_len):
    rows = seq_len // GRID_W
    row = jnp.repeat(jnp.arange(rows), GRID_W).astype(jnp.float32)
    col = jnp.tile(jnp.arange(GRID_W), rows).astype(jnp.float32)
    inv = ROPE_THETA ** (-jnp.arange(0, ROPE_HALF, 2, dtype=jnp.float32) / ROPE_HALF)
    return row[:, None] * inv, col[:, None] * inv


def rotate_half_pairs(x, ang):
    x1, x2 = jnp.split(x, 2, axis=-1)
    c = jnp.cos(ang)[None, :, None, :]
    s = jnp.sin(ang)[None, :, None, :]
    return jnp.concatenate([x1 * c - x2 * s, x1 * s + x2 * c], axis=-1)


def axial_rope(x, ang_row, ang_col):
    xr, xc = jnp.split(x, 2, axis=-1)
    return jnp.concatenate([rotate_half_pairs(xr, ang_row), rotate_half_pairs(xc, ang_col)], axis=-1)


def block_attention(q, k, v):
    B, S, H, D = q.shape
    G = H // KV_HEADS
    nb = S // Q_BLOCK
    qb = q.reshape(B, nb, Q_BLOCK, KV_HEADS, G, D).transpose(1, 0, 3, 4, 2, 5)
    scale = HEAD_DIM ** -0.5

    def one_block(qi):
        s = jnp.einsum('bkgqd,bskd->bkgqs', qi, k) * scale
        p = jax.nn.softmax(s, axis=-1)
        return jnp.einsum('bkgqs,bskd->bkgqd', p, v)

    o = lax.map(one_block, qb)
    return o.transpose(1, 0, 4, 2, 3, 5).reshape(B, S, H * D)


def lru_combine(e1, e2):
    a1, b1 = e1
    a2, b2 = e2
    return a1 * a2, a2 * b1 + b2


def rg_lru(xc, wa, ba, wx, bx, lam, reverse):
    B, S, W = xc.shape
    xb = xc.reshape(B, S, LRU_BLOCKS, LRU_BLOCK)
    r = jax.nn.sigmoid(jnp.einsum('bsnc,ncd->bsnd', xb, wa).reshape(B, S, W) + ba)
    i = jax.nn.sigmoid(jnp.einsum('bsnc,ncd->bsnd', xb, wx).reshape(B, S, W) + bx)
    log_a = LRU_C * r * jax.nn.log_sigmoid(lam)
    a = jnp.exp(log_a)
    u = jnp.sqrt(-jnp.expm1(2.0 * log_a)) * (i * xc)
    _, h = lax.associative_scan(lru_combine, (a, u), reverse=reverse, axis=1)
    return h


def even_layer(h, w_in, w_out, q_gain, k_gain, conv_w, conv_b, wa, ba, wx, bx, lam):
    B, S, _ = h.shape
    f32 = jnp.float32
    proj = h @ w_in
    q, k, v, g_attn, x_lru, g_lru = split_at(
        proj, [ATTN_WIDTH, KV_WIDTH, KV_WIDTH, ATTN_WIDTH, LRU_WIDTH, LRU_WIDTH])
    q = rmsnorm(q.reshape(B, S, ATTN_HEADS, HEAD_DIM).astype(f32), q_gain)
    k = rmsnorm(k.reshape(B, S, KV_HEADS, HEAD_DIM).astype(f32), k_gain)
    v = v.reshape(B, S, KV_HEADS, HEAD_DIM).astype(f32)
    ang_row, ang_col = axial_angles(S)
    q = axial_rope(q, ang_row, ang_col)
    k = axial_rope(k, ang_row, ang_col)
    attn = block_attention(q, k, v)
    xc = lax.conv_general_dilated(
        x_lru, conv_w[:, None, :], window_strides=(1,), padding=[CONV_PAD],
        dimension_numbers=('NWC', 'WIO', 'NWC'), feature_group_count=LRU_WIDTH) + conv_b
    xc = xc.astype(f32)
    y = (rg_lru(xc, wa[0], ba[0], wx[0], bx[0], lam[0], reverse=False)
         + rg_lru(xc, wa[1], ba[1], wx[1], bx[1], lam[1], reverse=True))
    mix = jnp.concatenate([attn * jax.nn.silu(g_attn.astype(f32)),
                           y * jax.nn.silu(g_lru.astype(f32))], axis=-1)
    return mix.astype(h.dtype) @ w_out


def mlstm_chunkwise(q, k, v, ig, lf):
    B, H, S, dk = q.shape
    dv = v.shape[-1]
    L = MLSTM_CHUNK
    nc = S // L

    def to_chunks(t):
        return jnp.moveaxis(t.reshape((B, H, nc, L) + t.shape[3:]), 2, 0)

    mask = jnp.tril(jnp.ones((L, L), dtype=bool))

    def step(carry, inp):
        C, n, m = carry
        qi, ki, vi, ii, fi = inp
        b = jnp.cumsum(fi, axis=-1)
        logd = jnp.where(mask, b[..., :, None] - b[..., None, :] + ii[..., None, :], -jnp.inf)
        m_inter = b + m[..., None]
        m_t = jnp.maximum(m_inter, jnp.max(logd, axis=-1))
        sc = jnp.einsum('bhld,bhsd->bhls', qi, ki) * jnp.exp(logd - m_t[..., None])
        inter = jnp.exp(m_inter - m_t)
        num = jnp.einsum('bhls,bhsv->bhlv', sc, vi) + inter[..., None] * jnp.einsum('bhld,bhvd->bhlv', qi, C)
        den = jnp.sum(sc, axis=-1) + inter * jnp.einsum('bhld,bhd->bhl', qi, n)
        hout = num / jnp.maximum(jnp.abs(den), jnp.exp(-m_t))[..., None]
        b_last = b[..., -1]
        w = b_last[..., None] - b + ii
        m_new = jnp.maximum(b_last + m, jnp.max(w, axis=-1))
        decay = jnp.exp(b_last + m - m_new)
        ws = jnp.exp(w - m_new[..., None])
        C_new = decay[..., None, None] * C + jnp.einsum('bhsv,bhsd->bhvd', vi * ws[..., None], ki)
        n_new = decay[..., None] * n + jnp.einsum('bhs,bhsd->bhd', ws, ki)
        return (C_new, n_new, m_new), hout

    init = (jnp.zeros((B, H, dv, dk), jnp.float32), jnp.zeros((B, H, dk), jnp.float32),
            jnp.zeros((B, H), jnp.float32))
    _, hc = lax.scan(step, init, (to_chunks(q), to_chunks(k), to_chunks(v), to_chunks(ig), to_chunks(lf)))
    return jnp.moveaxis(hc, 0, 2).reshape(B, H, S, dv)


def odd_layer(h, w_in, gate_bias, norm_gain, w_out):
    B, S, _ = h.shape
    f32 = jnp.float32
    proj = h @ w_in
    q, k, v, o, z, gates = split_at(
        proj, [MLSTM_QK_WIDTH, MLSTM_QK_WIDTH, MLSTM_WIDTH, MLSTM_WIDTH, MLSTM_WIDTH, N_GATE_SETS * MLSTM_HEADS])

    def heads(t, d):
        return t.reshape(B, S, MLSTM_HEADS, d).transpose(0, 2, 1, 3).astype(f32)

    q = heads(q, MLSTM_QK_DIM)
    k = heads(k, MLSTM_QK_DIM) * (MLSTM_QK_DIM ** -0.5)
    v = heads(v, MLSTM_V_DIM)
    g = gates.astype(f32).reshape(B, S, N_GATE_SETS, MLSTM_HEADS) + gate_bias.astype(f32)
    i_f, i_b, f_f, f_b = g.transpose(2, 0, 3, 1)
    h_f = mlstm_chunkwise(q, k, v, i_f, jax.nn.log_sigmoid(f_f))
    flip = lambda t: jnp.flip(t, axis=2)
    h_b = flip(mlstm_chunkwise(flip(q), flip(k), flip(v), flip(i_b), flip(jax.nn.log_sigmoid(f_b))))
    hs = (h_f + h_b).transpose(0, 2, 1, 3)
    hs = jax.nn.sigmoid(o.astype(f32)).reshape(B, S, MLSTM_HEADS, MLSTM_V_DIM) * hs
    hs = rmsnorm(hs, norm_gain.reshape(MLSTM_HEADS, MLSTM_V_DIM))
    hs = hs.reshape(B, S, MLSTM_WIDTH) * jax.nn.silu(z.astype(f32))
    return hs.astype(h.dtype) @ w_out


def setup_inputs(seed: int = 0) -> dict:
    key = jax.random.key(seed)
    ks = jax.random.split(key, 24)
    f32 = jnp.float32
    nrm = lambda k, shape, scale: jax.random.normal(k, shape, f32) * scale
    x = nrm(ks[0], (BATCH, SEQ, D_MODEL), 1.0)
    norm_gain = 1.0 + nrm(ks[1], (DEPTH, D_MODEL), 0.05)
    final_gain = 1.0 + nrm(ks[2], (D_MODEL,), 0.05)
    even_w_in = nrm(ks[3], (N_EVEN, D_MODEL, EVEN_IN), D_MODEL ** -0.5)
    even_w_out = nrm(ks[4], (N_EVEN, EVEN_MIX, D_MODEL), EVEN_MIX ** -0.5)
    q_norm_gain = 1.0 + nrm(ks[5], (N_EVEN, HEAD_DIM), 0.05)
    k_norm_gain = 1.0 + nrm(ks[6], (N_EVEN, HEAD_DIM), 0.05)
    conv_w = nrm(ks[7], (N_EVEN, CONV_W, LRU_WIDTH), CONV_W ** -0.5)
    conv_b = nrm(ks[8], (N_EVEN, LRU_WIDTH), 0.02)
    lru_wa = nrm(ks[9], (N_EVEN, 2, LRU_BLOCKS, LRU_BLOCK, LRU_BLOCK), LRU_BLOCK ** -0.5)
    lru_ba = nrm(ks[10], (N_EVEN, 2, LRU_WIDTH), 0.1)
    lru_wx = nrm(ks[11], (N_EVEN, 2, LRU_BLOCKS, LRU_BLOCK, LRU_BLOCK), LRU_BLOCK ** -0.5)
    lru_bx = nrm(ks[12], (N_EVEN, 2, LRU_WIDTH), 0.1)
    a0 = jax.random.uniform(ks[13], (N_EVEN, 2, LRU_WIDTH), f32, minval=0.9, maxval=0.999)
    p = a0 ** (1.0 / LRU_C)
    lru_lambda = jnp.log(p) - jnp.log1p(-p)
    odd_w_in = nrm(ks[14], (N_ODD, D_MODEL, ODD_IN), D_MODEL ** -0.5)
    i_bias = nrm(ks[15], (N_ODD, 2, MLSTM_HEADS), 0.1)
    f_bias = jnp.broadcast_to(jnp.linspace(3.0, 6.0, MLSTM_HEADS, dtype=f32), (N_ODD, 2, MLSTM_HEADS)) \
        + nrm(ks[16], (N_ODD, 2, MLSTM_HEADS), 0.1)
    odd_gate_bias = jnp.concatenate([i_bias, f_bias], axis=1)
    odd_norm_gain = 1.0 + nrm(ks[17], (N_ODD, MLSTM_WIDTH), 0.05)
    odd_w_out = nrm(ks[18], (N_ODD, MLSTM_WIDTH, D_MODEL), MLSTM_WIDTH ** -0.5)
    return {'x': x, 'norm_gain': norm_gain, 'final_gain': final_gain,
            'even_w_in': even_w_in, 'even_w_out': even_w_out,
            'q_norm_gain': q_norm_gain, 'k_norm_gain': k_norm_gain,
            'conv_w': conv_w, 'conv_b': conv_b,
            'lru_wa': lru_wa, 'lru_ba': lru_ba, 'lru_wx': lru_wx, 'lru_bx': lru_bx,
            'lru_lambda': lru_lambda,
            'odd_w_in': odd_w_in, 'odd_gate_bias': odd_gate_bias,
            'odd_norm_gain': odd_norm_gain, 'odd_w_out': odd_w_out}


def reference(x, norm_gain, final_gain, even_w_in, even_w_out, q_norm_gain, k_norm_gain,
              conv_w, conv_b, lru_wa, lru_ba, lru_wx, lru_bx, lru_lambda,
              odd_w_in, odd_gate_bias, odd_norm_gain, odd_w_out):
    for layer in range(DEPTH):
        hn = rmsnorm(x, norm_gain[layer])
        j = layer // 2
        if layer % 2 == 0:
            out = even_layer(hn, even_w_in[j], even_w_out[j], q_norm_gain[j], k_norm_gain[j],
                             conv_w[j], conv_b[j], lru_wa[j], lru_ba[j], lru_wx[j], lru_bx[j],
                             lru_lambda[j])
        else:
            out = odd_layer(hn, odd_w_in[j], odd_gate_bias[j], odd_norm_gain[j], odd_w_out[j])
        x = x + out.astype(x.dtype)
    return rmsnorm(x, final_gain)
```

```python
import functools

import jax
import jax.numpy as jnp
from jax import lax
from jax.experimental import pallas as pl
from jax.experimental.pallas import tpu as pltpu

F32 = jnp.float32
BF16 = jnp.bfloat16

EPS = 1e-6
GRID_W = 64
HEAD_DIM = 128
ATTN_HEADS = 8
KV_HEADS = 2
Q_GROUP = ATTN_HEADS // KV_HEADS
ATTN_WIDTH = ATTN_HEADS * HEAD_DIM
KV_WIDTH = KV_HEADS * HEAD_DIM
ROPE_THETA = 10000.0
LRU_WIDTH = 1024
LRU_BLOCKS = 8
LRU_BLOCK = 128
LRU_C = 8.0
CONV_W = 4
CONV_LEFT = 2
EVEN_IN = ATTN_WIDTH + 2 * KV_WIDTH + ATTN_WIDTH + 2 * LRU_WIDTH
MLSTM_HEADS = 8
MLSTM_V_DIM = 256
MLSTM_QK_DIM = 128
MLSTM_WIDTH = MLSTM_HEADS * MLSTM_V_DIM
MLSTM_QK_WIDTH = MLSTM_HEADS * MLSTM_QK_DIM
MLSTM_CHUNK = 128
N_GATE_SETS = 4

LANES = 128
SUBLANES = 8
VMEM_LIMIT = 56 * 1024 * 1024
SCAN_SEGMENTS = SUBLANES


def _sigmoid(x):
    return 1.0 / (1.0 + jnp.exp(-x))


def _log_sigmoid(x):
    return jnp.minimum(x, 0.0) - jnp.log1p(jnp.exp(-jnp.abs(x)))


def _params(*sem):
    return pltpu.CompilerParams(dimension_semantics=sem, vmem_limit_bytes=VMEM_LIMIT)


def _rmsnorm_rows_to(x_ref, g_ref, dst_ref, rows, chunk):
    g = g_ref[...]

    def body(c, carry):
        r0 = pl.multiple_of(c * chunk, chunk)
        x = x_ref[pl.ds(r0, chunk), :]
        ms = jnp.mean(x * x, axis=-1, keepdims=True)
        dst_ref[pl.ds(r0, chunk), :] = (x * lax.rsqrt(ms + EPS) * g).astype(dst_ref.dtype)
        return carry

    lax.fori_loop(0, rows // chunk, body, 0)


def _norm_matmul_kernel(x_ref, g_ref, w_ref, o_ref, hn_ref, *, tm):
    @pl.when(pl.program_id(1) == 0)
    def _():
        _rmsnorm_rows_to(x_ref, g_ref, hn_ref, tm, 64)

    o_ref[...] = jnp.dot(hn_ref[...], w_ref[...], preferred_element_type=F32).astype(o_ref.dtype)


def _norm_matmul(x, gain, w, out_dtype, tm, tn):
    t, d = x.shape
    n = w.shape[1]
    return pl.pallas_call(
        functools.partial(_norm_matmul_kernel, tm=tm),
        out_shape=jax.ShapeDtypeStruct((t, n), out_dtype),
        grid=(t // tm, n // tn),
        in_specs=[
            pl.BlockSpec((tm, d), lambda i, j: (i, 0)),
            pl.BlockSpec((1, d), lambda i, j: (0, 0)),
            pl.BlockSpec((d, tn), lambda i, j: (0, j)),
        ],
        out_specs=pl.BlockSpec((tm, tn), lambda i, j: (i, j)),
        scratch_shapes=[pltpu.VMEM((tm, d), BF16)],
        compiler_params=_params("parallel", "arbitrary"),
        name="norm_in_proj",
    )(x, gain, w)


def _matmul_kernel(a_ref, w_ref, o_ref):
    o_ref[...] = jnp.dot(a_ref[...], w_ref[...], preferred_element_type=F32).astype(o_ref.dtype)


def _matmul(a, w, n, col_block0, out_dtype, tm, tn, name):
    t, k = a.shape
    return pl.pallas_call(
        _matmul_kernel,
        out_shape=jax.ShapeDtypeStruct((t, n), out_dtype),
        grid=(t // tm, n // tn),
        in_specs=[
            pl.BlockSpec((tm, k), lambda i, j: (i, 0)),
            pl.BlockSpec((k, tn), lambda i, j: (0, j + col_block0)),
        ],
        out_specs=pl.BlockSpec((tm, tn), lambda i, j: (i, j)),
        compiler_params=_params("parallel", "arbitrary"),
        name=name,
    )(a, w)


def _out_proj_epilogue(y_ref, x_ref, g_ref, xo_ref, no_ref, rows, chunk):
    g = g_ref[...]

    def body(c, carry):
        r0 = pl.multiple_of(c * chunk, chunk)
        xn = x_ref[pl.ds(r0, chunk), :] + y_ref[pl.ds(r0, chunk), :]
        if xo_ref is not None:
            xo_ref[pl.ds(r0, chunk), :] = xn
        ms = jnp.mean(xn * xn, axis=-1, keepdims=True)
        no_ref[pl.ds(r0, chunk), :] = (xn * lax.rsqrt(ms + EPS) * g).astype(no_ref.dtype)
        return carry

    lax.fori_loop(0, rows // chunk, body, 0)


def _out_proj0_kernel(a_ref, b_ref, wa_ref, wb_ref, x_ref, g_ref, xo_ref, no_ref, y_ref, *, tm):
    y_ref[...] = (jnp.dot(a_ref[...], wa_ref[...], preferred_element_type=F32)
                  + jnp.dot(b_ref[...], wb_ref[...], preferred_element_type=F32))
    _out_proj_epilogue(y_ref, x_ref, g_ref, xo_ref, no_ref, tm, 64)


def _out_proj0(a, b, w, x, gain, tm):
    t, ka = a.shape
    kb = b.shape[1]
    d = w.shape[1]
    return pl.pallas_call(
        functools.partial(_out_proj0_kernel, tm=tm),
        out_shape=(jax.ShapeDtypeStruct((t, d), F32), jax.ShapeDtypeStruct((t, d), BF16)),
        grid=(t // tm,),
        in_specs=[
            pl.BlockSpec((tm, ka), lambda i: (i, 0)),
            pl.BlockSpec((tm, kb), lambda i: (i, 0)),
            pl.BlockSpec((ka, d), lambda i: (0, 0)),
            pl.BlockSpec((kb, d), lambda i: (ka // kb, 0)),
            pl.BlockSpec((tm, d), lambda i: (i, 0)),
            pl.BlockSpec((1, d), lambda i: (0, 0)),
        ],
        out_specs=(pl.BlockSpec((tm, d), lambda i: (i, 0)), pl.BlockSpec((tm, d), lambda i: (i, 0))),
        scratch_shapes=[pltpu.VMEM((tm, d), F32)],
        compiler_params=_params("parallel"),
        name="out_proj0",
    )(a, b, w, w, x, gain)


def _out_proj1_kernel(a_ref, w_ref, x_ref, g_ref, no_ref, y_ref, *, tm):
    y_ref[...] = jnp.dot(a_ref[...], w_ref[...], preferred_element_type=F32)
    _out_proj_epilogue(y_ref, x_ref, g_ref, None, no_ref, tm, 64)


def _out_proj1(a, w, x, gain, tm):
    t, k = a.shape
    d = w.shape[1]
    return pl.pallas_call(
        functools.partial(_out_proj1_kernel, tm=tm),
        out_shape=jax.ShapeDtypeStruct((t, d), F32),
        grid=(t // tm,),
        in_specs=[
            pl.BlockSpec((tm, k), lambda i: (i, 0)),
            pl.BlockSpec((k, d), lambda i: (0, 0)),
            pl.BlockSpec((tm, d), lambda i: (i, 0)),
            pl.BlockSpec((1, d), lambda i: (0, 0)),
        ],
        out_specs=pl.BlockSpec((tm, d), lambda i: (i, 0)),
        scratch_shapes=[pltpu.VMEM((tm, d), F32)],
        compiler_params=_params("parallel"),
        name="out_proj1",
    )(a, w, x, gain)


def _rope_tables(seq_len):
    half = HEAD_DIM // 2
    t = jnp.arange(seq_len)
    row = (t // GRID_W).astype(F32)
    col = (t % GRID_W).astype(F32)
    inv = ROPE_THETA ** (-jnp.arange(0, half, 2, dtype=F32) / half)
    ar = row[:, None] * inv
    ac = col[:, None] * inv
    cos_t = jnp.concatenate([jnp.cos(ar), jnp.cos(ar), jnp.cos(ac), jnp.cos(ac)], axis=-1)
    sin_t = jnp.concatenate([-jnp.sin(ar), jnp.sin(ar), -jnp.sin(ac), jnp.sin(ac)], axis=-1)
    return cos_t, sin_t


def _norm_rope(x, gain, cos_v, sin_v):
    quarter = HEAD_DIM // 4
    ms = jnp.mean(x * x, axis=-1, keepdims=True)
    xn = x * lax.rsqrt(ms + EPS) * gain
    lane = lax.broadcasted_iota(jnp.int32, xn.shape, 1)
    first = (lane % (2 * quarter)) < quarter
    partner = jnp.where(first, pltpu.roll(xn, HEAD_DIM - quarter, 1), pltpu.roll(xn, quarter, 1))
    return xn * cos_v + partner * sin_v


def _attention_kernel(q_ref, k_ref, v_ref, g_ref, qg_ref, kg_ref, cos_ref, sin_ref, o_ref,
                      kr_ref, vb_ref, *, seq, tq):
    n_blk = seq // tq
    kg = kg_ref[...]
    qg = qg_ref[...]
    scale = HEAD_DIM ** -0.5

    def prep_kv(c, carry):
        r0 = pl.multiple_of(c * tq, tq)
        rows = pl.ds(r0, tq)
        kr_ref[rows, :] = _norm_rope(k_ref[0, rows, :], kg, cos_ref[rows, :], sin_ref[rows, :]).astype(BF16)
        vb_ref[rows, :] = v_ref[0, rows, :].astype(BF16)
        return carry

    lax.fori_loop(0, n_blk, prep_kv, 0)

    for g in range(Q_GROUP):
        cols = slice(g * HEAD_DIM, (g + 1) * HEAD_DIM)

        def q_block(c, carry, cols=cols):
            r0 = pl.multiple_of(c * tq, tq)
            rows = pl.ds(r0, tq)
            q = _norm_rope(q_ref[0, rows, cols], qg, cos_ref[rows, :], sin_ref[rows, :])
            s = lax.dot_general(q.astype(BF16), kr_ref[...], (((1,), (1,)), ((), ())),
                                preferred_element_type=F32) * scale
            m = jnp.max(s, axis=-1, keepdims=True)
            p = jnp.exp(s - m)
            l = jnp.sum(p, axis=-1, keepdims=True)
            o = jnp.dot(p.astype(BF16), vb_ref[...], preferred_element_type=F32) / l
            gate = g_ref[0, rows, cols]
            o_ref[0, rows, cols] = (o * (gate * _sigmoid(gate))).astype(o_ref.dtype)
            return carry

        lax.fori_loop(0, n_blk, q_block, 0)


def _attention(proj, q_gain, k_gain, cos_t, sin_t, tq):
    b, s, _ = proj.shape
    gw = Q_GROUP * HEAD_DIM
    k_blk0 = ATTN_WIDTH // HEAD_DIM
    v_blk0 = (ATTN_WIDTH + KV_WIDTH) // HEAD_DIM
    g_blk0 = (ATTN_WIDTH + 2 * KV_WIDTH) // gw
    return pl.pallas_call(
        functools.partial(_attention_kernel, seq=s, tq=tq),
        out_shape=jax.ShapeDtypeStruct((b, s, ATTN_WIDTH), BF16),
        grid=(b, KV_HEADS),
        in_specs=[
            pl.BlockSpec((1, s, gw), lambda i, j: (i, 0, j)),
            pl.BlockSpec((1, s, HEAD_DIM), lambda i, j: (i, 0, k_blk0 + j)),
            pl.BlockSpec((1, s, HEAD_DIM), lambda i, j: (i, 0, v_blk0 + j)),
            pl.BlockSpec((1, s, gw), lambda i, j: (i, 0, g_blk0 + j)),
            pl.BlockSpec((1, HEAD_DIM), lambda i, j: (0, 0)),
            pl.BlockSpec((1, HEAD_DIM), lambda i, j: (0, 0)),
            pl.BlockSpec((s, HEAD_DIM), lambda i, j: (0, 0)),
            pl.BlockSpec((s, HEAD_DIM), lambda i, j: (0, 0)),
        ],
        out_specs=pl.BlockSpec((1, s, gw), lambda i, j: (i, 0, j)),
        scratch_shapes=[pltpu.VMEM((s, HEAD_DIM), BF16), pltpu.VMEM((s, HEAD_DIM), BF16)],
        compiler_params=_params("parallel", "parallel"),
        name="attention",
    )(proj, proj, proj, proj, q_gain, k_gain, cos_t, sin_t)


def _lru_kernel(x_ref, g_ref, cw_ref, cb_ref, w_ref, b_ref, lam_ref, o_ref,
                xpad_ref, af_ref, uf_ref, ab_ref, ub_ref, *, seq):
    seg = seq // SCAN_SEGMENTS
    pitch = seg + SUBLANES
    pad = SUBLANES
    a_refs = (af_ref, ab_ref)
    u_refs = (uf_ref, ub_ref)

    zeros = jnp.zeros((pad, LRU_BLOCK), F32)
    xpad_ref[0:pad, :] = zeros
    xpad_ref[pad + seq:pad + seq + pad, :] = zeros
    xpad_ref[pad:pad + seq, :] = x_ref[0]

    cw = cw_ref[...]
    cb = cb_ref[...]
    w = w_ref[0]
    bias = b_ref[0]
    log_sig_lam = _log_sigmoid(lam_ref[...])

    for k in range(SCAN_SEGMENTS):
        xc = cb
        for j in range(CONV_W):
            lo = pad + k * seg + j - CONV_LEFT
            xc = xc + cw[j:j + 1, :] * xpad_ref[lo:lo + seg, :]
        gates = jnp.dot(xc.astype(BF16), w, preferred_element_type=F32) + bias
        for d in range(2):
            c0 = 2 * d * LRU_BLOCK
            r = _sigmoid(gates[:, c0:c0 + LRU_BLOCK])
            i = _sigmoid(gates[:, c0 + LRU_BLOCK:c0 + 2 * LRU_BLOCK])
            log_a = LRU_C * r * log_sig_lam[d:d + 1, :]
            a = jnp.exp(log_a)
            one_minus_a2 = -jnp.tanh(log_a) * (a * a + 1.0)
            a_refs[d][k * pitch:k * pitch + seg, :] = a
            u_refs[d][k * pitch:k * pitch + seg, :] = jnp.sqrt(one_minus_a2) * (i * xc)

    def rows_at(t):
        return pl.ds(t, SCAN_SEGMENTS, stride=pitch)

    def pass1(t, carry):
        hf, pf, hb, pb = carry
        tb = seg - 1 - t
        a = af_ref[rows_at(t), :]
        hf = a * hf + uf_ref[rows_at(t), :]
        pf = pf * a
        uf_ref[rows_at(t), :] = hf
        af_ref[rows_at(t), :] = pf
        a = ab_ref[rows_at(tb), :]
        hb = a * hb + ub_ref[rows_at(tb), :]
        pb = pb * a
        ub_ref[rows_at(tb), :] = hb
        ab_ref[rows_at(tb), :] = pb
        return hf, pf, hb, pb

    zero = jnp.zeros((SCAN_SEGMENTS, LRU_BLOCK), F32)
    one = jnp.ones((SCAN_SEGMENTS, LRU_BLOCK), F32)
    hf, pf, hb, pb = lax.fori_loop(0, seg, pass1, (zero, one, zero, one))

    row = lax.broadcasted_iota(jnp.int32, (SCAN_SEGMENTS, LRU_BLOCK), 0)
    cf = zero
    c = jnp.zeros((1, LRU_BLOCK), F32)
    for k in range(SCAN_SEGMENTS):
        cf = jnp.where(row == k, c, cf)
        c = hf[k:k + 1, :] + pf[k:k + 1, :] * c
    cbk = zero
    c = jnp.zeros((1, LRU_BLOCK), F32)
    for k in reversed(range(SCAN_SEGMENTS)):
        cbk = jnp.where(row == k, c, cbk)
        c = hb[k:k + 1, :] + pb[k:k + 1, :] * c

    def pass2(t, carry):
        y = (uf_ref[rows_at(t), :] + af_ref[rows_at(t), :] * cf
             + ub_ref[rows_at(t), :] + ab_ref[rows_at(t), :] * cbk)
        uf_ref[rows_at(t), :] = y
        return carry

    lax.fori_loop(0, seg, pass2, 0)

    for k in range(SCAN_SEGMENTS):
        gate = g_ref[0, k * seg:(k + 1) * seg, :]
        y = uf_ref[k * pitch:k * pitch + seg, :]
        o_ref[0, k * seg:(k + 1) * seg, :] = (y * (gate * _sigmoid(gate))).astype(o_ref.dtype)


def _lru(proj, conv_w, conv_b, w_cat, b_cat, lam):
    b, s, _ = proj.shape
    x_blk0 = (2 * ATTN_WIDTH + 2 * KV_WIDTH) // LRU_BLOCK
    g_blk0 = x_blk0 + LRU_BLOCKS
    seg_rows = SCAN_SEGMENTS * (s // SCAN_SEGMENTS + SUBLANES)
    return pl.pallas_call(
        functools.partial(_lru_kernel, seq=s),
        out_shape=jax.ShapeDtypeStruct((b, s, LRU_WIDTH), BF16),
        grid=(b, LRU_BLOCKS),
        in_specs=[
            pl.BlockSpec((1, s, LRU_BLOCK), lambda i, n: (i, 0, x_blk0 + n)),
            pl.BlockSpec((1, s, LRU_BLOCK), lambda i, n: (i, 0, g_blk0 + n)),
            pl.BlockSpec((CONV_W, LRU_BLOCK), lambda i, n: (0, n)),
            pl.BlockSpec((1, LRU_BLOCK), lambda i, n: (0, n)),
            pl.BlockSpec((1, LRU_BLOCK, 4 * LRU_BLOCK), lambda i, n: (n, 0, 0)),
            pl.BlockSpec((1, 1, 4 * LRU_BLOCK), lambda i, n: (n, 0, 0)),
            pl.BlockSpec((2, LRU_BLOCK), lambda i, n: (0, n)),
        ],
        out_specs=pl.BlockSpec((1, s, LRU_BLOCK), lambda i, n: (i, 0, n)),
        scratch_shapes=[pltpu.VMEM((s + 2 * SUBLANES, LRU_BLOCK), F32)]
        + [pltpu.VMEM((seg_rows, LRU_BLOCK), F32) for _ in range(4)],
        compiler_params=_params("parallel", "parallel"),
        name="rg_lru",
    )(proj, proj, conv_w, conv_b, w_cat, b_cat, lam)


def _split_dot(mat, x):
    hi = x.astype(BF16)
    lo = (x - hi.astype(F32)).astype(BF16)
    return (jnp.dot(mat, hi, preferred_element_type=F32)
            + jnp.dot(mat, lo, preferred_element_type=F32))


def _mlstm_chunk(q, k, v, ig_col, lf_col, state, reverse):
    ct, n, m = state
    chunk = q.shape[0]
    scale = MLSTM_QK_DIM ** -0.5
    li = lax.broadcasted_iota(jnp.int32, (chunk, chunk), 0)
    si = lax.broadcasted_iota(jnp.int32, (chunk, chunk), 1)
    mask = (si >= li) if reverse else (si <= li)

    lf_b = jnp.broadcast_to(lf_col, (chunk, chunk))
    ig_b = jnp.broadcast_to(ig_col, (chunk, chunk))
    b_l = _split_dot(jnp.where(mask, 1.0, 0.0).astype(BF16), lf_b)
    b_col = b_l[:, 0:1]
    b_tot = b_l[0:1, 0:1] if reverse else b_l[chunk - 1:chunk, 0:1]
    logd = jnp.where(mask, b_l - b_l.T + ig_b.T, -jnp.inf)
    m_inter = b_col + m
    m_t = jnp.maximum(m_inter, jnp.max(logd, axis=-1, keepdims=True))
    qk = lax.dot_general(q, k, (((1,), (1,)), ((), ())), preferred_element_type=F32)
    sc = qk * scale * jnp.exp(logd - m_t)
    inter = jnp.exp(m_inter - m_t) * scale
    num = (jnp.dot(sc.astype(BF16), v, preferred_element_type=F32)
           + inter * jnp.dot(q, ct.astype(BF16), preferred_element_type=F32))
    qn = jnp.sum(q.astype(F32) * n, axis=-1, keepdims=True)
    den = jnp.sum(sc, axis=-1, keepdims=True) + inter * qn
    hout = num / jnp.maximum(jnp.abs(den), jnp.exp(-m_t))

    w_col = b_tot - b_col + ig_col
    m_new = jnp.maximum(b_tot + m, jnp.max(w_col, axis=0, keepdims=True))
    decay = jnp.exp(b_tot + m - m_new)
    ws = jnp.exp(w_col - m_new)
    kf = k.astype(F32)
    vw = (v.astype(F32) * ws).astype(BF16)
    ct_new = decay * ct + jnp.dot(kf.T.astype(BF16), vw, preferred_element_type=F32)
    n_new = decay * n + jnp.sum(ws * kf, axis=0, keepdims=True)
    return hout, (ct_new, n_new, m_new)


def _mlstm_kernel(q_ref, k_ref, v_ref, o_ref, z_ref, gt_ref, gb_ref, ng_ref, out_ref,
                  hf_ref, hb_ref, ctf_ref, ctb_ref, nf_ref, nb_ref, *, seq):
    chunk = MLSTM_CHUNK
    n_chunks = seq // chunk
    head = pl.program_id(1)
    lane = lax.broadcasted_iota(jnp.int32, (chunk, LANES), 1)
    gate_bias = gb_ref[...]

    def gate_col(rows, set_idx):
        g = gt_ref[0, rows, :] + gate_bias
        return jnp.sum(jnp.where(lane == set_idx * MLSTM_HEADS + head, g, 0.0), axis=-1, keepdims=True)

    ctf_ref[...] = jnp.zeros_like(ctf_ref)
    ctb_ref[...] = jnp.zeros_like(ctb_ref)
    nf_ref[...] = jnp.zeros_like(nf_ref)
    nb_ref[...] = jnp.zeros_like(nb_ref)

    def step(c, carry):
        m_f, m_b = carry
        rows_f = pl.ds(pl.multiple_of(c * chunk, chunk), chunk)
        rows_b = pl.ds(pl.multiple_of((n_chunks - 1 - c) * chunk, chunk), chunk)
        h_f, (ct, n, m_f) = _mlstm_chunk(
            q_ref[0, rows_f, :], k_ref[0, rows_f, :], v_ref[0, rows_f, :],
            gate_col(rows_f, 0), _log_sigmoid(gate_col(rows_f, 2)),
            (ctf_ref[...], nf_ref[...], m_f), reverse=False)
        hf_ref[rows_f, :] = h_f
        ctf_ref[...] = ct
        nf_ref[...] = n
        h_b, (ct, n, m_b) = _mlstm_chunk(
            q_ref[0, rows_b, :], k_ref[0, rows_b, :], v_ref[0, rows_b, :],
            gate_col(rows_b, 1), _log_sigmoid(gate_col(rows_b, 3)),
            (ctb_ref[...], nb_ref[...], m_b), reverse=True)
        hb_ref[rows_b, :] = h_b
        ctb_ref[...] = ct
        nb_ref[...] = n
        return m_f, m_b

    m0 = jnp.zeros((1, 1), F32)
    lax.fori_loop(0, n_chunks, step, (m0, m0))

    norm_gain = ng_ref[...]

    def finish(c, carry):
        rows = pl.ds(pl.multiple_of(c * chunk, chunk), chunk)
        hs = _sigmoid(o_ref[0, rows, :]) * (hf_ref[rows, :] + hb_ref[rows, :])
        ms = jnp.mean(hs * hs, axis=-1, keepdims=True)
        hs = hs * lax.rsqrt(ms + EPS) * norm_gain
        z = z_ref[0, rows, :]
        out_ref[0, rows, :] = (hs * (z * _sigmoid(z))).astype(out_ref.dtype)
        return carry

    lax.fori_loop(0, n_chunks, finish, 0)


def _mlstm(qkv, oz, gates, gate_bias, norm_gain):
    b, s, _ = qkv.shape
    dk, dv = MLSTM_QK_DIM, MLSTM_V_DIM
    k_blk0 = MLSTM_QK_WIDTH // dk
    v_blk0 = 2 * MLSTM_QK_WIDTH // dv
    z_blk0 = MLSTM_WIDTH // dv
    return pl.pallas_call(
        functools.partial(_mlstm_kernel, seq=s),
        out_shape=jax.ShapeDtypeStruct((b, s, MLSTM_WIDTH), BF16),
        grid=(b, MLSTM_HEADS),
        in_specs=[
            pl.BlockSpec((1, s, dk), lambda i, h: (i, 0, h)),
            pl.BlockSpec((1, s, dk), lambda i, h: (i, 0, k_blk0 + h)),
            pl.BlockSpec((1, s, dv), lambda i, h: (i, 0, v_blk0 + h)),
            pl.BlockSpec((1, s, dv), lambda i, h: (i, 0, h)),
            pl.BlockSpec((1, s, dv), lambda i, h: (i, 0, z_blk0 + h)),
            pl.BlockSpec((1, s, LANES), lambda i, h: (i, 0, 0)),
            pl.BlockSpec((1, LANES), lambda i, h: (0, 0)),
            pl.BlockSpec((1, dv), lambda i, h: (0, h)),
        ],
        out_specs=pl.BlockSpec((1, s, dv), lambda i, h: (i, 0, h)),
        scratch_shapes=[
            pltpu.VMEM((s, dv), F32), pltpu.VMEM((s, dv), F32),
            pltpu.VMEM((dk, dv), F32), pltpu.VMEM((dk, dv), F32),
            pltpu.VMEM((1, dk), F32), pltpu.VMEM((1, dk), F32),
        ],
        compiler_params=_params("parallel", "parallel"),
        name="mlstm",
    )(qkv, qkv, qkv, oz, oz, gates, gate_bias, norm_gain)


def _row_tile(t, want):
    return want if t % want == 0 else t


def kernel(x, norm_gain, final_gain, even_w_in, even_w_out, q_norm_gain, k_norm_gain, conv_w, conv_b,
           lru_wa, lru_ba, lru_wx, lru_bx, lru_lambda, odd_w_in, odd_gate_bias, odd_norm_gain, odd_w_out):
    b, s, d = x.shape
    t = b * s
    assert s % (SCAN_SEGMENTS * SUBLANES) == 0 and s % MLSTM_CHUNK == 0 and s % GRID_W == 0
    x2 = x.reshape(t, d)
    tn = 512

    proj0 = _norm_matmul(x2, norm_gain[0][None], even_w_in[0].astype(BF16), F32,
                         tm=_row_tile(t, 1024), tn=tn).reshape(b, s, EVEN_IN)
    cos_t, sin_t = _rope_tables(s)
    attn = _attention(proj0, q_norm_gain[0][None], k_norm_gain[0][None], cos_t, sin_t,
                      tq=_row_tile(s, 256))
    w_cat = jnp.concatenate([lru_wa[0, 0], lru_wx[0, 0], lru_wa[0, 1], lru_wx[0, 1]], axis=-1).astype(BF16)
    b_cat = jnp.concatenate([lru_ba[0, 0].reshape(LRU_BLOCKS, 1, LRU_BLOCK),
                             lru_bx[0, 0].reshape(LRU_BLOCKS, 1, LRU_BLOCK),
                             lru_ba[0, 1].reshape(LRU_BLOCKS, 1, LRU_BLOCK),
                             lru_bx[0, 1].reshape(LRU_BLOCKS, 1, LRU_BLOCK)], axis=-1)
    lru = _lru(proj0, conv_w[0], conv_b[0][None], w_cat, b_cat, lru_lambda[0])
    x1, hn1 = _out_proj0(attn.reshape(t, ATTN_WIDTH), lru.reshape(t, LRU_WIDTH),
                         even_w_out[0].astype(BF16), x2, norm_gain[1][None], tm=_row_tile(t, 512))

    w1 = odd_w_in[0]
    w1_main = w1.astype(BF16)
    qkv_w = 2 * MLSTM_QK_WIDTH + MLSTM_WIDTH
    oz_w = 2 * MLSTM_WIDTH
    tm1 = _row_tile(t, 1024)
    qkv = _matmul(hn1, w1_main, qkv_w, 0, BF16, tm1, tn, "in_proj1_qkv").reshape(b, s, qkv_w)
    oz = _matmul(hn1, w1_main, oz_w, qkv_w // tn, F32, tm1, tn, "in_proj1_oz").reshape(b, s, oz_w)
    n_gates = N_GATE_SETS * MLSTM_HEADS
    w_gates = jnp.pad(w1[:, qkv_w + oz_w:], ((0, 0), (0, LANES - n_gates))).astype(BF16)
    gates = _matmul(hn1, w_gates, LANES, 0, F32, tm1, LANES, "in_proj1_gates").reshape(b, s, LANES)
    gate_bias = jnp.pad(odd_gate_bias[0].reshape(1, n_gates), ((0, 0), (0, LANES - n_gates)))
    mix1 = _mlstm(qkv, oz, gates, gate_bias, odd_norm_gain[0][None])
    out = _out_proj1(mix1.reshape(t, MLSTM_WIDTH), odd_w_out[0].astype(BF16), x1, final_gain[None],
                     tm=_row_tile(t, 512))
    return out.reshape(b, s, d)
```

```python
import functools

import jax
import jax.numpy as jnp
from jax import lax
from jax.experimental import pallas as pl
from jax.experimental.pallas import tpu as pltpu

F32 = jnp.float32
BF16 = jnp.bfloat16

EPS = 1e-6
GRID_W = 64
HEAD_DIM = 128
ATTN_HEADS = 8
KV_HEADS = 2
Q_GROUP = ATTN_HEADS // KV_HEADS
ATTN_WIDTH = ATTN_HEADS * HEAD_DIM
KV_WIDTH = KV_HEADS * HEAD_DIM
ROPE_THETA = 10000.0
LRU_WIDTH = 1024
LRU_BLOCKS = 8
LRU_BLOCK = 128
LRU_C = 8.0
CONV_W = 4
CONV_LEFT = 2
EVEN_IN = ATTN_WIDTH + 2 * KV_WIDTH + ATTN_WIDTH + 2 * LRU_WIDTH
MLSTM_HEADS = 8
MLSTM_V_DIM = 256
MLSTM_QK_DIM = 128
MLSTM_WIDTH = MLSTM_HEADS * MLSTM_V_DIM
MLSTM_QK_WIDTH = MLSTM_HEADS * MLSTM_QK_DIM
MLSTM_CHUNK = 128
N_GATE_SETS = 4

LANES = 128
SUBLANES = 8
VMEM_LIMIT = 56 * 1024 * 1024
SCAN_SEGMENTS = SUBLANES


def _sigmoid(x):
    return 1.0 / (1.0 + jnp.exp(-x))


def _log_sigmoid(x):
    return jnp.minimum(x, 0.0) - jnp.log1p(jnp.exp(-jnp.abs(x)))


def _params(*sem):
    return pltpu.CompilerParams(dimension_semantics=sem, vmem_limit_bytes=VMEM_LIMIT)


def _rmsnorm_rows_to(x_ref, g_ref, dst_ref, rows, chunk):
    g = g_ref[...]

    def body(c, carry):
        r0 = pl.multiple_of(c * chunk, chunk)
        x = x_ref[pl.ds(r0, chunk), :]
        ms = jnp.mean(x * x, axis=-1, keepdims=True)
        dst_ref[pl.ds(r0, chunk), :] = (x * lax.rsqrt(ms + EPS) * g).astype(dst_ref.dtype)
        return carry

    lax.fori_loop(0, rows // chunk, body, 0)


def _norm_matmul_kernel(x_ref, g_ref, w_ref, o_ref, hn_ref, *, tm):
    @pl.when(pl.program_id(1) == 0)
    def _():
        _rmsnorm_rows_to(x_ref, g_ref, hn_ref, tm, 64)

    o_ref[...] = jnp.dot(hn_ref[...], w_ref[...], preferred_element_type=F32).astype(o_ref.dtype)


def _norm_matmul(x, gain, w, out_dtype, tm, tn):
    t, d = x.shape
    n = w.shape[1]
    return pl.pallas_call(
        functools.partial(_norm_matmul_kernel, tm=tm),
        out_shape=jax.ShapeDtypeStruct((t, n), out_dtype),
        grid=(t // tm, n // tn),
        in_specs=[
            pl.BlockSpec((tm, d), lambda i, j: (i, 0)),
            pl.BlockSpec((1, d), lambda i, j: (0, 0)),
            pl.BlockSpec((d, tn), lambda i, j: (0, j)),
        ],
        out_specs=pl.BlockSpec((tm, tn), lambda i, j: (i, j)),
        scratch_shapes=[pltpu.VMEM((tm, d), BF16)],
        compiler_params=_params("parallel", "arbitrary"),
        name="norm_in_proj",
    )(x, gain, w)


def _matmul_kernel(a_ref, w_ref, o_ref):
    o_ref[...] = jnp.dot(a_ref[...], w_ref[...], preferred_element_type=F32).astype(o_ref.dtype)


def _matmul(a, w, n, col_block0, out_dtype, tm, tn, name):
    t, k = a.shape
    return pl.pallas_call(
        _matmul_kernel,
        out_shape=jax.ShapeDtypeStruct((t, n), out_dtype),
        grid=(t // tm, n // tn),
        in_specs=[
            pl.BlockSpec((tm, k), lambda i, j: (i, 0)),
            pl.BlockSpec((k, tn), lambda i, j: (0, j + col_block0)),
        ],
        out_specs=pl.BlockSpec((tm, tn), lambda i, j: (i, j)),
        compiler_params=_params("parallel", "arbitrary"),
        name=name,
    )(a, w)


def _out_proj_epilogue(y_ref, x_ref, g_ref, xo_ref, no_ref, rows, chunk):
    g = g_ref[...]

    def body(c, carry):
        r0 = pl.multiple_of(c * chunk, chunk)
        xn = x_ref[pl.ds(r0, chunk), :] + y_ref[pl.ds(r0, chunk), :]
        if xo_ref is not None:
            xo_ref[pl.ds(r0, chunk), :] = xn
        ms = jnp.mean(xn * xn, axis=-1, keepdims=True)
        no_ref[pl.ds(r0, chunk), :] = (xn * lax.rsqrt(ms + EPS) * g).astype(no_ref.dtype)
        return carry

    lax.fori_loop(0, rows // chunk, body, 0)


def _out_proj0_kernel(a_ref, b_ref, wa_ref, wb_ref, x_ref, g_ref, xo_ref, no_ref, y_ref, *, tm):
    y_ref[...] = (jnp.dot(a_ref[...], wa_ref[...], preferred_element_type=F32)
                  + jnp.dot(b_ref[...], wb_ref[...], preferred_element_type=F32))
    _out_proj_epilogue(y_ref, x_ref, g_ref, xo_ref, no_ref, tm, 64)


def _out_proj0(a, b, w, x, gain, tm):
    t, ka = a.shape
    kb = b.shape[1]
    d = w.shape[1]
    return pl.pallas_call(
        functools.partial(_out_proj0_kernel, tm=tm),
        out_shape=(jax.ShapeDtypeStruct((t, d), F32), jax.ShapeDtypeStruct((t, d), BF16)),
        grid=(t // tm,),
        in_specs=[
            pl.BlockSpec((tm, ka), lambda i: (i, 0)),
            pl.BlockSpec((tm, kb), lambda i: (i, 0)),
            pl.BlockSpec((ka, d), lambda i: (0, 0)),
            pl.BlockSpec((kb, d), lambda i: (ka // kb, 0)),
            pl.BlockSpec((tm, d), lambda i: (i, 0)),
            pl.BlockSpec((1, d), lambda i: (0, 0)),
        ],
        out_specs=(pl.BlockSpec((tm, d), lambda i: (i, 0)), pl.BlockSpec((tm, d), lambda i: (i, 0))),
        scratch_shapes=[pltpu.VMEM((tm, d), F32)],
        compiler_params=_params("parallel"),
        name="out_proj0",
    )(a, b, w, w, x, gain)


def _out_proj1_kernel(a_ref, w_ref, x_ref, g_ref, no_ref, y_ref, *, tm):
    y_ref[...] = jnp.dot(a_ref[...], w_ref[...], preferred_element_type=F32)
    _out_proj_epilogue(y_ref, x_ref, g_ref, None, no_ref, tm, 64)


def _out_proj1(a, w, x, gain, tm):
    t, k = a.shape
    d = w.shape[1]
    return pl.pallas_call(
        functools.partial(_out_proj1_kernel, tm=tm),
        out_shape=jax.ShapeDtypeStruct((t, d), F32),
        grid=(t // tm,),
        in_specs=[
            pl.BlockSpec((tm, k), lambda i: (i, 0)),
            pl.BlockSpec((k, d), lambda i: (0, 0)),
            pl.BlockSpec((tm, d), lambda i: (i, 0)),
            pl.BlockSpec((1, d), lambda i: (0, 0)),
        ],
        out_specs=pl.BlockSpec((tm, d), lambda i: (i, 0)),
        scratch_shapes=[pltpu.VMEM((tm, d), F32)],
        compiler_params=_params("parallel"),
        name="out_proj1",
    )(a, w, x, gain)


def _head_lane_order():
    quarter = HEAD_DIM // 4
    idx = jnp.arange(HEAD_DIM).reshape(2, 2, quarter)
    return idx.transpose(1, 0, 2).reshape(HEAD_DIM)


def _rope_tables(seq_len):
    half = HEAD_DIM // 2
    t = jnp.arange(seq_len)
    row = (t // GRID_W).astype(F32)
    col = (t % GRID_W).astype(F32)
    inv = ROPE_THETA ** (-jnp.arange(0, half, 2, dtype=F32) / half)
    ar = row[:, None] * inv
    ac = col[:, None] * inv
    cos_t = jnp.concatenate([jnp.cos(ar), jnp.cos(ac), jnp.cos(ar), jnp.cos(ac)], axis=-1)
    sin_t = jnp.concatenate([-jnp.sin(ar), -jnp.sin(ac), jnp.sin(ar), jnp.sin(ac)], axis=-1)
    return cos_t, sin_t


def _norm_rope(x, gain, cos_v, sin_v):
    ms = jnp.mean(x * x, axis=-1, keepdims=True)
    xn = x * lax.rsqrt(ms + EPS) * gain
    return xn * cos_v + pltpu.roll(xn, HEAD_DIM // 2, 1) * sin_v


def _attention_kernel(q_ref, k_ref, v_ref, g_ref, qg_ref, kg_ref, cos_ref, sin_ref, o_ref,
                      kr_ref, vt_ref, qr_ref, s_ref, of_ref, *, seq, tq, kc):
    n_blk = seq // tq
    n_kc = seq // kc
    n_total = Q_GROUP * n_blk
    kg = kg_ref[...]
    qg = qg_ref[...]
    q_scale = HEAD_DIM ** -0.5 * 1.4426950408889634
    eye = jnp.where(lax.broadcasted_iota(jnp.int32, (HEAD_DIM, HEAD_DIM), 0)
                    == lax.broadcasted_iota(jnp.int32, (HEAD_DIM, HEAD_DIM), 1), 1.0, 0.0).astype(BF16)

    def prep(c, carry):
        rows = pl.ds(pl.multiple_of(c * tq, tq), tq)
        cos_v, sin_v = cos_ref[rows, :], sin_ref[rows, :]
        kr_ref[rows, :] = _norm_rope(k_ref[0, rows, :], kg, cos_v, sin_v).astype(BF16)
        vt_ref[:, rows] = lax.dot_general(eye, v_ref[0, rows, :].astype(BF16), (((1,), (1,)), ((), ())),
                                          preferred_element_type=F32).astype(BF16)
        for g in range(Q_GROUP):
            q = _norm_rope(q_ref[0, rows, g * HEAD_DIM:(g + 1) * HEAD_DIM], qg, cos_v, sin_v) * q_scale
            qr_ref[pl.ds(pl.multiple_of(g * seq + c * tq, tq), tq), :] = q.astype(BF16)
        return carry

    lax.fori_loop(0, n_blk, prep, 0, unroll=2 if n_blk % 2 == 0 else 1)

    def blk_rows(i):
        return pl.ds(pl.multiple_of(i * tq, tq), tq)

    def scores_chunk(i, j):
        return lax.dot_general(kr_ref[j * kc:(j + 1) * kc, :], qr_ref[blk_rows(i), :],
                               (((1,), (1,)), ((), ())), preferred_element_type=F32)

    def col_max8(x):
        return jnp.max(x.reshape(x.shape[0] // SUBLANES, SUBLANES, x.shape[1]), axis=0)

    def col_sum8(x):
        return jnp.sum(x.reshape(x.shape[0] // SUBLANES, SUBLANES, x.shape[1]), axis=0)

    neg_inf8 = jnp.full((SUBLANES, tq), -jnp.inf, F32)
    m8 = neg_inf8
    for j in range(n_kc):
        s_t = scores_chunk(0, j)
        s_ref[0, j] = s_t
        m8 = jnp.maximum(m8, col_max8(s_t))

    def block(i, m8_cur, slot):
        nxt = jnp.minimum(i + 1, n_total - 1)
        m_cur = jnp.max(m8_cur, axis=0, keepdims=True)
        m8_next = neg_inf8
        l8 = jnp.zeros((SUBLANES, tq), F32)
        acc = jnp.zeros((HEAD_DIM, tq), F32)
        for j in range(n_kc):
            s_next = scores_chunk(nxt, j)
            s_ref[1 - slot, j] = s_next
            m8_next = jnp.maximum(m8_next, col_max8(s_next))
            p_t = jnp.exp2(s_ref[slot, j] - m_cur)
            l8 = l8 + col_sum8(p_t)
            acc = acc + jnp.dot(vt_ref[:, j * kc:(j + 1) * kc], p_t.astype(BF16),
                                preferred_element_type=F32)
        l = jnp.sum(l8, axis=0, keepdims=True)
        of_ref[blk_rows(i), :] = (acc * (1.0 / l)).T
        return m8_next

    def block_pair(ii, m8_cur):
        return block(2 * ii + 1, block(2 * ii, m8_cur, 0), 1)

    lax.fori_loop(0, n_total // 2, block_pair, m8)

    def finish(c, carry):
        rows = pl.ds(pl.multiple_of(c * tq, tq), tq)
        for g in range(Q_GROUP):
            cols = slice(g * HEAD_DIM, (g + 1) * HEAD_DIM)
            gate = g_ref[0, rows, cols]
            o = of_ref[pl.ds(pl.multiple_of(g * seq + c * tq, tq), tq), :]
            o_ref[0, rows, cols] = (o * (gate * _sigmoid(gate))).astype(o_ref.dtype)
        return carry

    lax.fori_loop(0, n_blk, finish, 0)


def _attention(proj, q_gain, k_gain, cos_t, sin_t, tq):
    b, s, _ = proj.shape
    gw = Q_GROUP * HEAD_DIM
    k_blk0 = ATTN_WIDTH // HEAD_DIM
    v_blk0 = (ATTN_WIDTH + KV_WIDTH) // HEAD_DIM
    g_blk0 = (ATTN_WIDTH + 2 * KV_WIDTH) // gw
    kc = _row_tile(s, 512)
    return pl.pallas_call(
        functools.partial(_attention_kernel, seq=s, tq=tq, kc=kc),
        out_shape=jax.ShapeDtypeStruct((b, s, ATTN_WIDTH), BF16),
        grid=(b, KV_HEADS),
        in_specs=[
            pl.BlockSpec((1, s, gw), lambda i, j: (i, 0, j)),
            pl.BlockSpec((1, s, HEAD_DIM), lambda i, j: (i, 0, k_blk0 + j)),
            pl.BlockSpec((1, s, HEAD_DIM), lambda i, j: (i, 0, v_blk0 + j)),
            pl.BlockSpec((1, s, gw), lambda i, j: (i, 0, g_blk0 + j)),
            pl.BlockSpec((1, HEAD_DIM), lambda i, j: (0, 0)),
            pl.BlockSpec((1, HEAD_DIM), lambda i, j: (0, 0)),
            pl.BlockSpec((s, HEAD_DIM), lambda i, j: (0, 0)),
            pl.BlockSpec((s, HEAD_DIM), lambda i, j: (0, 0)),
        ],
        out_specs=pl.BlockSpec((1, s, gw), lambda i, j: (i, 0, j)),
        scratch_shapes=[
            pltpu.VMEM((s, HEAD_DIM), BF16),
            pltpu.VMEM((HEAD_DIM, s), BF16),
            pltpu.VMEM((Q_GROUP * s, HEAD_DIM), BF16),
            pltpu.VMEM((2, s // kc, kc, tq), F32),
            pltpu.VMEM((Q_GROUP * s, HEAD_DIM), F32),
        ],
        compiler_params=_params("parallel", "parallel"),
        name="attention",
    )(proj, proj, proj, proj, q_gain, k_gain, cos_t, sin_t)


def _lru_kernel(x_ref, g_ref, cw_ref, cb_ref, w_ref, b_ref, lam_ref, o_ref,
                xpad_ref, af_ref, uf_ref, ab_ref, ub_ref, *, seq):
    seg = seq // SCAN_SEGMENTS
    pitch = seg + SUBLANES
    pad = SUBLANES
    a_refs = (af_ref, ab_ref)
    u_refs = (uf_ref, ub_ref)

    zeros = jnp.zeros((pad, LRU_BLOCK), F32)
    xpad_ref[0:pad, :] = zeros
    xpad_ref[pad + seq:pad + seq + pad, :] = zeros
    xpad_ref[pad:pad + seq, :] = x_ref[0]

    cw = cw_ref[...]
    cb = cb_ref[...]
    w = w_ref[0]
    bias = b_ref[0]
    log_sig_lam = _log_sigmoid(lam_ref[...])

    for k in range(SCAN_SEGMENTS):
        xc = cb
        for j in range(CONV_W):
            lo = pad + k * seg + j - CONV_LEFT
            xc = xc + cw[j:j + 1, :] * xpad_ref[lo:lo + seg, :]
        gates = jnp.dot(xc.astype(BF16), w, preferred_element_type=F32) + bias
        for d in range(2):
            c0 = 2 * d * LRU_BLOCK
            r = _sigmoid(gates[:, c0:c0 + LRU_BLOCK])
            i = _sigmoid(gates[:, c0 + LRU_BLOCK:c0 + 2 * LRU_BLOCK])
            log_a = LRU_C * r * log_sig_lam[d:d + 1, :]
            a = jnp.exp(log_a)
            one_minus_a2 = -jnp.tanh(log_a) * (a * a + 1.0)
            a_refs[d][k * pitch:k * pitch + seg, :] = a
            u_refs[d][k * pitch:k * pitch + seg, :] = jnp.sqrt(one_minus_a2) * (i * xc)

    def rows_at(t):
        return pl.ds(t, SCAN_SEGMENTS, stride=pitch)

    def pass1(t, carry):
        hf, pf, hb, pb = carry
        tb = seg - 1 - t
        a = af_ref[rows_at(t), :]
        hf = a * hf + uf_ref[rows_at(t), :]
        pf = pf * a
        uf_ref[rows_at(t), :] = hf
        af_ref[rows_at(t), :] = pf
        a = ab_ref[rows_at(tb), :]
        hb = a * hb + ub_ref[rows_at(tb), :]
        pb = pb * a
        ub_ref[rows_at(tb), :] = hb
        ab_ref[rows_at(tb), :] = pb
        return hf, pf, hb, pb

    zero = jnp.zeros((SCAN_SEGMENTS, LRU_BLOCK), F32)
    one = jnp.ones((SCAN_SEGMENTS, LRU_BLOCK), F32)
    hf, pf, hb, pb = lax.fori_loop(0, seg, pass1, (zero, one, zero, one))

    row = lax.broadcasted_iota(jnp.int32, (SCAN_SEGMENTS, LRU_BLOCK), 0)
    cf = zero
    c = jnp.zeros((1, LRU_BLOCK), F32)
    for k in range(SCAN_SEGMENTS):
        cf = jnp.where(row == k, c, cf)
        c = hf[k:k + 1, :] + pf[k:k + 1, :] * c
    cbk = zero
    c = jnp.zeros((1, LRU_BLOCK), F32)
    for k in reversed(range(SCAN_SEGMENTS)):
        cbk = jnp.where(row == k, c, cbk)
        c = hb[k:k + 1, :] + pb[k:k + 1, :] * c

    def pass2(t, carry):
        y = (uf_ref[rows_at(t), :] + af_ref[rows_at(t), :] * cf
             + ub_ref[rows_at(t), :] + ab_ref[rows_at(t), :] * cbk)
        uf_ref[rows_at(t), :] = y
        return carry

    lax.fori_loop(0, seg, pass2, 0)

    for k in range(SCAN_SEGMENTS):
        gate = g_ref[0, k * seg:(k + 1) * seg, :]
        y = uf_ref[k * pitch:k * pitch + seg, :]
        o_ref[0, k * seg:(k + 1) * seg, :] = (y * (gate * _sigmoid(gate))).astype(o_ref.dtype)


def _lru(proj, conv_w, conv_b, w_cat, b_cat, lam):
    b, s, _ = proj.shape
    x_blk0 = (2 * ATTN_WIDTH + 2 * KV_WIDTH) // LRU_BLOCK
    g_blk0 = x_blk0 + LRU_BLOCKS
    seg_rows = SCAN_SEGMENTS * (s // SCAN_SEGMENTS + SUBLANES)
    return pl.pallas_call(
        functools.partial(_lru_kernel, seq=s),
        out_shape=jax.ShapeDtypeStruct((b, s, LRU_WIDTH), BF16),
        grid=(b, LRU_BLOCKS),
        in_specs=[
            pl.BlockSpec((1, s, LRU_BLOCK), lambda i, n: (i, 0, x_blk0 + n)),
            pl.BlockSpec((1, s, LRU_BLOCK), lambda i, n: (i, 0, g_blk0 + n)),
            pl.BlockSpec((CONV_W, LRU_BLOCK), lambda i, n: (0, n)),
            pl.BlockSpec((1, LRU_BLOCK), lambda i, n: (0, n)),
            pl.BlockSpec((1, LRU_BLOCK, 4 * LRU_BLOCK), lambda i, n: (n, 0, 0)),
            pl.BlockSpec((1, 1, 4 * LRU_BLOCK), lambda i, n: (n, 0, 0)),
            pl.BlockSpec((2, LRU_BLOCK), lambda i, n: (0, n)),
        ],
        out_specs=pl.BlockSpec((1, s, LRU_BLOCK), lambda i, n: (i, 0, n)),
        scratch_shapes=[pltpu.VMEM((s + 2 * SUBLANES, LRU_BLOCK), F32)]
        + [pltpu.VMEM((seg_rows, LRU_BLOCK), F32) for _ in range(4)],
        compiler_params=_params("parallel", "parallel"),
        name="rg_lru",
    )(proj, proj, conv_w, conv_b, w_cat, b_cat, lam)


def _hi_lo(x):
    hi = x.astype(BF16)
    lo = (x - hi.astype(F32)).astype(BF16)
    return hi, lo


def _dot_nt(a, b):
    return lax.dot_general(a, b, (((1,), (1,)), ((), ())), preferred_element_type=F32)


def _lane_tile(x, width):
    return jnp.concatenate([x] * (width // x.shape[1]), axis=1) if width != x.shape[1] else x


_G_LF, _G_C, _G_WS, _G_BTOT, _G_MW = range(5)


def _mlstm_gate_table(gate_ref, bias_ref, head, tab_ref):
    chunk = MLSTM_CHUNK
    ri = lax.broadcasted_iota(jnp.int32, (chunk, chunk), 0)
    si = lax.broadcasted_iota(jnp.int32, (chunk, chunk), 1)
    for d in range(2):
        ig = gate_ref[0, 0, d] + bias_ref[d, head]
        lf = _log_sigmoid(gate_ref[0, 0, 2 + d] + bias_ref[2 + d, head])
        upper = jnp.where((ri >= si) if d else (ri <= si), 1.0, 0.0).astype(BF16)
        hi, lo = _hi_lo(lf)
        b = (jnp.dot(hi, upper, preferred_element_type=F32)
             + jnp.dot(lo, upper, preferred_element_type=F32))
        b_tot = jnp.broadcast_to(b[:, 0:1] if d else b[:, chunk - 1:chunk], b.shape)
        w = b_tot - b + ig
        mw = jnp.broadcast_to(jnp.max(w, axis=-1, keepdims=True), b.shape)
        tab_ref[d, _G_LF] = lf
        tab_ref[d, _G_C] = ig - b
        tab_ref[d, _G_WS] = jnp.exp(w - mw)
        tab_ref[d, _G_BTOT] = b_tot
        tab_ref[d, _G_MW] = mw


def _chunk_mask(d, chunk):
    li = lax.broadcasted_iota(jnp.int32, (chunk, chunk), 0)
    si = lax.broadcasted_iota(jnp.int32, (chunk, chunk), 1)
    return (si >= li) if d else (si <= li), li == si


def _mlstm_intra_matmuls(d, ci, q, k, tab_ref):
    chunk = q.shape[0]
    mask, diag = _chunk_mask(d, chunk)
    eye = jnp.where(diag, 1.0, 0.0).astype(BF16)
    a = _dot_nt(jnp.concatenate([q, eye], axis=0), k)
    hi, lo = _hi_lo(tab_ref[d, _G_LF, pl.ds(ci, 1), :])
    hl = jnp.concatenate([jnp.broadcast_to(hi, (chunk, chunk)), jnp.broadcast_to(lo, (chunk, chunk))], axis=0)
    b2 = _dot_nt(jnp.where(mask, 1.0, 0.0).astype(BF16), hl)
    return a[:chunk], a[chunk:], b2[:, :chunk] + b2[:, chunk:]


def _mlstm_intra_weights(d, ci, qk, kt, b_l, tab_ref):
    mask, _ = _chunk_mask(d, qk.shape[0])
    row = pl.ds(ci, 1)
    logd = jnp.where(mask, b_l + tab_ref[d, _G_C, row, :], -jnp.inf)
    mi = jnp.broadcast_to(jnp.max(logd, axis=-1, keepdims=True), logd.shape)
    p0 = (qk * jnp.exp(logd - mi)).astype(BF16)
    ktw = (kt * tab_ref[d, _G_WS, row, :]).astype(BF16)
    return p0, ktw, mi


def _mlstm_state_matmuls(q, v, p0, ktw, c_state, n_state):
    ones = jnp.ones((q.shape[0], LANES), BF16)
    n0 = jnp.dot(p0, v, preferred_element_type=F32)
    r0 = jnp.dot(p0, ones, preferred_element_type=F32)
    kv0 = jnp.dot(ktw, v, preferred_element_type=F32)
    kn0 = jnp.dot(ktw, ones, preferred_element_type=F32)
    qc = jnp.dot(q, c_state.astype(BF16), preferred_element_type=F32)
    qn = jnp.dot(q, n_state.astype(BF16), preferred_element_type=F32)
    return n0, r0, kv0, kn0, qc, qn


def _mlstm_combine(b_l, mi, b_tot, mw, prods, c_state, n_state, m):
    n0, r0, kv0, kn0, qc, qn = prods
    dv = n0.shape[1]
    scale = MLSTM_QK_DIM ** -0.5
    m_inter = b_l + m
    m_t = jnp.maximum(m_inter, mi)
    e1 = jnp.exp(mi - m_t) * scale
    e2 = jnp.exp(m_inter - m_t) * scale
    den = e1 * r0 + e2 * qn
    inv = 1.0 / jnp.maximum(jnp.abs(den), jnp.exp(-m_t))
    hout = _lane_tile(e1 * inv, dv) * n0 + _lane_tile(e2 * inv, dv) * qc

    m_new = jnp.maximum(b_tot + m, mw)
    d1 = jnp.exp(b_tot + m - m_new)
    d2 = jnp.exp(mw - m_new)
    c_new = _lane_tile(d1, dv) * c_state + _lane_tile(d2, dv) * kv0
    n_new = d1 * n_state + d2 * kn0
    return hout, c_new, n_new, m_new


def _mlstm_kernel(bias_ref, q_ref, k_ref, v_ref, o_ref, z_ref, gate_ref, ng_ref, out_ref,
                  h_ref, c_ref, n_ref, tab_ref, pw_ref, pm_ref, *, seq):
    chunk = MLSTM_CHUNK
    n_chunks = seq // chunk
    _mlstm_gate_table(gate_ref, bias_ref, pl.program_id(1), tab_ref)
    c_ref[...] = jnp.zeros_like(c_ref)
    n_ref[...] = jnp.zeros_like(n_ref)

    def chunk_of(d, c):
        return (n_chunks - 1 - c) if d else c

    def rows_of(ci):
        return pl.ds(pl.multiple_of(ci * chunk, chunk), chunk)

    def intra_matmuls(d, c):
        ci = chunk_of(d, c)
        return _mlstm_intra_matmuls(d, ci, q_ref[0, rows_of(ci), :], k_ref[0, rows_of(ci), :], tab_ref)

    def park_weights(d, c, slot, parts):
        qk, kt, b_l = parts
        p0, ktw, mi = _mlstm_intra_weights(d, chunk_of(d, c), qk, kt, b_l, tab_ref)
        pw_ref[slot, d, 0] = p0
        pw_ref[slot, d, 1] = ktw
        pm_ref[slot, d, 0] = b_l
        pm_ref[slot, d, 1] = mi

    for d in range(2):
        park_weights(d, 0, 0, intra_matmuls(d, 0))

    def step(c, carry):
        ms = list(carry)
        slot = c % 2
        c_next = jnp.minimum(c + 1, n_chunks - 1)
        prods, parts = [], []
        for d in range(2):
            rows = rows_of(chunk_of(d, c))
            prods.append(_mlstm_state_matmuls(q_ref[0, rows, :], v_ref[0, rows, :],
                                              pw_ref[slot, d, 0], pw_ref[slot, d, 1], c_ref[d], n_ref[d]))
            parts.append(intra_matmuls(d, c_next))
        for d in range(2):
            ci = chunk_of(d, c)
            hout, c_new, n_new, ms[d] = _mlstm_combine(
                pm_ref[slot, d, 0], pm_ref[slot, d, 1],
                tab_ref[d, _G_BTOT, pl.ds(ci, 1), :], tab_ref[d, _G_MW, pl.ds(ci, 1), :],
                prods[d], c_ref[d], n_ref[d], ms[d])
            h_ref[d, rows_of(ci), :] = hout
            c_ref[d] = c_new
            n_ref[d] = n_new
            park_weights(d, c_next, 1 - slot, parts[d])
        return tuple(ms)

    m0 = jnp.zeros((1, LANES), F32)
    lax.fori_loop(0, n_chunks, step, (m0, m0), unroll=2)

    norm_gain = ng_ref[...]

    def finish(c, carry):
        rows = pl.ds(pl.multiple_of(c * chunk, chunk), chunk)
        hs = _sigmoid(o_ref[0, rows, :]) * (h_ref[0, rows, :] + h_ref[1, rows, :])
        ms = jnp.mean(hs * hs, axis=-1, keepdims=True)
        hs = hs * lax.rsqrt(ms + EPS) * norm_gain
        z = z_ref[0, rows, :]
        out_ref[0, rows, :] = (hs * (z * _sigmoid(z))).astype(out_ref.dtype)
        return carry

    lax.fori_loop(0, n_chunks, finish, 0, unroll=2)


def _mlstm(qkv, oz, gates_t, gate_bias, norm_gain):
    b, s, _ = qkv.shape
    dk, dv = MLSTM_QK_DIM, MLSTM_V_DIM
    n_chunks = s // MLSTM_CHUNK
    k_blk0 = MLSTM_QK_WIDTH // dk
    v_blk0 = 2 * MLSTM_QK_WIDTH // dv
    z_blk0 = MLSTM_WIDTH // dv
    return pl.pallas_call(
        functools.partial(_mlstm_kernel, seq=s),
        out_shape=jax.ShapeDtypeStruct((b, s, MLSTM_WIDTH), BF16),
        grid=(b, MLSTM_HEADS),
        in_specs=[
            pl.BlockSpec(memory_space=pltpu.SMEM),
            pl.BlockSpec((1, s, dk), lambda i, h: (i, 0, h)),
            pl.BlockSpec((1, s, dk), lambda i, h: (i, 0, k_blk0 + h)),
            pl.BlockSpec((1, s, dv), lambda i, h: (i, 0, v_blk0 + h)),
            pl.BlockSpec((1, s, dv), lambda i, h: (i, 0, h)),
            pl.BlockSpec((1, s, dv), lambda i, h: (i, 0, z_blk0 + h)),
            pl.BlockSpec((1, 1, N_GATE_SETS, n_chunks, MLSTM_CHUNK), lambda i, h: (i, h, 0, 0, 0)),
            pl.BlockSpec((1, dv), lambda i, h: (0, h)),
        ],
        out_specs=pl.BlockSpec((1, s, dv), lambda i, h: (i, 0, h)),
        scratch_shapes=[
            pltpu.VMEM((2, s, dv), F32),
            pltpu.VMEM((2, dk, dv), F32),
            pltpu.VMEM((2, dk, LANES), F32),
            pltpu.VMEM((2, 5, n_chunks, MLSTM_CHUNK), F32),
            pltpu.VMEM((2, 2, 2, MLSTM_CHUNK, MLSTM_CHUNK), BF16),
            pltpu.VMEM((2, 2, 2, MLSTM_CHUNK, MLSTM_CHUNK), F32),
        ],
        compiler_params=_params("parallel", "parallel"),
        name="mlstm",
    )(gate_bias, qkv, qkv, qkv, oz, oz, gates_t, norm_gain)


def _row_tile(t, want):
    return want if t % want == 0 else t


def kernel(x, norm_gain, final_gain, even_w_in, even_w_out, q_norm_gain, k_norm_gain, conv_w, conv_b,
           lru_wa, lru_ba, lru_wx, lru_bx, lru_lambda, odd_w_in, odd_gate_bias, odd_norm_gain, odd_w_out):
    b, s, d = x.shape
    t = b * s
    assert s % (SCAN_SEGMENTS * SUBLANES) == 0 and s % MLSTM_CHUNK == 0 and s % GRID_W == 0
    x2 = x.reshape(t, d)
    tn = 512

    order = _head_lane_order()
    qk_w = ATTN_WIDTH + KV_WIDTH
    w0 = even_w_in[0]
    w0_qk = w0[:, :qk_w].reshape(d, qk_w // HEAD_DIM, HEAD_DIM)[:, :, order].reshape(d, qk_w)
    w0 = jnp.concatenate([w0_qk, w0[:, qk_w:]], axis=1).astype(BF16)
    proj0 = _norm_matmul(x2, norm_gain[0][None], w0, F32,
                         tm=_row_tile(t, 1024), tn=tn).reshape(b, s, EVEN_IN)
    cos_t, sin_t = _rope_tables(s)
    attn = _attention(proj0, q_norm_gain[0][order][None], k_norm_gain[0][order][None], cos_t, sin_t,
                      tq=_row_tile(s, 256))
    w_cat = jnp.concatenate([lru_wa[0, 0], lru_wx[0, 0], lru_wa[0, 1], lru_wx[0, 1]], axis=-1).astype(BF16)
    b_cat = jnp.concatenate([lru_ba[0, 0].reshape(LRU_BLOCKS, 1, LRU_BLOCK),
                             lru_bx[0, 0].reshape(LRU_BLOCKS, 1, LRU_BLOCK),
                             lru_ba[0, 1].reshape(LRU_BLOCKS, 1, LRU_BLOCK),
                             lru_bx[0, 1].reshape(LRU_BLOCKS, 1, LRU_BLOCK)], axis=-1)
    lru = _lru(proj0, conv_w[0], conv_b[0][None], w_cat, b_cat, lru_lambda[0])
    x1, hn1 = _out_proj0(attn.reshape(t, ATTN_WIDTH), lru.reshape(t, LRU_WIDTH),
                         even_w_out[0].astype(BF16), x2, norm_gain[1][None], tm=_row_tile(t, 512))

    w1 = odd_w_in[0]
    w1_main = w1.astype(BF16)
    qkv_w = 2 * MLSTM_QK_WIDTH + MLSTM_WIDTH
    oz_w = 2 * MLSTM_WIDTH
    tm1 = _row_tile(t, 1024)
    qkv = _matmul(hn1, w1_main, qkv_w, 0, BF16, tm1, tn, "in_proj1_qkv").reshape(b, s, qkv_w)
    oz = _matmul(hn1, w1_main, oz_w, qkv_w // tn, F32, tm1, tn, "in_proj1_oz").reshape(b, s, oz_w)
    n_gates = N_GATE_SETS * MLSTM_HEADS
    w_gates = jnp.pad(w1[:, qkv_w + oz_w:], ((0, 0), (0, LANES - n_gates))).astype(BF16)
    gates = _matmul(hn1, w_gates, LANES, 0, F32, tm1, LANES, "in_proj1_gates")
    gates_t = gates[:, :n_gates].reshape(b, s, N_GATE_SETS, MLSTM_HEADS).transpose(0, 3, 2, 1)
    gates_t = gates_t.reshape(b, MLSTM_HEADS, N_GATE_SETS, s // MLSTM_CHUNK, MLSTM_CHUNK)
    mix1 = _mlstm(qkv, oz, gates_t, odd_gate_bias[0], odd_norm_gain[0][None])
    out = _out_proj1(mix1.reshape(t, MLSTM_WIDTH), odd_w_out[0].astype(BF16), x1, final_gain[None],
                     tm=_row_tile(t, 512))
    return out.reshape(b, s, d)
```

```python
import functools

import jax
import jax.numpy as jnp
from jax import lax
from jax.experimental import pallas as pl
from jax.experimental.pallas import tpu as pltpu

F32 = jnp.float32
BF16 = jnp.bfloat16

EPS = 1e-6
GRID_W = 64
HEAD_DIM = 128
ATTN_HEADS = 8
KV_HEADS = 2
Q_GROUP = ATTN_HEADS // KV_HEADS
ATTN_WIDTH = ATTN_HEADS * HEAD_DIM
KV_WIDTH = KV_HEADS * HEAD_DIM
ROPE_THETA = 10000.0
LRU_WIDTH = 1024
LRU_BLOCKS = 8
LRU_BLOCK = 128
LRU_C = 8.0
CONV_W = 4
CONV_LEFT = 2
EVEN_IN = ATTN_WIDTH + 2 * KV_WIDTH + ATTN_WIDTH + 2 * LRU_WIDTH
MLSTM_HEADS = 8
MLSTM_V_DIM = 256
MLSTM_QK_DIM = 128
MLSTM_WIDTH = MLSTM_HEADS * MLSTM_V_DIM
MLSTM_QK_WIDTH = MLSTM_HEADS * MLSTM_QK_DIM
MLSTM_CHUNK = 128
N_GATE_SETS = 4

LOG2_E = 1.4426950408889634

LANES = 128
SUBLANES = 8
VMEM_LIMIT = 56 * 1024 * 1024
SCAN_SEGMENTS = 4 * SUBLANES


def _sigmoid(x):
    return 1.0 / (1.0 + jnp.exp(-x))


def _log_sigmoid(x):
    return jnp.minimum(x, 0.0) - jnp.log1p(jnp.exp(-jnp.abs(x)))


def _params(*sem):
    return pltpu.CompilerParams(dimension_semantics=sem, vmem_limit_bytes=VMEM_LIMIT)


def _rmsnorm_rows_to(x_ref, g_ref, dst_ref, rows, chunk):
    g = g_ref[...]

    def body(c, carry):
        r0 = pl.multiple_of(c * chunk, chunk)
        x = x_ref[pl.ds(r0, chunk), :]
        ms = jnp.mean(x * x, axis=-1, keepdims=True)
        dst_ref[pl.ds(r0, chunk), :] = (x * lax.rsqrt(ms + EPS) * g).astype(dst_ref.dtype)
        return carry

    lax.fori_loop(0, rows // chunk, body, 0)


def _norm_matmul_kernel(x_ref, g_ref, w_ref, o_ref, hn_ref, *, tm):
    @pl.when(pl.program_id(1) == 0)
    def _():
        _rmsnorm_rows_to(x_ref, g_ref, hn_ref, tm, 64)

    o_ref[...] = jnp.dot(hn_ref[...], w_ref[...], preferred_element_type=F32).astype(o_ref.dtype)


def _norm_matmul(x, gain, w, out_dtype, tm, tn):
    t, d = x.shape
    n = w.shape[1]
    return pl.pallas_call(
        functools.partial(_norm_matmul_kernel, tm=tm),
        out_shape=jax.ShapeDtypeStruct((t, n), out_dtype),
        grid=(t // tm, n // tn),
        in_specs=[
            pl.BlockSpec((tm, d), lambda i, j: (i, 0)),
            pl.BlockSpec((1, d), lambda i, j: (0, 0)),
            pl.BlockSpec((d, tn), lambda i, j: (0, j)),
        ],
        out_specs=pl.BlockSpec((tm, tn), lambda i, j: (i, j)),
        scratch_shapes=[pltpu.VMEM((tm, d), BF16)],
        compiler_params=_params("parallel", "arbitrary"),
        name="norm_in_proj",
    )(x, gain, w)


def _matmul_kernel(a_ref, w_ref, o_ref):
    o_ref[...] = jnp.dot(a_ref[...], w_ref[...], preferred_element_type=F32).astype(o_ref.dtype)


def _matmul(a, w, n, col_block0, out_dtype, tm, tn, name):
    t, k = a.shape
    return pl.pallas_call(
        _matmul_kernel,
        out_shape=jax.ShapeDtypeStruct((t, n), out_dtype),
        grid=(t // tm, n // tn),
        in_specs=[
            pl.BlockSpec((tm, k), lambda i, j: (i, 0)),
            pl.BlockSpec((k, tn), lambda i, j: (0, j + col_block0)),
        ],
        out_specs=pl.BlockSpec((tm, tn), lambda i, j: (i, j)),
        compiler_params=_params("parallel", "arbitrary"),
        name=name,
    )(a, w)


def _out_proj_epilogue(y_ref, x_ref, g_ref, xo_ref, no_ref, rows, chunk):
    g = g_ref[...]

    def body(c, carry):
        r0 = pl.multiple_of(c * chunk, chunk)
        xn = x_ref[pl.ds(r0, chunk), :] + y_ref[pl.ds(r0, chunk), :]
        if xo_ref is not None:
            xo_ref[pl.ds(r0, chunk), :] = xn
        ms = jnp.mean(xn * xn, axis=-1, keepdims=True)
        no_ref[pl.ds(r0, chunk), :] = (xn * lax.rsqrt(ms + EPS) * g).astype(no_ref.dtype)
        return carry

    lax.fori_loop(0, rows // chunk, body, 0)


def _out_proj0_kernel(a_ref, b_ref, wa_ref, wb_ref, x_ref, g_ref, xo_ref, no_ref, y_ref, *, tm):
    y_ref[...] = (jnp.dot(a_ref[...], wa_ref[...], preferred_element_type=F32)
                  + jnp.dot(b_ref[...], wb_ref[...], preferred_element_type=F32))
    _out_proj_epilogue(y_ref, x_ref, g_ref, xo_ref, no_ref, tm, 64)


def _out_proj0(a, b, w, x, gain, tm):
    t, ka = a.shape
    kb = b.shape[1]
    d = w.shape[1]
    return pl.pallas_call(
        functools.partial(_out_proj0_kernel, tm=tm),
        out_shape=(jax.ShapeDtypeStruct((t, d), F32), jax.ShapeDtypeStruct((t, d), BF16)),
        grid=(t // tm,),
        in_specs=[
            pl.BlockSpec((tm, ka), lambda i: (i, 0)),
            pl.BlockSpec((tm, kb), lambda i: (i, 0)),
            pl.BlockSpec((ka, d), lambda i: (0, 0)),
            pl.BlockSpec((kb, d), lambda i: (ka // kb, 0)),
            pl.BlockSpec((tm, d), lambda i: (i, 0)),
            pl.BlockSpec((1, d), lambda i: (0, 0)),
        ],
        out_specs=(pl.BlockSpec((tm, d), lambda i: (i, 0)), pl.BlockSpec((tm, d), lambda i: (i, 0))),
        scratch_shapes=[pltpu.VMEM((tm, d), F32)],
        compiler_params=_params("parallel"),
        name="out_proj0",
    )(a, b, w, w, x, gain)


def _out_proj1_kernel(a_ref, w_ref, x_ref, g_ref, no_ref, y_ref, *, tm):
    y_ref[...] = jnp.dot(a_ref[...], w_ref[...], preferred_element_type=F32)
    _out_proj_epilogue(y_ref, x_ref, g_ref, None, no_ref, tm, 64)


def _out_proj1(a, w, x, gain, tm):
    t, k = a.shape
    d = w.shape[1]
    return pl.pallas_call(
        functools.partial(_out_proj1_kernel, tm=tm),
        out_shape=jax.ShapeDtypeStruct((t, d), F32),
        grid=(t // tm,),
        in_specs=[
            pl.BlockSpec((tm, k), lambda i: (i, 0)),
            pl.BlockSpec((k, d), lambda i: (0, 0)),
            pl.BlockSpec((tm, d), lambda i: (i, 0)),
            pl.BlockSpec((1, d), lambda i: (0, 0)),
        ],
        out_specs=pl.BlockSpec((tm, d), lambda i: (i, 0)),
        scratch_shapes=[pltpu.VMEM((tm, d), F32)],
        compiler_params=_params("parallel"),
        name="out_proj1",
    )(a, w, x, gain)


def _head_lane_order():
    quarter = HEAD_DIM // 4
    idx = jnp.arange(HEAD_DIM).reshape(2, 2, quarter)
    return idx.transpose(1, 0, 2).reshape(HEAD_DIM)


def _rope_tables(seq_len):
    half = HEAD_DIM // 2
    t = jnp.arange(seq_len)
    row = (t // GRID_W).astype(F32)
    col = (t % GRID_W).astype(F32)
    inv = ROPE_THETA ** (-jnp.arange(0, half, 2, dtype=F32) / half)
    ar = row[:, None] * inv
    ac = col[:, None] * inv
    cos_t = jnp.concatenate([jnp.cos(ar), jnp.cos(ac), jnp.cos(ar), jnp.cos(ac)], axis=-1)
    sin_t = jnp.concatenate([-jnp.sin(ar), -jnp.sin(ac), jnp.sin(ar), jnp.sin(ac)], axis=-1)
    return cos_t, sin_t


def _norm_rope(x, gain, cos_v, sin_v):
    ms = jnp.mean(x * x, axis=-1, keepdims=True)
    xn = x * lax.rsqrt(ms + EPS) * gain
    return xn * cos_v + pltpu.roll(xn, HEAD_DIM // 2, 1) * sin_v


def _attention_kernel(q_ref, k_ref, v_ref, g_ref, qg_ref, kg_ref, cos_ref, sin_ref, o_ref,
                      kr_ref, vt_ref, qr_ref, s_ref, of_ref, *, seq, tq, kc):
    n_blk = seq // tq
    n_kc = seq // kc
    n_total = Q_GROUP * n_blk
    kg = kg_ref[...]
    qg = qg_ref[...]
    q_scale = HEAD_DIM ** -0.5 * LOG2_E
    eye = jnp.where(lax.broadcasted_iota(jnp.int32, (HEAD_DIM, HEAD_DIM), 0)
                    == lax.broadcasted_iota(jnp.int32, (HEAD_DIM, HEAD_DIM), 1), 1.0, 0.0).astype(BF16)

    def prep(c, carry):
        rows = pl.ds(pl.multiple_of(c * tq, tq), tq)
        cos_v, sin_v = cos_ref[rows, :], sin_ref[rows, :]
        kr_ref[rows, :] = _norm_rope(k_ref[0, rows, :], kg, cos_v, sin_v).astype(BF16)
        vt_ref[:, rows] = lax.dot_general(eye, v_ref[0, rows, :].astype(BF16), (((1,), (1,)), ((), ())),
                                          preferred_element_type=F32).astype(BF16)
        for g in range(Q_GROUP):
            q = _norm_rope(q_ref[0, rows, g * HEAD_DIM:(g + 1) * HEAD_DIM], qg, cos_v, sin_v) * q_scale
            qr_ref[pl.ds(pl.multiple_of(g * seq + c * tq, tq), tq), :] = q.astype(BF16)
        return carry

    lax.fori_loop(0, n_blk, prep, 0, unroll=2 if n_blk % 2 == 0 else 1)

    def blk_rows(i):
        return pl.ds(pl.multiple_of(i * tq, tq), tq)

    def scores_chunk(i, j):
        return lax.dot_general(kr_ref[j * kc:(j + 1) * kc, :], qr_ref[blk_rows(i), :],
                               (((1,), (1,)), ((), ())), preferred_element_type=F32)

    def col_max8(x):
        return jnp.max(x.reshape(x.shape[0] // SUBLANES, SUBLANES, x.shape[1]), axis=0)

    def col_sum8(x):
        return jnp.sum(x.reshape(x.shape[0] // SUBLANES, SUBLANES, x.shape[1]), axis=0)

    neg_inf8 = jnp.full((SUBLANES, tq), -jnp.inf, F32)
    m8 = neg_inf8
    for j in range(n_kc):
        s_t = scores_chunk(0, j)
        s_ref[0, j] = s_t
        m8 = jnp.maximum(m8, col_max8(s_t))

    def block(i, m8_cur, slot):
        nxt = jnp.minimum(i + 1, n_total - 1)
        m_cur = jnp.max(m8_cur, axis=0, keepdims=True)
        m8_next = neg_inf8
        l8 = jnp.zeros((SUBLANES, tq), F32)
        acc = jnp.zeros((HEAD_DIM, tq), F32)
        for j in range(n_kc):
            s_next = scores_chunk(nxt, j)
            s_ref[1 - slot, j] = s_next
            m8_next = jnp.maximum(m8_next, col_max8(s_next))
            p_t = jnp.exp2(s_ref[slot, j] - m_cur)
            l8 = l8 + col_sum8(p_t)
            acc = acc + jnp.dot(vt_ref[:, j * kc:(j + 1) * kc], p_t.astype(BF16),
                                preferred_element_type=F32)
        l = jnp.sum(l8, axis=0, keepdims=True)
        of_ref[blk_rows(i), :] = (acc * (1.0 / l)).T
        return m8_next

    def block_pair(ii, m8_cur):
        return block(2 * ii + 1, block(2 * ii, m8_cur, 0), 1)

    lax.fori_loop(0, n_total // 2, block_pair, m8)

    def finish(c, carry):
        rows = pl.ds(pl.multiple_of(c * tq, tq), tq)
        for g in range(Q_GROUP):
            cols = slice(g * HEAD_DIM, (g + 1) * HEAD_DIM)
            gate = g_ref[0, rows, cols]
            o = of_ref[pl.ds(pl.multiple_of(g * seq + c * tq, tq), tq), :]
            o_ref[0, rows, cols] = (o * (gate * _sigmoid(gate))).astype(o_ref.dtype)
        return carry

    lax.fori_loop(0, n_blk, finish, 0)


def _attention(proj, q_gain, k_gain, cos_t, sin_t, tq):
    b, s, _ = proj.shape
    gw = Q_GROUP * HEAD_DIM
    k_blk0 = ATTN_WIDTH // HEAD_DIM
    v_blk0 = (ATTN_WIDTH + KV_WIDTH) // HEAD_DIM
    g_blk0 = (ATTN_WIDTH + 2 * KV_WIDTH) // gw
    kc = _row_tile(s, 512)
    return pl.pallas_call(
        functools.partial(_attention_kernel, seq=s, tq=tq, kc=kc),
        out_shape=jax.ShapeDtypeStruct((b, s, ATTN_WIDTH), BF16),
        grid=(b, KV_HEADS),
        in_specs=[
            pl.BlockSpec((1, s, gw), lambda i, j: (i, 0, j)),
            pl.BlockSpec((1, s, HEAD_DIM), lambda i, j: (i, 0, k_blk0 + j)),
            pl.BlockSpec((1, s, HEAD_DIM), lambda i, j: (i, 0, v_blk0 + j)),
            pl.BlockSpec((1, s, gw), lambda i, j: (i, 0, g_blk0 + j)),
            pl.BlockSpec((1, HEAD_DIM), lambda i, j: (0, 0)),
            pl.BlockSpec((1, HEAD_DIM), lambda i, j: (0, 0)),
            pl.BlockSpec((s, HEAD_DIM), lambda i, j: (0, 0)),
            pl.BlockSpec((s, HEAD_DIM), lambda i, j: (0, 0)),
        ],
        out_specs=pl.BlockSpec((1, s, gw), lambda i, j: (i, 0, j)),
        scratch_shapes=[
            pltpu.VMEM((s, HEAD_DIM), BF16),
            pltpu.VMEM((HEAD_DIM, s), BF16),
            pltpu.VMEM((Q_GROUP * s, HEAD_DIM), BF16),
            pltpu.VMEM((2, s // kc, kc, tq), F32),
            pltpu.VMEM((Q_GROUP * s, HEAD_DIM), F32),
        ],
        compiler_params=_params("parallel", "parallel"),
        name="attention",
    )(proj, proj, proj, proj, q_gain, k_gain, cos_t, sin_t)


def _lru_kernel(x_ref, g_ref, cw_ref, cb_ref, w_ref, lam_ref, o_ref,
                xpad_ref, af_ref, uf_ref, ab_ref, ub_ref, hf_ref, pf_ref, hb_ref, pb_ref,
                *, seq, n_row_chunks):
    seg = seq // SCAN_SEGMENTS + SUBLANES // 2
    rows_padded = SCAN_SEGMENTS * seg
    chunk = seq // n_row_chunks
    pad = SUBLANES
    a_refs = (af_ref, ab_ref)
    u_refs = (uf_ref, ub_ref)

    zeros = jnp.zeros((pad, LRU_BLOCK), F32)
    xpad_ref[0:pad, :] = zeros
    xpad_ref[pad + seq:pad + seq + pad, :] = zeros
    xpad_ref[pad:pad + seq, :] = x_ref[0]
    for ref in a_refs + u_refs:
        ref[seq:rows_padded, :] = jnp.zeros((rows_padded - seq, LRU_BLOCK), F32)

    cw = cw_ref[...]
    cb = cb_ref[...]
    w = w_ref[0]
    decay = (-0.5 * LRU_C) * _log_sigmoid(lam_ref[...])
    bias_taps = jnp.where(lax.broadcasted_iota(jnp.int32, (chunk, LRU_BLOCK), 1) < 2, 1.0, 0.0).astype(BF16)

    for k in range(n_row_chunks):
        half_xc = cb
        for j in range(CONV_W):
            lo = pad + k * chunk + j - CONV_LEFT
            half_xc = half_xc + cw[j:j + 1, :] * xpad_ref[lo:lo + chunk, :]
        lhs = jnp.concatenate([half_xc.astype(BF16), bias_taps], axis=1)
        th = jnp.tanh(jnp.dot(lhs, w, preferred_element_type=F32))
        for d in range(2):
            c0 = 2 * d * LRU_BLOCK
            th_r = th[:, c0:c0 + LRU_BLOCK]
            th_i = th[:, c0 + LRU_BLOCK:c0 + 2 * LRU_BLOCK]
            dec = decay[d:d + 1, :]
            neg_log_a = dec * th_r + dec
            ix = half_xc * th_i + half_xc
            a = jnp.exp2(neg_log_a * -LOG2_E)
            one_minus_a2 = jnp.tanh(neg_log_a) * (a * a + 1.0)
            a_refs[d][k * chunk:(k + 1) * chunk, :] = a
            u_refs[d][k * chunk:(k + 1) * chunk, :] = jnp.exp2(0.5 * jnp.log2(one_minus_a2)) * ix

    n_groups = SCAN_SEGMENTS // SUBLANES

    def rows_at(t, grp):
        return pl.ds(t + grp * SUBLANES * seg, SUBLANES, stride=seg)

    def pass1(t, carry):
        hf, pf, hb, pb = (list(c) for c in carry)
        tb = seg - 1 - t
        for grp in range(n_groups):
            a = af_ref[rows_at(t, grp), :]
            hf[grp] = a * hf[grp] + uf_ref[rows_at(t, grp), :]
            pf[grp] = pf[grp] * a
            hf_ref[rows_at(t, grp), :] = hf[grp]
            pf_ref[rows_at(t, grp), :] = pf[grp]
            a = ab_ref[rows_at(tb, grp), :]
            hb[grp] = a * hb[grp] + ub_ref[rows_at(tb, grp), :]
            pb[grp] = pb[grp] * a
            hb_ref[rows_at(tb, grp), :] = hb[grp]
            pb_ref[rows_at(tb, grp), :] = pb[grp]
        return tuple(hf), tuple(pf), tuple(hb), tuple(pb)

    unroll = 2 if seg % 2 == 0 else 1
    zero = (jnp.zeros((SUBLANES, LRU_BLOCK), F32),) * n_groups
    one = (jnp.ones((SUBLANES, LRU_BLOCK), F32),) * n_groups
    hf, pf, hb, pb = lax.fori_loop(0, seg, pass1, (zero, one, zero, one), unroll=unroll)

    row = lax.broadcasted_iota(jnp.int32, (SUBLANES, LRU_BLOCK), 0)

    def entering_states(h_end, p_end, order):
        tiles = [jnp.zeros((SUBLANES, LRU_BLOCK), F32)] * n_groups
        c = jnp.zeros((1, LRU_BLOCK), F32)
        for k in order:
            grp, sub = divmod(k, SUBLANES)
            tiles[grp] = jnp.where(row == sub, c, tiles[grp])
            c = h_end[grp][sub:sub + 1, :] + p_end[grp][sub:sub + 1, :] * c
        return tiles

    cf = entering_states(hf, pf, range(SCAN_SEGMENTS))
    cbk = entering_states(hb, pb, reversed(range(SCAN_SEGMENTS)))

    def pass2(t, carry):
        for grp in range(n_groups):
            y = (hf_ref[rows_at(t, grp), :] + pf_ref[rows_at(t, grp), :] * cf[grp]
                 + hb_ref[rows_at(t, grp), :] + pb_ref[rows_at(t, grp), :] * cbk[grp])
            uf_ref[rows_at(t, grp), :] = y
        return carry

    lax.fori_loop(0, seg, pass2, 0, unroll=unroll)

    for k in range(n_row_chunks):
        rows = slice(k * chunk, (k + 1) * chunk)
        half_gate = g_ref[0, rows, :]
        silu = half_gate * jnp.tanh(half_gate) + half_gate
        o_ref[0, rows, :] = (uf_ref[rows, :] * silu).astype(o_ref.dtype)


def _lru_gate_weights(wa, ba, wx, bx):
    w_cat = jnp.concatenate([wa[0], wx[0], wa[1], wx[1]], axis=-1).astype(BF16)
    half_b = 0.5 * jnp.concatenate([ba[0].reshape(LRU_BLOCKS, 1, LRU_BLOCK), bx[0].reshape(LRU_BLOCKS, 1, LRU_BLOCK),
                                    ba[1].reshape(LRU_BLOCKS, 1, LRU_BLOCK), bx[1].reshape(LRU_BLOCKS, 1, LRU_BLOCK)],
                                   axis=-1)
    hi = half_b.astype(BF16)
    lo = (half_b - hi.astype(F32)).astype(BF16)
    fill = jnp.zeros((LRU_BLOCKS, LRU_BLOCK - 2, 4 * LRU_BLOCK), BF16)
    return jnp.concatenate([w_cat, hi, lo, fill], axis=1)


def _lru(proj, half_conv_w, half_conv_b, w_gates, lam):
    b, s, _ = proj.shape
    x_blk0 = (2 * ATTN_WIDTH + 2 * KV_WIDTH) // LRU_BLOCK
    g_blk0 = x_blk0 + LRU_BLOCKS
    seg_rows = s + SCAN_SEGMENTS * (SUBLANES // 2)
    n_row_chunks = 8 if s % (8 * SUBLANES) == 0 else 1
    return pl.pallas_call(
        functools.partial(_lru_kernel, seq=s, n_row_chunks=n_row_chunks),
        out_shape=jax.ShapeDtypeStruct((b, s, LRU_WIDTH), BF16),
        grid=(b, LRU_BLOCKS),
        in_specs=[
            pl.BlockSpec((1, s, LRU_BLOCK), lambda i, n: (i, 0, x_blk0 + n)),
            pl.BlockSpec((1, s, LRU_BLOCK), lambda i, n: (i, 0, g_blk0 + n)),
            pl.BlockSpec((CONV_W, LRU_BLOCK), lambda i, n: (0, n)),
            pl.BlockSpec((1, LRU_BLOCK), lambda i, n: (0, n)),
            pl.BlockSpec((1, 2 * LRU_BLOCK, 4 * LRU_BLOCK), lambda i, n: (n, 0, 0)),
            pl.BlockSpec((2, LRU_BLOCK), lambda i, n: (0, n)),
        ],
        out_specs=pl.BlockSpec((1, s, LRU_BLOCK), lambda i, n: (i, 0, n)),
        scratch_shapes=[pltpu.VMEM((s + 2 * SUBLANES, LRU_BLOCK), F32)]
        + [pltpu.VMEM((seg_rows, LRU_BLOCK), F32) for _ in range(8)],
        compiler_params=_params("parallel", "parallel"),
        name="rg_lru",
    )(proj, proj, half_conv_w, half_conv_b, w_gates, lam)


def _hi_lo(x):
    hi = x.astype(BF16)
    lo = (x - hi.astype(F32)).astype(BF16)
    return hi, lo


def _dot_nt(a, b):
    return lax.dot_general(a, b, (((1,), (1,)), ((), ())), preferred_element_type=F32)


def _lane_tile(x, width):
    return jnp.concatenate([x] * (width // x.shape[1]), axis=1) if width != x.shape[1] else x


_G_LF, _G_C, _G_WS, _G_BTOT, _G_MW = range(5)


def _mlstm_gate_table(gate_ref, bias_ref, head, tab_ref):
    chunk = MLSTM_CHUNK
    ri = lax.broadcasted_iota(jnp.int32, (chunk, chunk), 0)
    si = lax.broadcasted_iota(jnp.int32, (chunk, chunk), 1)
    for d in range(2):
        ig = gate_ref[0, 0, d] + bias_ref[d, head]
        lf = _log_sigmoid(gate_ref[0, 0, 2 + d] + bias_ref[2 + d, head])
        upper = jnp.where((ri >= si) if d else (ri <= si), 1.0, 0.0).astype(BF16)
        hi, lo = _hi_lo(lf)
        b = (jnp.dot(hi, upper, preferred_element_type=F32)
             + jnp.dot(lo, upper, preferred_element_type=F32))
        b_tot = jnp.broadcast_to(b[:, 0:1] if d else b[:, chunk - 1:chunk], b.shape)
        w = b_tot - b + ig
        mw = jnp.broadcast_to(jnp.max(w, axis=-1, keepdims=True), b.shape)
        tab_ref[d, _G_LF] = lf
        tab_ref[d, _G_C] = ig - b
        tab_ref[d, _G_WS] = jnp.exp(w - mw)
        tab_ref[d, _G_BTOT] = b_tot
        tab_ref[d, _G_MW] = mw


def _chunk_mask(d, chunk):
    li = lax.broadcasted_iota(jnp.int32, (chunk, chunk), 0)
    si = lax.broadcasted_iota(jnp.int32, (chunk, chunk), 1)
    return (si >= li) if d else (si <= li), li == si


def _mlstm_intra_matmuls(d, ci, q, k, tab_ref):
    chunk = q.shape[0]
    mask, diag = _chunk_mask(d, chunk)
    eye = jnp.where(diag, 1.0, 0.0).astype(BF16)
    a = _dot_nt(jnp.concatenate([q, eye], axis=0), k)
    hi, lo = _hi_lo(tab_ref[d, _G_LF, pl.ds(ci, 1), :])
    hl = jnp.concatenate([jnp.broadcast_to(hi, (chunk, chunk)), jnp.broadcast_to(lo, (chunk, chunk))], axis=0)
    b2 = _dot_nt(jnp.where(mask, 1.0, 0.0).astype(BF16), hl)
    return a[:chunk], a[chunk:], b2[:, :chunk] + b2[:, chunk:]


def _mlstm_intra_weights(d, ci, qk, kt, b_l, tab_ref):
    mask, _ = _chunk_mask(d, qk.shape[0])
    row = pl.ds(ci, 1)
    logd = jnp.where(mask, b_l + tab_ref[d, _G_C, row, :], -jnp.inf)
    mi = jnp.broadcast_to(jnp.max(logd, axis=-1, keepdims=True), logd.shape)
    p0 = (qk * jnp.exp(logd - mi)).astype(BF16)
    ktw = (kt * tab_ref[d, _G_WS, row, :]).astype(BF16)
    return p0, ktw, mi


def _mlstm_state_matmuls(q, v, p0, ktw, c_state, n_state):
    ones = jnp.ones((q.shape[0], LANES), BF16)
    n0 = jnp.dot(p0, v, preferred_element_type=F32)
    r0 = jnp.dot(p0, ones, preferred_element_type=F32)
    kv0 = jnp.dot(ktw, v, preferred_element_type=F32)
    kn0 = jnp.dot(ktw, ones, preferred_element_type=F32)
    qc = jnp.dot(q, c_state.astype(BF16), preferred_element_type=F32)
    qn = jnp.dot(q, n_state.astype(BF16), preferred_element_type=F32)
    return n0, r0, kv0, kn0, qc, qn


def _mlstm_combine(b_l, mi, b_tot, mw, prods, c_state, n_state, m):
    n0, r0, kv0, kn0, qc, qn = prods
    dv = n0.shape[1]
    scale = MLSTM_QK_DIM ** -0.5
    m_inter = b_l + m
    m_t = jnp.maximum(m_inter, mi)
    e1 = jnp.exp(mi - m_t) * scale
    e2 = jnp.exp(m_inter - m_t) * scale
    den = e1 * r0 + e2 * qn
    inv = 1.0 / jnp.maximum(jnp.abs(den), jnp.exp(-m_t))
    hout = _lane_tile(e1 * inv, dv) * n0 + _lane_tile(e2 * inv, dv) * qc

    m_new = jnp.maximum(b_tot + m, mw)
    d1 = jnp.exp(b_tot + m - m_new)
    d2 = jnp.exp(mw - m_new)
    c_new = _lane_tile(d1, dv) * c_state + _lane_tile(d2, dv) * kv0
    n_new = d1 * n_state + d2 * kn0
    return hout, c_new, n_new, m_new


def _mlstm_kernel(bias_ref, q_ref, k_ref, v_ref, o_ref, z_ref, gate_ref, ng_ref, out_ref,
                  h_ref, c_ref, n_ref, tab_ref, pw_ref, pm_ref, *, seq):
    chunk = MLSTM_CHUNK
    n_chunks = seq // chunk
    _mlstm_gate_table(gate_ref, bias_ref, pl.program_id(1), tab_ref)
    c_ref[...] = jnp.zeros_like(c_ref)
    n_ref[...] = jnp.zeros_like(n_ref)

    def chunk_of(d, c):
        return (n_chunks - 1 - c) if d else c

    def rows_of(ci):
        return pl.ds(pl.multiple_of(ci * chunk, chunk), chunk)

    def intra_matmuls(d, c):
        ci = chunk_of(d, c)
        return _mlstm_intra_matmuls(d, ci, q_ref[0, rows_of(ci), :], k_ref[0, rows_of(ci), :], tab_ref)

    def park_weights(d, c, slot, parts):
        qk, kt, b_l = parts
        p0, ktw, mi = _mlstm_intra_weights(d, chunk_of(d, c), qk, kt, b_l, tab_ref)
        pw_ref[slot, d, 0] = p0
        pw_ref[slot, d, 1] = ktw
        pm_ref[slot, d, 0] = b_l
        pm_ref[slot, d, 1] = mi

    for d in range(2):
        park_weights(d, 0, 0, intra_matmuls(d, 0))

    def step(c, carry):
        ms = list(carry)
        slot = c % 2
        c_next = jnp.minimum(c + 1, n_chunks - 1)
        prods, parts = [], []
        for d in range(2):
            rows = rows_of(chunk_of(d, c))
            prods.append(_mlstm_state_matmuls(q_ref[0, rows, :], v_ref[0, rows, :],
                                              pw_ref[slot, d, 0], pw_ref[slot, d, 1], c_ref[d], n_ref[d]))
            parts.append(intra_matmuls(d, c_next))
        for d in range(2):
            ci = chunk_of(d, c)
            hout, c_new, n_new, ms[d] = _mlstm_combine(
                pm_ref[slot, d, 0], pm_ref[slot, d, 1],
                tab_ref[d, _G_BTOT, pl.ds(ci, 1), :], tab_ref[d, _G_MW, pl.ds(ci, 1), :],
                prods[d], c_ref[d], n_ref[d], ms[d])
            h_ref[d, rows_of(ci), :] = hout
            c_ref[d] = c_new
            n_ref[d] = n_new
            park_weights(d, c_next, 1 - slot, parts[d])
        return tuple(ms)

    m0 = jnp.zeros((1, LANES), F32)
    lax.fori_loop(0, n_chunks, step, (m0, m0), unroll=2)

    norm_gain = ng_ref[...]

    def finish(c, carry):
        rows = pl.ds(pl.multiple_of(c * chunk, chunk), chunk)
        hs = _sigmoid(o_ref[0, rows, :]) * (h_ref[0, rows, :] + h_ref[1, rows, :])
        ms = jnp.mean(hs * hs, axis=-1, keepdims=True)
        hs = hs * lax.rsqrt(ms + EPS) * norm_gain
        z = z_ref[0, rows, :]
        out_ref[0, rows, :] = (hs * (z * _sigmoid(z))).astype(out_ref.dtype)
        return carry

    lax.fori_loop(0, n_chunks, finish, 0, unroll=2)


def _mlstm(qkv, oz, gates_t, gate_bias, norm_gain):
    b, s, _ = qkv.shape
    dk, dv = MLSTM_QK_DIM, MLSTM_V_DIM
    n_chunks = s // MLSTM_CHUNK
    k_blk0 = MLSTM_QK_WIDTH // dk
    v_blk0 = 2 * MLSTM_QK_WIDTH // dv
    z_blk0 = MLSTM_WIDTH // dv
    return pl.pallas_call(
        functools.partial(_mlstm_kernel, seq=s),
        out_shape=jax.ShapeDtypeStruct((b, s, MLSTM_WIDTH), BF16),
        grid=(b, MLSTM_HEADS),
        in_specs=[
            pl.BlockSpec(memory_space=pltpu.SMEM),
            pl.BlockSpec((1, s, dk), lambda i, h: (i, 0, h)),
            pl.BlockSpec((1, s, dk), lambda i, h: (i, 0, k_blk0 + h)),
            pl.BlockSpec((1, s, dv), lambda i, h: (i, 0, v_blk0 + h)),
            pl.BlockSpec((1, s, dv), lambda i, h: (i, 0, h)),
            pl.BlockSpec((1, s, dv), lambda i, h: (i, 0, z_blk0 + h)),
            pl.BlockSpec((1, 1, N_GATE_SETS, n_chunks, MLSTM_CHUNK), lambda i, h: (i, h, 0, 0, 0)),
            pl.BlockSpec((1, dv), lambda i, h: (0, h)),
        ],
        out_specs=pl.BlockSpec((1, s, dv), lambda i, h: (i, 0, h)),
        scratch_shapes=[
            pltpu.VMEM((2, s, dv), F32),
            pltpu.VMEM((2, dk, dv), F32),
            pltpu.VMEM((2, dk, LANES), F32),
            pltpu.VMEM((2, 5, n_chunks, MLSTM_CHUNK), F32),
            pltpu.VMEM((2, 2, 2, MLSTM_CHUNK, MLSTM_CHUNK), BF16),
            pltpu.VMEM((2, 2, 2, MLSTM_CHUNK, MLSTM_CHUNK), F32),
        ],
        compiler_params=_params("parallel", "parallel"),
        name="mlstm",
    )(gate_bias, qkv, qkv, qkv, oz, oz, gates_t, norm_gain)


def _row_tile(t, want):
    return want if t % want == 0 else t


def kernel(x, norm_gain, final_gain, even_w_in, even_w_out, q_norm_gain, k_norm_gain, conv_w, conv_b,
           lru_wa, lru_ba, lru_wx, lru_bx, lru_lambda, odd_w_in, odd_gate_bias, odd_norm_gain, odd_w_out):
    b, s, d = x.shape
    t = b * s
    assert s % (SCAN_SEGMENTS * SUBLANES) == 0 and s % MLSTM_CHUNK == 0 and s % GRID_W == 0
    x2 = x.reshape(t, d)
    tn = 512

    order = _head_lane_order()
    qk_w = ATTN_WIDTH + KV_WIDTH
    w0 = even_w_in[0]
    w0_qk = w0[:, :qk_w].reshape(d, qk_w // HEAD_DIM, HEAD_DIM)[:, :, order].reshape(d, qk_w)
    g_lru0 = EVEN_IN - LRU_WIDTH
    w0 = jnp.concatenate([w0_qk, w0[:, qk_w:g_lru0], 0.5 * w0[:, g_lru0:]], axis=1).astype(BF16)
    proj0 = _norm_matmul(x2, norm_gain[0][None], w0, F32,
                         tm=_row_tile(t, 1024), tn=tn).reshape(b, s, EVEN_IN)
    cos_t, sin_t = _rope_tables(s)
    attn = _attention(proj0, q_norm_gain[0][order][None], k_norm_gain[0][order][None], cos_t, sin_t,
                      tq=_row_tile(s, 256))
    lru = _lru(proj0, 0.5 * conv_w[0], 0.5 * conv_b[0][None],
               _lru_gate_weights(lru_wa[0], lru_ba[0], lru_wx[0], lru_bx[0]), lru_lambda[0])
    x1, hn1 = _out_proj0(attn.reshape(t, ATTN_WIDTH), lru.reshape(t, LRU_WIDTH),
                         even_w_out[0].astype(BF16), x2, norm_gain[1][None], tm=_row_tile(t, 512))

    w1 = odd_w_in[0]
    w1_main = w1.astype(BF16)
    qkv_w = 2 * MLSTM_QK_WIDTH + MLSTM_WIDTH
    oz_w = 2 * MLSTM_WIDTH
    tm1 = _row_tile(t, 1024)
    qkv = _matmul(hn1, w1_main, qkv_w, 0, BF16, tm1, tn, "in_proj1_qkv").reshape(b, s, qkv_w)
    oz = _matmul(hn1, w1_main, oz_w, qkv_w // tn, F32, tm1, tn, "in_proj1_oz").reshape(b, s, oz_w)
    n_gates = N_GATE_SETS * MLSTM_HEADS
    w_gates = jnp.pad(w1[:, qkv_w + oz_w:], ((0, 0), (0, LANES - n_gates))).astype(BF16)
    gates = _matmul(hn1, w_gates, LANES, 0, F32, tm1, LANES, "in_proj1_gates")
    gates_t = gates[:, :n_gates].reshape(b, s, N_GATE_SETS, MLSTM_HEADS).transpose(0, 3, 2, 1)
    gates_t = gates_t.reshape(b, MLSTM_HEADS, N_GATE_SETS, s // MLSTM_CHUNK, MLSTM_CHUNK)
    mix1 = _mlstm(qkv, oz, gates_t, odd_gate_bias[0], odd_norm_gain[0][None])
    out = _out_proj1(mix1.reshape(t, MLSTM_WIDTH), odd_w_out[0].astype(BF16), x1, final_gain[None],
                     tm=_row_tile(t, 512))
    return out.reshape(b, s, d)
```

```python
import functools

import jax
import jax.numpy as jnp
from jax import lax
from jax.experimental import pallas as pl
from jax.experimental.pallas import tpu as pltpu

F32 = jnp.float32
BF16 = jnp.bfloat16

EPS = 1e-6
GRID_W = 64
HEAD_DIM = 128
ATTN_HEADS = 8
KV_HEADS = 2
Q_GROUP = ATTN_HEADS // KV_HEADS
ATTN_WIDTH = ATTN_HEADS * HEAD_DIM
KV_WIDTH = KV_HEADS * HEAD_DIM
ROPE_THETA = 10000.0
LRU_WIDTH = 1024
LRU_BLOCKS = 8
LRU_BLOCK = 128
LRU_C = 8.0
CONV_W = 4
CONV_LEFT = 2
EVEN_IN = ATTN_WIDTH + 2 * KV_WIDTH + ATTN_WIDTH + 2 * LRU_WIDTH
MLSTM_HEADS = 8
MLSTM_V_DIM = 256
MLSTM_QK_DIM = 128
MLSTM_WIDTH = MLSTM_HEADS * MLSTM_V_DIM
MLSTM_QK_WIDTH = MLSTM_HEADS * MLSTM_QK_DIM
MLSTM_CHUNK = 128
N_GATE_SETS = 4

LOG2_E = 1.4426950408889634

LANES = 128
SUBLANES = 8
VMEM_LIMIT = 56 * 1024 * 1024
SCAN_SEGMENTS = 4 * SUBLANES


def _sigmoid(x):
    return 1.0 / (1.0 + jnp.exp(-x))


def _log_sigmoid(x):
    return jnp.minimum(x, 0.0) - jnp.log1p(jnp.exp(-jnp.abs(x)))


def _params(*sem):
    return pltpu.CompilerParams(dimension_semantics=sem, vmem_limit_bytes=VMEM_LIMIT)


def _rmsnorm_rows_to(x_ref, g_ref, dst_ref, rows, chunk):
    g = g_ref[...]

    def body(c, carry):
        r0 = pl.multiple_of(c * chunk, chunk)
        x = x_ref[pl.ds(r0, chunk), :]
        ms = jnp.mean(x * x, axis=-1, keepdims=True)
        dst_ref[pl.ds(r0, chunk), :] = (x * lax.rsqrt(ms + EPS) * g).astype(dst_ref.dtype)
        return carry

    lax.fori_loop(0, rows // chunk, body, 0)


def _norm_matmul_kernel(x_ref, g_ref, w_ref, o_ref, hn_ref, *, tm):
    @pl.when(pl.program_id(1) == 0)
    def _():
        _rmsnorm_rows_to(x_ref, g_ref, hn_ref, tm, 64)

    o_ref[...] = jnp.dot(hn_ref[...], w_ref[...], preferred_element_type=F32).astype(o_ref.dtype)


def _norm_matmul(x, gain, w, out_dtype, tm, tn):
    t, d = x.shape
    n = w.shape[1]
    return pl.pallas_call(
        functools.partial(_norm_matmul_kernel, tm=tm),
        out_shape=jax.ShapeDtypeStruct((t, n), out_dtype),
        grid=(t // tm, n // tn),
        in_specs=[
            pl.BlockSpec((tm, d), lambda i, j: (i, 0)),
            pl.BlockSpec((1, d), lambda i, j: (0, 0)),
            pl.BlockSpec((d, tn), lambda i, j: (0, j)),
        ],
        out_specs=pl.BlockSpec((tm, tn), lambda i, j: (i, j)),
        scratch_shapes=[pltpu.VMEM((tm, d), BF16)],
        compiler_params=_params("parallel", "arbitrary"),
        name="norm_in_proj",
    )(x, gain, w)


def _matmul_kernel(a_ref, w_ref, o_ref):
    o_ref[...] = jnp.dot(a_ref[...], w_ref[...], preferred_element_type=F32).astype(o_ref.dtype)


def _matmul(a, w, n, col_block0, out_dtype, tm, tn, name):
    t, k = a.shape
    return pl.pallas_call(
        _matmul_kernel,
        out_shape=jax.ShapeDtypeStruct((t, n), out_dtype),
        grid=(t // tm, n // tn),
        in_specs=[
            pl.BlockSpec((tm, k), lambda i, j: (i, 0)),
            pl.BlockSpec((k, tn), lambda i, j: (0, j + col_block0)),
        ],
        out_specs=pl.BlockSpec((tm, tn), lambda i, j: (i, j)),
        compiler_params=_params("parallel", "arbitrary"),
        name=name,
    )(a, w)


def _matmul_t_kernel(wt_ref, a_ref, o_ref):
    o_ref[...] = lax.dot_general(wt_ref[...], a_ref[...], (((1,), (1,)), ((), ())),
                                 preferred_element_type=F32)


def _matmul_t(wt, a, tm, name):
    n, k = wt.shape
    t = a.shape[0]
    return pl.pallas_call(
        _matmul_t_kernel,
        out_shape=jax.ShapeDtypeStruct((n, t), F32),
        grid=(t // tm,),
        in_specs=[
            pl.BlockSpec((n, k), lambda i: (0, 0)),
            pl.BlockSpec((tm, k), lambda i: (i, 0)),
        ],
        out_specs=pl.BlockSpec((n, tm), lambda i: (0, i)),
        compiler_params=_params("parallel"),
        name=name,
    )(wt, a)


def _out_proj_epilogue(y_ref, x_ref, g_ref, xo_ref, no_ref, rows, chunk):
    g = g_ref[...]

    def body(c, carry):
        r0 = pl.multiple_of(c * chunk, chunk)
        xn = x_ref[pl.ds(r0, chunk), :] + y_ref[pl.ds(r0, chunk), :]
        if xo_ref is not None:
            xo_ref[pl.ds(r0, chunk), :] = xn
        ms = jnp.mean(xn * xn, axis=-1, keepdims=True)
        no_ref[pl.ds(r0, chunk), :] = (xn * lax.rsqrt(ms + EPS) * g).astype(no_ref.dtype)
        return carry

    lax.fori_loop(0, rows // chunk, body, 0)


def _out_proj0_kernel(a_ref, b_ref, wa_ref, wb_ref, x_ref, g_ref, xo_ref, no_ref, y_ref, *, tm):
    y_ref[...] = (jnp.dot(a_ref[...], wa_ref[...], preferred_element_type=F32)
                  + jnp.dot(b_ref[...], wb_ref[...], preferred_element_type=F32))
    _out_proj_epilogue(y_ref, x_ref, g_ref, xo_ref, no_ref, tm, 64)


def _out_proj0(a, b, w, x, gain, tm):
    t, ka = a.shape
    kb = b.shape[1]
    d = w.shape[1]
    return pl.pallas_call(
        functools.partial(_out_proj0_kernel, tm=tm),
        out_shape=(jax.ShapeDtypeStruct((t, d), F32), jax.ShapeDtypeStruct((t, d), BF16)),
        grid=(t // tm,),
        in_specs=[
            pl.BlockSpec((tm, ka), lambda i: (i, 0)),
            pl.BlockSpec((tm, kb), lambda i: (i, 0)),
            pl.BlockSpec((ka, d), lambda i: (0, 0)),
            pl.BlockSpec((kb, d), lambda i: (ka // kb, 0)),
            pl.BlockSpec((tm, d), lambda i: (i, 0)),
            pl.BlockSpec((1, d), lambda i: (0, 0)),
        ],
        out_specs=(pl.BlockSpec((tm, d), lambda i: (i, 0)), pl.BlockSpec((tm, d), lambda i: (i, 0))),
        scratch_shapes=[pltpu.VMEM((tm, d), F32)],
        compiler_params=_params("parallel"),
        name="out_proj0",
    )(a, b, w, w, x, gain)


def _out_proj1_kernel(a_ref, w_ref, x_ref, g_ref, no_ref, y_ref, *, tm):
    y_ref[...] = jnp.dot(a_ref[...], w_ref[...], preferred_element_type=F32)
    _out_proj_epilogue(y_ref, x_ref, g_ref, None, no_ref, tm, 64)


def _out_proj1(a, w, x, gain, tm):
    t, k = a.shape
    d = w.shape[1]
    return pl.pallas_call(
        functools.partial(_out_proj1_kernel, tm=tm),
        out_shape=jax.ShapeDtypeStruct((t, d), F32),
        grid=(t // tm,),
        in_specs=[
            pl.BlockSpec((tm, k), lambda i: (i, 0)),
            pl.BlockSpec((k, d), lambda i: (0, 0)),
            pl.BlockSpec((tm, d), lambda i: (i, 0)),
            pl.BlockSpec((1, d), lambda i: (0, 0)),
        ],
        out_specs=pl.BlockSpec((tm, d), lambda i: (i, 0)),
        scratch_shapes=[pltpu.VMEM((tm, d), F32)],
        compiler_params=_params("parallel"),
        name="out_proj1",
    )(a, w, x, gain)


def _head_lane_order():
    quarter = HEAD_DIM // 4
    idx = jnp.arange(HEAD_DIM).reshape(2, 2, quarter)
    return idx.transpose(1, 0, 2).reshape(HEAD_DIM)


def _rope_tables(seq_len):
    half = HEAD_DIM // 2
    t = jnp.arange(seq_len)
    row = (t // GRID_W).astype(F32)
    col = (t % GRID_W).astype(F32)
    inv = ROPE_THETA ** (-jnp.arange(0, half, 2, dtype=F32) / half)
    ar = row[:, None] * inv
    ac = col[:, None] * inv
    cos_t = jnp.concatenate([jnp.cos(ar), jnp.cos(ac), jnp.cos(ar), jnp.cos(ac)], axis=-1)
    sin_t = jnp.concatenate([-jnp.sin(ar), -jnp.sin(ac), jnp.sin(ar), jnp.sin(ac)], axis=-1)
    return cos_t, sin_t


def _norm_rope(x, gain, cos_v, sin_v):
    ms = jnp.mean(x * x, axis=-1, keepdims=True)
    xn = x * lax.rsqrt(ms + EPS) * gain
    return xn * cos_v + pltpu.roll(xn, HEAD_DIM // 2, 1) * sin_v


def _attention_kernel(q_ref, k_ref, v_ref, g_ref, qg_ref, kg_ref, cos_ref, sin_ref, o_ref,
                      kr_ref, vt_ref, qr_ref, s_ref, of_ref, *, seq, tq, kc):
    n_blk = seq // tq
    n_kc = seq // kc
    n_total = Q_GROUP * n_blk
    kg = kg_ref[...]
    qg = qg_ref[...]
    q_scale = HEAD_DIM ** -0.5 * LOG2_E
    eye = jnp.where(lax.broadcasted_iota(jnp.int32, (HEAD_DIM, HEAD_DIM), 0)
                    == lax.broadcasted_iota(jnp.int32, (HEAD_DIM, HEAD_DIM), 1), 1.0, 0.0).astype(BF16)

    def prep(c, carry):
        rows = pl.ds(pl.multiple_of(c * tq, tq), tq)
        cos_v, sin_v = cos_ref[rows, :], sin_ref[rows, :]
        kr_ref[rows, :] = _norm_rope(k_ref[0, rows, :], kg, cos_v, sin_v).astype(BF16)
        vt_ref[:, rows] = lax.dot_general(eye, v_ref[0, rows, :].astype(BF16), (((1,), (1,)), ((), ())),
                                          preferred_element_type=F32).astype(BF16)
        for g in range(Q_GROUP):
            q = _norm_rope(q_ref[0, rows, g * HEAD_DIM:(g + 1) * HEAD_DIM], qg, cos_v, sin_v) * q_scale
            qr_ref[pl.ds(pl.multiple_of(g * seq + c * tq, tq), tq), :] = q.astype(BF16)
        return carry

    lax.fori_loop(0, n_blk, prep, 0, unroll=2 if n_blk % 2 == 0 else 1)

    def blk_rows(i):
        return pl.ds(pl.multiple_of(i * tq, tq), tq)

    def scores_chunk(i, j):
        return lax.dot_general(kr_ref[j * kc:(j + 1) * kc, :], qr_ref[blk_rows(i), :],
                               (((1,), (1,)), ((), ())), preferred_element_type=F32)

    def col_max8(x):
        return jnp.max(x.reshape(x.shape[0] // SUBLANES, SUBLANES, x.shape[1]), axis=0)

    def col_sum8(x):
        return jnp.sum(x.reshape(x.shape[0] // SUBLANES, SUBLANES, x.shape[1]), axis=0)

    neg_inf8 = jnp.full((SUBLANES, tq), -jnp.inf, F32)
    m8 = neg_inf8
    for j in range(n_kc):
        s_t = scores_chunk(0, j)
        s_ref[0, j] = s_t
        m8 = jnp.maximum(m8, col_max8(s_t))

    def block(i, m8_cur, slot):
        nxt = jnp.minimum(i + 1, n_total - 1)
        m_cur = jnp.max(m8_cur, axis=0, keepdims=True)
        m8_next = neg_inf8
        l8 = jnp.zeros((SUBLANES, tq), F32)
        acc = jnp.zeros((HEAD_DIM, tq), F32)
        for j in range(n_kc):
            s_next = scores_chunk(nxt, j)
            s_ref[1 - slot, j] = s_next
            m8_next = jnp.maximum(m8_next, col_max8(s_next))
            p_t = jnp.exp2(s_ref[slot, j] - m_cur)
            l8 = l8 + col_sum8(p_t)
            acc = acc + jnp.dot(vt_ref[:, j * kc:(j + 1) * kc], p_t.astype(BF16),
                                preferred_element_type=F32)
        l = jnp.sum(l8, axis=0, keepdims=True)
        of_ref[blk_rows(i), :] = (acc * (1.0 / l)).T
        return m8_next

    def block_pair(ii, m8_cur):
        return block(2 * ii + 1, block(2 * ii, m8_cur, 0), 1)

    lax.fori_loop(0, n_total // 2, block_pair, m8)

    def finish(c, carry):
        rows = pl.ds(pl.multiple_of(c * tq, tq), tq)
        for g in range(Q_GROUP):
            cols = slice(g * HEAD_DIM, (g + 1) * HEAD_DIM)
            gate = g_ref[0, rows, cols]
            o = of_ref[pl.ds(pl.multiple_of(g * seq + c * tq, tq), tq), :]
            o_ref[0, rows, cols] = (o * (gate * _sigmoid(gate))).astype(o_ref.dtype)
        return carry

    lax.fori_loop(0, n_blk, finish, 0)


def _attention(proj, q_gain, k_gain, cos_t, sin_t, tq):
    b, s, _ = proj.shape
    gw = Q_GROUP * HEAD_DIM
    k_blk0 = ATTN_WIDTH // HEAD_DIM
    v_blk0 = (ATTN_WIDTH + KV_WIDTH) // HEAD_DIM
    g_blk0 = (ATTN_WIDTH + 2 * KV_WIDTH) // gw
    kc = _row_tile(s, 512)
    return pl.pallas_call(
        functools.partial(_attention_kernel, seq=s, tq=tq, kc=kc),
        out_shape=jax.ShapeDtypeStruct((b, s, ATTN_WIDTH), BF16),
        grid=(b, KV_HEADS),
        in_specs=[
            pl.BlockSpec((1, s, gw), lambda i, j: (i, 0, j)),
            pl.BlockSpec((1, s, HEAD_DIM), lambda i, j: (i, 0, k_blk0 + j)),
            pl.BlockSpec((1, s, HEAD_DIM), lambda i, j: (i, 0, v_blk0 + j)),
            pl.BlockSpec((1, s, gw), lambda i, j: (i, 0, g_blk0 + j)),
            pl.BlockSpec((1, HEAD_DIM), lambda i, j: (0, 0)),
            pl.BlockSpec((1, HEAD_DIM), lambda i, j: (0, 0)),
            pl.BlockSpec((s, HEAD_DIM), lambda i, j: (0, 0)),
            pl.BlockSpec((s, HEAD_DIM), lambda i, j: (0, 0)),
        ],
        out_specs=pl.BlockSpec((1, s, gw), lambda i, j: (i, 0, j)),
        scratch_shapes=[
            pltpu.VMEM((s, HEAD_DIM), BF16),
            pltpu.VMEM((HEAD_DIM, s), BF16),
            pltpu.VMEM((Q_GROUP * s, HEAD_DIM), BF16),
            pltpu.VMEM((2, s // kc, kc, tq), F32),
            pltpu.VMEM((Q_GROUP * s, HEAD_DIM), F32),
        ],
        compiler_params=_params("parallel", "parallel"),
        name="attention",
    )(proj, proj, proj, proj, q_gain, k_gain, cos_t, sin_t)


def _lru_kernel(x_ref, g_ref, cw_ref, cb_ref, w_ref, lam_ref, o_ref,
                xpad_ref, af_ref, uf_ref, ab_ref, ub_ref, hf_ref, pf_ref, hb_ref, pb_ref,
                *, seq, n_row_chunks):
    seg = seq // SCAN_SEGMENTS + SUBLANES // 2
    rows_padded = SCAN_SEGMENTS * seg
    chunk = seq // n_row_chunks
    pad = SUBLANES
    a_refs = (af_ref, ab_ref)
    u_refs = (uf_ref, ub_ref)

    zeros = jnp.zeros((pad, LRU_BLOCK), F32)
    xpad_ref[0:pad, :] = zeros
    xpad_ref[pad + seq:pad + seq + pad, :] = zeros
    xpad_ref[pad:pad + seq, :] = x_ref[0]
    for ref in a_refs + u_refs:
        ref[seq:rows_padded, :] = jnp.zeros((rows_padded - seq, LRU_BLOCK), F32)

    cw = cw_ref[...]
    cb = cb_ref[...]
    w = w_ref[0]
    decay = (-0.5 * LRU_C) * _log_sigmoid(lam_ref[...])
    bias_taps = jnp.where(lax.broadcasted_iota(jnp.int32, (chunk, LRU_BLOCK), 1) < 2, 1.0, 0.0).astype(BF16)

    for k in range(n_row_chunks):
        half_xc = cb
        for j in range(CONV_W):
            lo = pad + k * chunk + j - CONV_LEFT
            half_xc = half_xc + cw[j:j + 1, :] * xpad_ref[lo:lo + chunk, :]
        lhs = jnp.concatenate([half_xc.astype(BF16), bias_taps], axis=1)
        th = jnp.tanh(jnp.dot(lhs, w, preferred_element_type=F32))
        for d in range(2):
            c0 = 2 * d * LRU_BLOCK
            th_r = th[:, c0:c0 + LRU_BLOCK]
            th_i = th[:, c0 + LRU_BLOCK:c0 + 2 * LRU_BLOCK]
            dec = decay[d:d + 1, :]
            neg_log_a = dec * th_r + dec
            ix = half_xc * th_i + half_xc
            a = jnp.exp2(neg_log_a * -LOG2_E)
            one_minus_a2 = jnp.tanh(neg_log_a) * (a * a + 1.0)
            a_refs[d][k * chunk:(k + 1) * chunk, :] = a
            u_refs[d][k * chunk:(k + 1) * chunk, :] = jnp.exp2(0.5 * jnp.log2(one_minus_a2)) * ix

    n_groups = SCAN_SEGMENTS // SUBLANES

    def rows_at(t, grp):
        return pl.ds(t + grp * SUBLANES * seg, SUBLANES, stride=seg)

    def pass1(t, carry):
        hf, pf, hb, pb = (list(c) for c in carry)
        tb = seg - 1 - t
        for grp in range(n_groups):
            a = af_ref[rows_at(t, grp), :]
            hf[grp] = a * hf[grp] + uf_ref[rows_at(t, grp), :]
            pf[grp] = pf[grp] * a
            hf_ref[rows_at(t, grp), :] = hf[grp]
            pf_ref[rows_at(t, grp), :] = pf[grp]
            a = ab_ref[rows_at(tb, grp), :]
            hb[grp] = a * hb[grp] + ub_ref[rows_at(tb, grp), :]
            pb[grp] = pb[grp] * a
            hb_ref[rows_at(tb, grp), :] = hb[grp]
            pb_ref[rows_at(tb, grp), :] = pb[grp]
        return tuple(hf), tuple(pf), tuple(hb), tuple(pb)

    unroll = 2 if seg % 2 == 0 else 1
    zero = (jnp.zeros((SUBLANES, LRU_BLOCK), F32),) * n_groups
    one = (jnp.ones((SUBLANES, LRU_BLOCK), F32),) * n_groups
    hf, pf, hb, pb = lax.fori_loop(0, seg, pass1, (zero, one, zero, one), unroll=unroll)

    row = lax.broadcasted_iota(jnp.int32, (SUBLANES, LRU_BLOCK), 0)

    def entering_states(h_end, p_end, order):
        tiles = [jnp.zeros((SUBLANES, LRU_BLOCK), F32)] * n_groups
        c = jnp.zeros((1, LRU_BLOCK), F32)
        for k in order:
            grp, sub = divmod(k, SUBLANES)
            tiles[grp] = jnp.where(row == sub, c, tiles[grp])
            c = h_end[grp][sub:sub + 1, :] + p_end[grp][sub:sub + 1, :] * c
        return tiles

    cf = entering_states(hf, pf, range(SCAN_SEGMENTS))
    cbk = entering_states(hb, pb, reversed(range(SCAN_SEGMENTS)))

    def pass2(t, carry):
        for grp in range(n_groups):
            y = (hf_ref[rows_at(t, grp), :] + pf_ref[rows_at(t, grp), :] * cf[grp]
                 + hb_ref[rows_at(t, grp), :] + pb_ref[rows_at(t, grp), :] * cbk[grp])
            uf_ref[rows_at(t, grp), :] = y
        return carry

    lax.fori_loop(0, seg, pass2, 0, unroll=unroll)

    for k in range(n_row_chunks):
        rows = slice(k * chunk, (k + 1) * chunk)
        half_gate = g_ref[0, rows, :]
        silu = half_gate * jnp.tanh(half_gate) + half_gate
        o_ref[0, rows, :] = (uf_ref[rows, :] * silu).astype(o_ref.dtype)


def _lru_gate_weights(wa, ba, wx, bx):
    w_cat = jnp.concatenate([wa[0], wx[0], wa[1], wx[1]], axis=-1).astype(BF16)
    half_b = 0.5 * jnp.concatenate([ba[0].reshape(LRU_BLOCKS, 1, LRU_BLOCK), bx[0].reshape(LRU_BLOCKS, 1, LRU_BLOCK),
                                    ba[1].reshape(LRU_BLOCKS, 1, LRU_BLOCK), bx[1].reshape(LRU_BLOCKS, 1, LRU_BLOCK)],
                                   axis=-1)
    hi = half_b.astype(BF16)
    lo = (half_b - hi.astype(F32)).astype(BF16)
    fill = jnp.zeros((LRU_BLOCKS, LRU_BLOCK - 2, 4 * LRU_BLOCK), BF16)
    return jnp.concatenate([w_cat, hi, lo, fill], axis=1)


def _lru(proj, half_conv_w, half_conv_b, w_gates, lam):
    b, s, _ = proj.shape
    x_blk0 = (2 * ATTN_WIDTH + 2 * KV_WIDTH) // LRU_BLOCK
    g_blk0 = x_blk0 + LRU_BLOCKS
    seg_rows = s + SCAN_SEGMENTS * (SUBLANES // 2)
    n_row_chunks = 8 if s % (8 * SUBLANES) == 0 else 1
    return pl.pallas_call(
        functools.partial(_lru_kernel, seq=s, n_row_chunks=n_row_chunks),
        out_shape=jax.ShapeDtypeStruct((b, s, LRU_WIDTH), BF16),
        grid=(b, LRU_BLOCKS),
        in_specs=[
            pl.BlockSpec((1, s, LRU_BLOCK), lambda i, n: (i, 0, x_blk0 + n)),
            pl.BlockSpec((1, s, LRU_BLOCK), lambda i, n: (i, 0, g_blk0 + n)),
            pl.BlockSpec((CONV_W, LRU_BLOCK), lambda i, n: (0, n)),
            pl.BlockSpec((1, LRU_BLOCK), lambda i, n: (0, n)),
            pl.BlockSpec((1, 2 * LRU_BLOCK, 4 * LRU_BLOCK), lambda i, n: (n, 0, 0)),
            pl.BlockSpec((2, LRU_BLOCK), lambda i, n: (0, n)),
        ],
        out_specs=pl.BlockSpec((1, s, LRU_BLOCK), lambda i, n: (i, 0, n)),
        scratch_shapes=[pltpu.VMEM((s + 2 * SUBLANES, LRU_BLOCK), F32)]
        + [pltpu.VMEM((seg_rows, LRU_BLOCK), F32) for _ in range(8)],
        compiler_params=_params("parallel", "parallel"),
        name="rg_lru",
    )(proj, proj, half_conv_w, half_conv_b, w_gates, lam)


def _hi_lo(x):
    hi = x.astype(BF16)
    lo = (x - hi.astype(F32)).astype(BF16)
    return hi, lo


def _dot_nt(a, b):
    return lax.dot_general(a, b, (((1,), (1,)), ((), ())), preferred_element_type=F32)


def _lane_tile(x, width):
    return jnp.concatenate([x] * (width // x.shape[1]), axis=1) if width != x.shape[1] else x


_G_C, _G_WS, _G_BTOT, _G_MW = range(4)


def _mlstm_gate_table(gate_ref, bias_ref, hh, head, tab_ref, bt_ref):
    chunk = MLSTM_CHUNK
    ri = lax.broadcasted_iota(jnp.int32, (chunk, chunk), 0)
    si = lax.broadcasted_iota(jnp.int32, (chunk, chunk), 1)
    for d in range(2):
        ig = gate_ref[d, hh, 0] + bias_ref[d, head]
        lf = _log_sigmoid(gate_ref[2 + d, hh, 0] + bias_ref[2 + d, head])
        upper = jnp.where((ri >= si) if d else (ri <= si), 1.0, 0.0).astype(BF16)
        hi, lo = _hi_lo(lf)
        b = (jnp.dot(hi, upper, preferred_element_type=F32)
             + jnp.dot(lo, upper, preferred_element_type=F32))
        b_tot = jnp.broadcast_to(b[:, 0:1] if d else b[:, chunk - 1:chunk], b.shape)
        w = b_tot - b + ig
        mw = jnp.broadcast_to(jnp.max(w, axis=-1, keepdims=True), b.shape)
        tab = tab_ref.at[hh * 2 + d]
        tab[_G_C] = ig - b
        tab[_G_WS] = jnp.exp(w - mw)
        tab[_G_BTOT] = b_tot
        tab[_G_MW] = mw
        b_square = jnp.concatenate([b, jnp.zeros((chunk - b.shape[0], chunk), F32)], axis=0)
        bt_ref[hh * 2 + d] = b_square.T


def _chunk_mask(d, chunk):
    li = lax.broadcasted_iota(jnp.int32, (chunk, chunk), 0)
    si = lax.broadcasted_iota(jnp.int32, (chunk, chunk), 1)
    return (si >= li) if d else (si <= li)


def _mlstm_intra_weights(d, ci, qk, k, tab, bt):
    chunk = qk.shape[0]
    mask = _chunk_mask(d, chunk)
    row = pl.ds(ci, 1)
    c_masked = jnp.where(mask, tab[_G_C, row, :], -jnp.inf)
    cm = jnp.broadcast_to(jnp.max(c_masked, axis=-1, keepdims=True), c_masked.shape)
    p0 = (qk * jnp.exp(c_masked - cm)).astype(BF16)
    ktw = (k.astype(F32).T * tab[_G_WS, row, :]).astype(BF16)
    lane = lax.broadcasted_iota(jnp.int32, (chunk, chunk), 1)
    b_l = jnp.broadcast_to(jnp.sum(jnp.where(lane == ci, bt[...], 0.0), axis=-1, keepdims=True),
                           (chunk, chunk))
    return p0, ktw, cm, b_l


def _mlstm_state_matmuls(q, v, p0, ktw, c_state, n_state):
    chunk = q.shape[0]
    ones = jnp.ones((chunk, LANES), BF16)
    pk = jnp.concatenate([p0, ktw], axis=0)
    nv = jnp.dot(pk, v, preferred_element_type=F32)
    rs = jnp.dot(pk, ones, preferred_element_type=F32)
    qc = jnp.dot(q, c_state.astype(BF16), preferred_element_type=F32)
    qn = jnp.dot(q, n_state.astype(BF16), preferred_element_type=F32)
    return nv[:chunk], rs[:chunk], nv[chunk:], rs[chunk:], qc, qn


def _mlstm_combine(b_l, cm, b_tot, mw, prods, c_state, n_state, m):
    n0, r0, kv0, kn0, qc, qn = prods
    dv = n0.shape[1]
    scale = MLSTM_QK_DIM ** -0.5
    mx = jnp.maximum(m, cm)
    e1 = jnp.exp(cm - mx) * scale
    e2 = jnp.exp(m - mx) * scale
    den = e1 * r0 + e2 * qn
    inv = 1.0 / jnp.maximum(jnp.abs(den), jnp.exp(-(b_l + mx)))
    hout = _lane_tile(e1 * inv, dv) * n0 + _lane_tile(e2 * inv, dv) * qc

    m_new = jnp.maximum(b_tot + m, mw)
    d1 = jnp.exp(b_tot + m - m_new)
    d2 = jnp.exp(mw - m_new)
    c_new = _lane_tile(d1, dv) * c_state + _lane_tile(d2, dv) * kv0
    n_new = d1 * n_state + d2 * kn0
    return hout, c_new, n_new, m_new


MLSTM_HEADS_PER_STEP = 2


def _mlstm_kernel(bias_ref, q_ref, k_ref, v_ref, o_ref, z_ref, gate_ref, ng_ref, out_ref,
                  h_ref, c_ref, n_ref, tab_ref, bt_ref, pw_ref, pm_ref, *, seq):
    chunk = MLSTM_CHUNK
    n_chunks = seq // chunk
    dk, dv = MLSTM_QK_DIM, MLSTM_V_DIM
    streams = [(hh, d) for hh in range(MLSTM_HEADS_PER_STEP) for d in range(2)]
    for hh in range(MLSTM_HEADS_PER_STEP):
        _mlstm_gate_table(gate_ref, bias_ref, hh, pl.program_id(1) * MLSTM_HEADS_PER_STEP + hh,
                          tab_ref, bt_ref)
    c_ref[...] = jnp.zeros_like(c_ref)
    n_ref[...] = jnp.zeros_like(n_ref)

    def chunk_of(d, c):
        return (n_chunks - 1 - c) if d else c

    def rows_of(ci):
        return pl.ds(pl.multiple_of(ci * chunk, chunk), chunk)

    def qk_cols(hh):
        return slice(hh * dk, (hh + 1) * dk)

    def v_cols(hh):
        return slice(hh * dv, (hh + 1) * dv)

    def intra_matmuls(s, c):
        hh, d = streams[s]
        rows = rows_of(chunk_of(d, c))
        return _dot_nt(q_ref[0, rows, qk_cols(hh)], k_ref[0, rows, qk_cols(hh)])

    def park_weights(s, c, slot, qk):
        hh, d = streams[s]
        ci = chunk_of(d, c)
        p0, ktw, cm, b_l = _mlstm_intra_weights(d, ci, qk, k_ref[0, rows_of(ci), qk_cols(hh)],
                                                tab_ref.at[s], bt_ref.at[s])
        pw_ref[slot, s, 0] = p0
        pw_ref[slot, s, 1] = ktw
        pm_ref[slot, s, 0] = b_l
        pm_ref[slot, s, 1] = cm

    for s in range(len(streams)):
        park_weights(s, 0, 0, intra_matmuls(s, 0))

    def step(c, carry):
        ms = list(carry)
        slot = c % 2
        c_next = jnp.minimum(c + 1, n_chunks - 1)
        prods, parts = [], []
        for s, (hh, d) in enumerate(streams):
            rows = rows_of(chunk_of(d, c))
            prods.append(_mlstm_state_matmuls(q_ref[0, rows, qk_cols(hh)], v_ref[0, rows, v_cols(hh)],
                                              pw_ref[slot, s, 0], pw_ref[slot, s, 1], c_ref[s], n_ref[s]))
            parts.append(intra_matmuls(s, c_next))
        for s, (hh, d) in enumerate(streams):
            ci = chunk_of(d, c)
            hout, c_new, n_new, ms[s] = _mlstm_combine(
                pm_ref[slot, s, 0], pm_ref[slot, s, 1],
                tab_ref[s, _G_BTOT, pl.ds(ci, 1), :], tab_ref[s, _G_MW, pl.ds(ci, 1), :],
                prods[s], c_ref[s], n_ref[s], ms[s])
            h_ref[s, rows_of(ci), :] = hout
            c_ref[s] = c_new
            n_ref[s] = n_new
            park_weights(s, c_next, 1 - slot, parts[s])
        return tuple(ms)

    def finish(ci):
        rows = rows_of(ci)
        for hh in range(MLSTM_HEADS_PER_STEP):
            sig_o = 0.5 * jnp.tanh(o_ref[0, rows, v_cols(hh)]) + 0.5
            hs = sig_o * (h_ref[2 * hh, rows, :] + h_ref[2 * hh + 1, rows, :])
            ms = jnp.mean(hs * hs, axis=-1, keepdims=True)
            hs = hs * lax.rsqrt(ms + EPS) * ng_ref[:, v_cols(hh)]
            half_z = z_ref[0, rows, v_cols(hh)]
            out_ref[0, rows, v_cols(hh)] = (hs * (half_z * jnp.tanh(half_z) + half_z)).astype(out_ref.dtype)

    def step_and_finish(c, carry):
        carry = step(c, carry)
        finish(c)
        finish(n_chunks - 1 - c)
        return carry

    half = n_chunks // 2
    m0 = (jnp.zeros((1, LANES), F32),) * len(streams)
    ms_half = lax.fori_loop(0, half, step, m0)
    lax.fori_loop(half, n_chunks, step_and_finish, ms_half)


def _mlstm(qkv, oz, gates_t, gate_bias, norm_gain):
    b, s, _ = qkv.shape
    hps = MLSTM_HEADS_PER_STEP
    dk, dv = hps * MLSTM_QK_DIM, hps * MLSTM_V_DIM
    n_chunks = s // MLSTM_CHUNK
    n_streams = 2 * hps
    k_blk0 = MLSTM_QK_WIDTH // dk
    v_blk0 = 2 * MLSTM_QK_WIDTH // dv
    z_blk0 = MLSTM_WIDTH // dv
    return pl.pallas_call(
        functools.partial(_mlstm_kernel, seq=s),
        out_shape=jax.ShapeDtypeStruct((b, s, MLSTM_WIDTH), BF16),
        grid=(b, MLSTM_HEADS // hps),
        in_specs=[
            pl.BlockSpec(memory_space=pltpu.SMEM),
            pl.BlockSpec((1, s, dk), lambda i, h: (i, 0, h)),
            pl.BlockSpec((1, s, dk), lambda i, h: (i, 0, k_blk0 + h)),
            pl.BlockSpec((1, s, dv), lambda i, h: (i, 0, v_blk0 + h)),
            pl.BlockSpec((1, s, dv), lambda i, h: (i, 0, h)),
            pl.BlockSpec((1, s, dv), lambda i, h: (i, 0, z_blk0 + h)),
            pl.BlockSpec((N_GATE_SETS, hps, 1, n_chunks, MLSTM_CHUNK), lambda i, h: (0, h, i, 0, 0)),
            pl.BlockSpec((1, dv), lambda i, h: (0, h)),
        ],
        out_specs=pl.BlockSpec((1, s, dv), lambda i, h: (i, 0, h)),
        scratch_shapes=[
            pltpu.VMEM((n_streams, s, MLSTM_V_DIM), F32),
            pltpu.VMEM((n_streams, MLSTM_QK_DIM, MLSTM_V_DIM), F32),
            pltpu.VMEM((n_streams, MLSTM_QK_DIM, LANES), F32),
            pltpu.VMEM((n_streams, 4, n_chunks, MLSTM_CHUNK), F32),
            pltpu.VMEM((n_streams, MLSTM_CHUNK, MLSTM_CHUNK), F32),
            pltpu.VMEM((2, n_streams, 2, MLSTM_CHUNK, MLSTM_CHUNK), BF16),
            pltpu.VMEM((2, n_streams, 2, MLSTM_CHUNK, MLSTM_CHUNK), F32),
        ],
        compiler_params=_params("parallel", "parallel"),
        name="mlstm",
    )(gate_bias, qkv, qkv, qkv, oz, oz, gates_t, norm_gain)


def _row_tile(t, want):
    return want if t % want == 0 else t


def kernel(x, norm_gain, final_gain, even_w_in, even_w_out, q_norm_gain, k_norm_gain, conv_w, conv_b,
           lru_wa, lru_ba, lru_wx, lru_bx, lru_lambda, odd_w_in, odd_gate_bias, odd_norm_gain, odd_w_out):
    b, s, d = x.shape
    t = b * s
    assert s % (SCAN_SEGMENTS * SUBLANES) == 0 and s % (2 * MLSTM_CHUNK) == 0 and s % GRID_W == 0
    x2 = x.reshape(t, d)
    tn = 512

    order = _head_lane_order()
    qk_w = ATTN_WIDTH + KV_WIDTH
    w0 = even_w_in[0]
    w0_qk = w0[:, :qk_w].reshape(d, qk_w // HEAD_DIM, HEAD_DIM)[:, :, order].reshape(d, qk_w)
    g_lru0 = EVEN_IN - LRU_WIDTH
    w0 = jnp.concatenate([w0_qk, w0[:, qk_w:g_lru0], 0.5 * w0[:, g_lru0:]], axis=1).astype(BF16)
    proj0 = _norm_matmul(x2, norm_gain[0][None], w0, F32,
                         tm=_row_tile(t, 1024), tn=tn).reshape(b, s, EVEN_IN)
    cos_t, sin_t = _rope_tables(s)
    attn = _attention(proj0, q_norm_gain[0][order][None], k_norm_gain[0][order][None], cos_t, sin_t,
                      tq=_row_tile(s, 256))
    lru = _lru(proj0, 0.5 * conv_w[0], 0.5 * conv_b[0][None],
               _lru_gate_weights(lru_wa[0], lru_ba[0], lru_wx[0], lru_bx[0]), lru_lambda[0])
    x1, hn1 = _out_proj0(attn.reshape(t, ATTN_WIDTH), lru.reshape(t, LRU_WIDTH),
                         even_w_out[0].astype(BF16), x2, norm_gain[1][None], tm=_row_tile(t, 512))

    w1 = odd_w_in[0]
    qkv_w = 2 * MLSTM_QK_WIDTH + MLSTM_WIDTH
    oz_w = 2 * MLSTM_WIDTH
    w1_main = jnp.concatenate([w1[:, :qkv_w], 0.5 * w1[:, qkv_w:qkv_w + oz_w]], axis=1).astype(BF16)
    tm1 = _row_tile(t, 1024)
    qkv = _matmul(hn1, w1_main, qkv_w, 0, BF16, tm1, tn, "in_proj1_qkv").reshape(b, s, qkv_w)
    oz = _matmul(hn1, w1_main, oz_w, qkv_w // tn, F32, tm1, tn, "in_proj1_oz").reshape(b, s, oz_w)
    gates_t = _matmul_t(w1[:, qkv_w + oz_w:].T.astype(BF16), hn1, tm1, "in_proj1_gates")
    gates_t = gates_t.reshape(N_GATE_SETS, MLSTM_HEADS, b, s // MLSTM_CHUNK, MLSTM_CHUNK)
    mix1 = _mlstm(qkv, oz, gates_t, odd_gate_bias[0], odd_norm_gain[0][None])
    out = _out_proj1(mix1.reshape(t, MLSTM_WIDTH), odd_w_out[0].astype(BF16), x1, final_gain[None],
                     tm=_row_tile(t, 512))
    return out.reshape(b, s, d)
```

```python
import functools

import jax
import jax.numpy as jnp
from jax import lax
from jax.experimental import pallas as pl
from jax.experimental.pallas import tpu as pltpu

F32 = jnp.float32
BF16 = jnp.bfloat16

EPS = 1e-6
GRID_W = 64
HEAD_DIM = 128
ATTN_HEADS = 8
KV_HEADS = 2
Q_GROUP = ATTN_HEADS // KV_HEADS
ATTN_WIDTH = ATTN_HEADS * HEAD_DIM
KV_WIDTH = KV_HEADS * HEAD_DIM
ROPE_THETA = 10000.0
LRU_WIDTH = 1024
LRU_BLOCKS = 8
LRU_BLOCK = 128
LRU_C = 8.0
CONV_W = 4
CONV_LEFT = 2
EVEN_IN = ATTN_WIDTH + 2 * KV_WIDTH + ATTN_WIDTH + 2 * LRU_WIDTH
MLSTM_HEADS = 8
MLSTM_V_DIM = 256
MLSTM_QK_DIM = 128
MLSTM_WIDTH = MLSTM_HEADS * MLSTM_V_DIM
MLSTM_QK_WIDTH = MLSTM_HEADS * MLSTM_QK_DIM
MLSTM_CHUNK = 128
N_GATE_SETS = 4

LOG2_E = 1.4426950408889634

LANES = 128
SUBLANES = 8
VMEM_LIMIT = 56 * 1024 * 1024
SCAN_SEGMENTS = 4 * SUBLANES


def _sigmoid(x):
    return 1.0 / (1.0 + jnp.exp(-x))


def _log_sigmoid(x):
    return jnp.minimum(x, 0.0) - jnp.log1p(jnp.exp(-jnp.abs(x)))


def _params(*sem):
    return pltpu.CompilerParams(dimension_semantics=sem, vmem_limit_bytes=VMEM_LIMIT)


def _rmsnorm_rows_to(x_ref, g_ref, dst_ref, rows, chunk):
    g = g_ref[...]

    def body(c, carry):
        r0 = pl.multiple_of(c * chunk, chunk)
        x = x_ref[pl.ds(r0, chunk), :]
        ms = jnp.mean(x * x, axis=-1, keepdims=True)
        dst_ref[pl.ds(r0, chunk), :] = (x * lax.rsqrt(ms + EPS) * g).astype(dst_ref.dtype)
        return carry

    lax.fori_loop(0, rows // chunk, body, 0)


def _norm_matmul_kernel(x_ref, g_ref, w_ref, o_ref, hn_ref, *, tm, half_from):
    j = pl.program_id(1)

    @pl.when(j == 0)
    def _():
        _rmsnorm_rows_to(x_ref, g_ref, hn_ref, tm, 64)

    scale = jnp.where(j >= half_from, 0.5, 1.0).astype(F32)
    o_ref[...] = (jnp.dot(hn_ref[...], w_ref[...], preferred_element_type=F32) * scale).astype(o_ref.dtype)


def _norm_matmul(x, gain, w, out_dtype, tm, tn, half_from_col):
    t, d = x.shape
    n = w.shape[1]
    return pl.pallas_call(
        functools.partial(_norm_matmul_kernel, tm=tm, half_from=half_from_col // tn),
        out_shape=jax.ShapeDtypeStruct((t, n), out_dtype),
        grid=(t // tm, n // tn),
        in_specs=[
            pl.BlockSpec((tm, d), lambda i, j: (i, 0)),
            pl.BlockSpec((1, d), lambda i, j: (0, 0)),
            pl.BlockSpec((d, tn), lambda i, j: (0, j)),
        ],
        out_specs=pl.BlockSpec((tm, tn), lambda i, j: (i, j)),
        scratch_shapes=[pltpu.VMEM((tm, d), BF16)],
        compiler_params=_params("parallel", "arbitrary"),
        name="norm_in_proj",
    )(x, gain, w)


def _matmul_kernel(a_ref, w_ref, o_ref, wb_ref, *, w_scale, k_chunk):
    @pl.when(pl.program_id(1) == 0)
    def _():
        def cast(c, carry):
            rows = pl.ds(pl.multiple_of(c * k_chunk, k_chunk), k_chunk)
            wb_ref[rows, :] = (w_ref[rows, :] * w_scale).astype(wb_ref.dtype)
            return carry

        lax.fori_loop(0, w_ref.shape[0] // k_chunk, cast, 0)

    o_ref[...] = jnp.dot(a_ref[...], wb_ref[...], preferred_element_type=F32).astype(o_ref.dtype)


def _matmul(a, w, n, col_block0, out_dtype, tm, tn, name, w_scale=1.0):
    t, k = a.shape
    return pl.pallas_call(
        functools.partial(_matmul_kernel, w_scale=w_scale, k_chunk=64),
        out_shape=jax.ShapeDtypeStruct((t, n), out_dtype),
        grid=(n // tn, t // tm),
        in_specs=[
            pl.BlockSpec((tm, k), lambda j, i: (i, 0)),
            pl.BlockSpec((k, tn), lambda j, i: (0, j + col_block0)),
        ],
        out_specs=pl.BlockSpec((tm, tn), lambda j, i: (i, j)),
        scratch_shapes=[pltpu.VMEM((k, tn), BF16)],
        compiler_params=_params("parallel", "arbitrary"),
        name=name,
    )(a, w)


def _matmul_t_kernel(wt_ref, a_ref, o_ref):
    o_ref[...] = lax.dot_general(wt_ref[...], a_ref[...], (((1,), (1,)), ((), ())),
                                 preferred_element_type=F32)


def _matmul_t(wt, a, tm, name):
    n, k = wt.shape
    t = a.shape[0]
    return pl.pallas_call(
        _matmul_t_kernel,
        out_shape=jax.ShapeDtypeStruct((n, t), F32),
        grid=(t // tm,),
        in_specs=[
            pl.BlockSpec((n, k), lambda i: (0, 0)),
            pl.BlockSpec((tm, k), lambda i: (i, 0)),
        ],
        out_specs=pl.BlockSpec((n, tm), lambda i: (0, i)),
        compiler_params=_params("parallel"),
        name=name,
    )(wt, a)


def _out_proj_epilogue(y_ref, x_ref, g_ref, xo_ref, no_ref, rows, chunk):
    g = g_ref[...]

    def body(c, carry):
        r0 = pl.multiple_of(c * chunk, chunk)
        xn = x_ref[pl.ds(r0, chunk), :] + y_ref[pl.ds(r0, chunk), :]
        if xo_ref is not None:
            xo_ref[pl.ds(r0, chunk), :] = xn
        ms = jnp.mean(xn * xn, axis=-1, keepdims=True)
        no_ref[pl.ds(r0, chunk), :] = (xn * lax.rsqrt(ms + EPS) * g).astype(no_ref.dtype)
        return carry

    lax.fori_loop(0, rows // chunk, body, 0)


def _out_proj0_kernel(a_ref, b_ref, wa_ref, wb_ref, x_ref, g_ref, xo_ref, no_ref, y_ref, *, tm):
    y_ref[...] = (jnp.dot(a_ref[...], wa_ref[...], preferred_element_type=F32)
                  + jnp.dot(b_ref[...], wb_ref[...], preferred_element_type=F32))
    _out_proj_epilogue(y_ref, x_ref, g_ref, xo_ref, no_ref, tm, 64)


def _out_proj0(a, b, w, x, gain, tm):
    t, ka = a.shape
    kb = b.shape[1]
    d = w.shape[1]
    return pl.pallas_call(
        functools.partial(_out_proj0_kernel, tm=tm),
        out_shape=(jax.ShapeDtypeStruct((t, d), F32), jax.ShapeDtypeStruct((t, d), BF16)),
        grid=(t // tm,),
        in_specs=[
            pl.BlockSpec((tm, ka), lambda i: (i, 0)),
            pl.BlockSpec((tm, kb), lambda i: (i, 0)),
            pl.BlockSpec((ka, d), lambda i: (0, 0)),
            pl.BlockSpec((kb, d), lambda i: (ka // kb, 0)),
            pl.BlockSpec((tm, d), lambda i: (i, 0)),
            pl.BlockSpec((1, d), lambda i: (0, 0)),
        ],
        out_specs=(pl.BlockSpec((tm, d), lambda i: (i, 0)), pl.BlockSpec((tm, d), lambda i: (i, 0))),
        scratch_shapes=[pltpu.VMEM((tm, d), F32)],
        compiler_params=_params("parallel"),
        name="out_proj0",
    )(a, b, w, w, x, gain)


def _out_proj1_kernel(a_ref, w_ref, x_ref, g_ref, no_ref, y_ref, *, tm):
    y_ref[...] = jnp.dot(a_ref[...], w_ref[...], preferred_element_type=F32)
    _out_proj_epilogue(y_ref, x_ref, g_ref, None, no_ref, tm, 64)


def _out_proj1(a, w, x, gain, tm):
    t, k = a.shape
    d = w.shape[1]
    return pl.pallas_call(
        functools.partial(_out_proj1_kernel, tm=tm),
        out_shape=jax.ShapeDtypeStruct((t, d), F32),
        grid=(t // tm,),
        in_specs=[
            pl.BlockSpec((tm, k), lambda i: (i, 0)),
            pl.BlockSpec((k, d), lambda i: (0, 0)),
            pl.BlockSpec((tm, d), lambda i: (i, 0)),
            pl.BlockSpec((1, d), lambda i: (0, 0)),
        ],
        out_specs=pl.BlockSpec((tm, d), lambda i: (i, 0)),
        scratch_shapes=[pltpu.VMEM((tm, d), F32)],
        compiler_params=_params("parallel"),
        name="out_proj1",
    )(a, w, x, gain)


def _rope_tables(seq_len):
    half = HEAD_DIM // 2
    t = jnp.arange(seq_len)
    row = (t // GRID_W).astype(F32)
    col = (t % GRID_W).astype(F32)
    inv = ROPE_THETA ** (-jnp.arange(0, half, 2, dtype=F32) / half)
    ar = row[:, None] * inv
    ac = col[:, None] * inv
    cos_t = jnp.concatenate([jnp.cos(ar), jnp.cos(ar), jnp.cos(ac), jnp.cos(ac)], axis=-1)
    sin_t = jnp.concatenate([-jnp.sin(ar), jnp.sin(ar), -jnp.sin(ac), jnp.sin(ac)], axis=-1)
    return cos_t, sin_t


def _norm_rope(x, gain, cos_v, sin_v):
    quarter = HEAD_DIM // 4
    ms = jnp.mean(x * x, axis=-1, keepdims=True)
    xn = x * lax.rsqrt(ms + EPS) * gain
    lane = lax.broadcasted_iota(jnp.int32, xn.shape, 1)
    first = (lane % (2 * quarter)) < quarter
    partner = jnp.where(first, pltpu.roll(xn, HEAD_DIM - quarter, 1), pltpu.roll(xn, quarter, 1))
    return xn * cos_v + partner * sin_v


def _attention_kernel(q_ref, k_ref, v_ref, g_ref, qg_ref, kg_ref, cos_ref, sin_ref, o_ref,
                      kr_ref, vt_ref, qr_ref, s_ref, of_ref, *, seq, tq, kc):
    n_blk = seq // tq
    n_kc = seq // kc
    n_total = Q_GROUP * n_blk
    kg = kg_ref[...]
    qg = qg_ref[...]
    q_scale = HEAD_DIM ** -0.5 * LOG2_E
    eye = jnp.where(lax.broadcasted_iota(jnp.int32, (HEAD_DIM, HEAD_DIM), 0)
                    == lax.broadcasted_iota(jnp.int32, (HEAD_DIM, HEAD_DIM), 1), 1.0, 0.0).astype(BF16)

    def prep(c, carry):
        rows = pl.ds(pl.multiple_of(c * tq, tq), tq)
        cos_v, sin_v = cos_ref[rows, :], sin_ref[rows, :]
        kr_ref[rows, :] = _norm_rope(k_ref[0, rows, :], kg, cos_v, sin_v).astype(BF16)
        vt_ref[:, rows] = lax.dot_general(eye, v_ref[0, rows, :].astype(BF16), (((1,), (1,)), ((), ())),
                                          preferred_element_type=F32).astype(BF16)
        for g in range(Q_GROUP):
            q = _norm_rope(q_ref[0, rows, g * HEAD_DIM:(g + 1) * HEAD_DIM], qg, cos_v, sin_v) * q_scale
            qr_ref[pl.ds(pl.multiple_of(g * seq + c * tq, tq), tq), :] = q.astype(BF16)
        return carry

    lax.fori_loop(0, n_blk, prep, 0, unroll=2 if n_blk % 2 == 0 else 1)

    def blk_rows(i):
        return pl.ds(pl.multiple_of(i * tq, tq), tq)

    def scores_chunk(i, j):
        return lax.dot_general(kr_ref[j * kc:(j + 1) * kc, :], qr_ref[blk_rows(i), :],
                               (((1,), (1,)), ((), ())), preferred_element_type=F32)

    def col_max8(x):
        return jnp.max(x.reshape(x.shape[0] // SUBLANES, SUBLANES, x.shape[1]), axis=0)

    def col_sum8(x):
        return jnp.sum(x.reshape(x.shape[0] // SUBLANES, SUBLANES, x.shape[1]), axis=0)

    neg_inf8 = jnp.full((SUBLANES, tq), -jnp.inf, F32)
    m8 = neg_inf8
    for j in range(n_kc):
        s_t = scores_chunk(0, j)
        s_ref[0, j] = s_t
        m8 = jnp.maximum(m8, col_max8(s_t))

    def block(i, m8_cur, slot):
        nxt = jnp.minimum(i + 1, n_total - 1)
        m_cur = jnp.max(m8_cur, axis=0, keepdims=True)
        m8_next = neg_inf8
        l8 = jnp.zeros((SUBLANES, tq), F32)
        acc = jnp.zeros((HEAD_DIM, tq), F32)
        for j in range(n_kc):
            s_next = scores_chunk(nxt, j)
            s_ref[1 - slot, j] = s_next
            m8_next = jnp.maximum(m8_next, col_max8(s_next))
            p_t = jnp.exp2(s_ref[slot, j] - m_cur)
            l8 = l8 + col_sum8(p_t)
            acc = acc + jnp.dot(vt_ref[:, j * kc:(j + 1) * kc], p_t.astype(BF16),
                                preferred_element_type=F32)
        l = jnp.sum(l8, axis=0, keepdims=True)
        of_ref[blk_rows(i), :] = (acc * (1.0 / l)).T
        return m8_next

    def block_pair(ii, m8_cur):
        return block(2 * ii + 1, block(2 * ii, m8_cur, 0), 1)

    lax.fori_loop(0, n_total // 2, block_pair, m8)

    def finish(c, carry):
        rows = pl.ds(pl.multiple_of(c * tq, tq), tq)
        for g in range(Q_GROUP):
            cols = slice(g * HEAD_DIM, (g + 1) * HEAD_DIM)
            gate = g_ref[0, rows, cols]
            o = of_ref[pl.ds(pl.multiple_of(g * seq + c * tq, tq), tq), :]
            o_ref[0, rows, cols] = (o * (gate * _sigmoid(gate))).astype(o_ref.dtype)
        return carry

    lax.fori_loop(0, n_blk, finish, 0)


def _attention(proj, q_gain, k_gain, cos_t, sin_t, tq):
    b, s, _ = proj.shape
    gw = Q_GROUP * HEAD_DIM
    k_blk0 = ATTN_WIDTH // HEAD_DIM
    v_blk0 = (ATTN_WIDTH + KV_WIDTH) // HEAD_DIM
    g_blk0 = (ATTN_WIDTH + 2 * KV_WIDTH) // gw
    kc = _row_tile(s, 512)
    return pl.pallas_call(
        functools.partial(_attention_kernel, seq=s, tq=tq, kc=kc),
        out_shape=jax.ShapeDtypeStruct((b, s, ATTN_WIDTH), BF16),
        grid=(b, KV_HEADS),
        in_specs=[
            pl.BlockSpec((1, s, gw), lambda i, j: (i, 0, j)),
            pl.BlockSpec((1, s, HEAD_DIM), lambda i, j: (i, 0, k_blk0 + j)),
            pl.BlockSpec((1, s, HEAD_DIM), lambda i, j: (i, 0, v_blk0 + j)),
            pl.BlockSpec((1, s, gw), lambda i, j: (i, 0, g_blk0 + j)),
            pl.BlockSpec((1, HEAD_DIM), lambda i, j: (0, 0)),
            pl.BlockSpec((1, HEAD_DIM), lambda i, j: (0, 0)),
            pl.BlockSpec((s, HEAD_DIM), lambda i, j: (0, 0)),
            pl.BlockSpec((s, HEAD_DIM), lambda i, j: (0, 0)),
        ],
        out_specs=pl.BlockSpec((1, s, gw), lambda i, j: (i, 0, j)),
        scratch_shapes=[
            pltpu.VMEM((s, HEAD_DIM), BF16),
            pltpu.VMEM((HEAD_DIM, s), BF16),
            pltpu.VMEM((Q_GROUP * s, HEAD_DIM), BF16),
            pltpu.VMEM((2, s // kc, kc, tq), F32),
            pltpu.VMEM((Q_GROUP * s, HEAD_DIM), F32),
        ],
        compiler_params=_params("parallel", "parallel"),
        name="attention",
    )(proj, proj, proj, proj, q_gain, k_gain, cos_t, sin_t)


def _lru_kernel(x_ref, g_ref, cw_ref, cb_ref, w_ref, lam_ref, o_ref,
                xpad_ref, af_ref, uf_ref, ab_ref, ub_ref, hf_ref, pf_ref, hb_ref, pb_ref,
                *, seq, n_row_chunks):
    seg = seq // SCAN_SEGMENTS + SUBLANES // 2
    rows_padded = SCAN_SEGMENTS * seg
    chunk = seq // n_row_chunks
    pad = SUBLANES
    a_refs = (af_ref, ab_ref)
    u_refs = (uf_ref, ub_ref)

    zeros = jnp.zeros((pad, LRU_BLOCK), F32)
    xpad_ref[0:pad, :] = zeros
    xpad_ref[pad + seq:pad + seq + pad, :] = zeros
    xpad_ref[pad:pad + seq, :] = x_ref[0]
    for ref in a_refs + u_refs:
        ref[seq:rows_padded, :] = jnp.zeros((rows_padded - seq, LRU_BLOCK), F32)

    cw = cw_ref[...]
    cb = cb_ref[...]
    w = w_ref[0]
    decay = (-0.5 * LRU_C) * _log_sigmoid(lam_ref[...])
    bias_taps = jnp.where(lax.broadcasted_iota(jnp.int32, (chunk, LRU_BLOCK), 1) < 2, 1.0, 0.0).astype(BF16)

    for k in range(n_row_chunks):
        half_xc = cb
        for j in range(CONV_W):
            lo = pad + k * chunk + j - CONV_LEFT
            half_xc = half_xc + cw[j:j + 1, :] * xpad_ref[lo:lo + chunk, :]
        lhs = jnp.concatenate([half_xc.astype(BF16), bias_taps], axis=1)
        th = jnp.tanh(jnp.dot(lhs, w, preferred_element_type=F32))
        for d in range(2):
            c0 = 2 * d * LRU_BLOCK
            th_r = th[:, c0:c0 + LRU_BLOCK]
            th_i = th[:, c0 + LRU_BLOCK:c0 + 2 * LRU_BLOCK]
            dec = decay[d:d + 1, :]
            neg_log_a = dec * th_r + dec
            ix = half_xc * th_i + half_xc
            a = jnp.exp2(neg_log_a * -LOG2_E)
            one_minus_a2 = jnp.tanh(neg_log_a) * (a * a + 1.0)
            a_refs[d][k * chunk:(k + 1) * chunk, :] = a
            u_refs[d][k * chunk:(k + 1) * chunk, :] = jnp.exp2(0.5 * jnp.log2(one_minus_a2)) * ix

    n_groups = SCAN_SEGMENTS // SUBLANES

    def rows_at(t, grp):
        return pl.ds(t + grp * SUBLANES * seg, SUBLANES, stride=seg)

    def pass1(t, carry):
        hf, pf, hb, pb = (list(c) for c in carry)
        tb = seg - 1 - t
        for grp in range(n_groups):
            a = af_ref[rows_at(t, grp), :]
            hf[grp] = a * hf[grp] + uf_ref[rows_at(t, grp), :]
            pf[grp] = pf[grp] * a
            hf_ref[rows_at(t, grp), :] = hf[grp]
            pf_ref[rows_at(t, grp), :] = pf[grp]
            a = ab_ref[rows_at(tb, grp), :]
            hb[grp] = a * hb[grp] + ub_ref[rows_at(tb, grp), :]
            pb[grp] = pb[grp] * a
            hb_ref[rows_at(tb, grp), :] = hb[grp]
            pb_ref[rows_at(tb, grp), :] = pb[grp]
        return tuple(hf), tuple(pf), tuple(hb), tuple(pb)

    unroll = 2 if seg % 2 == 0 else 1
    zero = (jnp.zeros((SUBLANES, LRU_BLOCK), F32),) * n_groups
    one = (jnp.ones((SUBLANES, LRU_BLOCK), F32),) * n_groups
    hf, pf, hb, pb = lax.fori_loop(0, seg, pass1, (zero, one, zero, one), unroll=unroll)

    row = lax.broadcasted_iota(jnp.int32, (SUBLANES, LRU_BLOCK), 0)

    def entering_states(h_end, p_end, order):
        tiles = [jnp.zeros((SUBLANES, LRU_BLOCK), F32)] * n_groups
        c = jnp.zeros((1, LRU_BLOCK), F32)
        for k in order:
            grp, sub = divmod(k, SUBLANES)
            tiles[grp] = jnp.where(row == sub, c, tiles[grp])
            c = h_end[grp][sub:sub + 1, :] + p_end[grp][sub:sub + 1, :] * c
        return tiles

    cf = entering_states(hf, pf, range(SCAN_SEGMENTS))
    cbk = entering_states(hb, pb, reversed(range(SCAN_SEGMENTS)))

    def pass2(t, carry):
        for grp in range(n_groups):
            y = (hf_ref[rows_at(t, grp), :] + pf_ref[rows_at(t, grp), :] * cf[grp]
                 + hb_ref[rows_at(t, grp), :] + pb_ref[rows_at(t, grp), :] * cbk[grp])
            uf_ref[rows_at(t, grp), :] = y
        return carry

    lax.fori_loop(0, seg, pass2, 0, unroll=unroll)

    for k in range(n_row_chunks):
        rows = slice(k * chunk, (k + 1) * chunk)
        half_gate = g_ref[0, rows, :]
        silu = half_gate * jnp.tanh(half_gate) + half_gate
        o_ref[0, rows, :] = (uf_ref[rows, :] * silu).astype(o_ref.dtype)


def _lru_gate_weights(wa, ba, wx, bx):
    w_cat = jnp.concatenate([wa[0], wx[0], wa[1], wx[1]], axis=-1).astype(BF16)
    half_b = 0.5 * jnp.concatenate([ba[0].reshape(LRU_BLOCKS, 1, LRU_BLOCK), bx[0].reshape(LRU_BLOCKS, 1, LRU_BLOCK),
                                    ba[1].reshape(LRU_BLOCKS, 1, LRU_BLOCK), bx[1].reshape(LRU_BLOCKS, 1, LRU_BLOCK)],
                                   axis=-1)
    hi = half_b.astype(BF16)
    lo = (half_b - hi.astype(F32)).astype(BF16)
    fill = jnp.zeros((LRU_BLOCKS, LRU_BLOCK - 2, 4 * LRU_BLOCK), BF16)
    return jnp.concatenate([w_cat, hi, lo, fill], axis=1)


def _lru(proj, half_conv_w, half_conv_b, w_gates, lam):
    b, s, _ = proj.shape
    x_blk0 = (2 * ATTN_WIDTH + 2 * KV_WIDTH) // LRU_BLOCK
    g_blk0 = x_blk0 + LRU_BLOCKS
    seg_rows = s + SCAN_SEGMENTS * (SUBLANES // 2)
    n_row_chunks = 8 if s % (8 * SUBLANES) == 0 else 1
    return pl.pallas_call(
        functools.partial(_lru_kernel, seq=s, n_row_chunks=n_row_chunks),
        out_shape=jax.ShapeDtypeStruct((b, s, LRU_WIDTH), BF16),
        grid=(b, LRU_BLOCKS),
        in_specs=[
            pl.BlockSpec((1, s, LRU_BLOCK), lambda i, n: (i, 0, x_blk0 + n)),
            pl.BlockSpec((1, s, LRU_BLOCK), lambda i, n: (i, 0, g_blk0 + n)),
            pl.BlockSpec((CONV_W, LRU_BLOCK), lambda i, n: (0, n)),
            pl.BlockSpec((1, LRU_BLOCK), lambda i, n: (0, n)),
            pl.BlockSpec((1, 2 * LRU_BLOCK, 4 * LRU_BLOCK), lambda i, n: (n, 0, 0)),
            pl.BlockSpec((2, LRU_BLOCK), lambda i, n: (0, n)),
        ],
        out_specs=pl.BlockSpec((1, s, LRU_BLOCK), lambda i, n: (i, 0, n)),
        scratch_shapes=[pltpu.VMEM((s + 2 * SUBLANES, LRU_BLOCK), F32)]
        + [pltpu.VMEM((seg_rows, LRU_BLOCK), F32) for _ in range(8)],
        compiler_params=_params("parallel", "parallel"),
        name="rg_lru",
    )(proj, proj, half_conv_w, half_conv_b, w_gates, lam)


def _hi_lo(x):
    hi = x.astype(BF16)
    lo = (x - hi.astype(F32)).astype(BF16)
    return hi, lo


def _dot_nt(a, b):
    return lax.dot_general(a, b, (((1,), (1,)), ((), ())), preferred_element_type=F32)


def _lane_tile(x, width):
    return jnp.concatenate([x] * (width // x.shape[1]), axis=1) if width != x.shape[1] else x


_G_C, _G_WS, _G_BTOT, _G_MW = range(4)


def _mlstm_gate_table(gate_ref, bias_ref, hh, head, tab_ref, bt_ref):
    chunk = MLSTM_CHUNK
    ri = lax.broadcasted_iota(jnp.int32, (chunk, chunk), 0)
    si = lax.broadcasted_iota(jnp.int32, (chunk, chunk), 1)
    for d in range(2):
        ig = gate_ref[d, hh, 0] + bias_ref[d, head]
        lf = _log_sigmoid(gate_ref[2 + d, hh, 0] + bias_ref[2 + d, head])
        upper = jnp.where((ri >= si) if d else (ri <= si), 1.0, 0.0).astype(BF16)
        hi, lo = _hi_lo(lf)
        b = (jnp.dot(hi, upper, preferred_element_type=F32)
             + jnp.dot(lo, upper, preferred_element_type=F32))
        b_tot = jnp.broadcast_to(b[:, 0:1] if d else b[:, chunk - 1:chunk], b.shape)
        w = b_tot - b + ig
        mw = jnp.broadcast_to(jnp.max(w, axis=-1, keepdims=True), b.shape)
        tab = tab_ref.at[hh * 2 + d]
        tab[_G_C] = ig - b
        tab[_G_WS] = jnp.exp(w - mw)
        tab[_G_BTOT] = b_tot
        tab[_G_MW] = mw
        b_square = jnp.concatenate([b, jnp.zeros((chunk - b.shape[0], chunk), F32)], axis=0)
        bt_ref[hh * 2 + d] = b_square.T


def _chunk_mask(d, chunk):
    li = lax.broadcasted_iota(jnp.int32, (chunk, chunk), 0)
    si = lax.broadcasted_iota(jnp.int32, (chunk, chunk), 1)
    return (si >= li) if d else (si <= li)


def _mlstm_intra_weights(d, ci, qk, k, tab, bt):
    chunk = qk.shape[0]
    mask = _chunk_mask(d, chunk)
    row = pl.ds(ci, 1)
    c_masked = jnp.where(mask, tab[_G_C, row, :], -jnp.inf)
    cm = jnp.broadcast_to(jnp.max(c_masked, axis=-1, keepdims=True), c_masked.shape)
    p0 = (qk * jnp.exp(c_masked - cm)).astype(BF16)
    ktw = (k.astype(F32).T * tab[_G_WS, row, :]).astype(BF16)
    lane = lax.broadcasted_iota(jnp.int32, (chunk, chunk), 1)
    b_l = jnp.broadcast_to(jnp.sum(jnp.where(lane == ci, bt[...], 0.0), axis=-1, keepdims=True),
                           (chunk, chunk))
    return p0, ktw, cm, b_l


def _mlstm_state_matmuls(q, v, p0, ktw, c_state, n_state):
    chunk = q.shape[0]
    ones = jnp.ones((chunk, LANES), BF16)
    pk = jnp.concatenate([p0, ktw], axis=0)
    nv = jnp.dot(pk, v, preferred_element_type=F32)
    rs = jnp.dot(pk, ones, preferred_element_type=F32)
    qc = jnp.dot(q, c_state.astype(BF16), preferred_element_type=F32)
    qn = jnp.dot(q, n_state.astype(BF16), preferred_element_type=F32)
    return nv[:chunk], rs[:chunk], nv[chunk:], rs[chunk:], qc, qn


def _mlstm_combine(b_l, cm, b_tot, mw, prods, c_state, n_state, m):
    n0, r0, kv0, kn0, qc, qn = prods
    dv = n0.shape[1]
    scale = MLSTM_QK_DIM ** -0.5
    mx = jnp.maximum(m, cm)
    e1 = jnp.exp(cm - mx) * scale
    e2 = jnp.exp(m - mx) * scale
    den = e1 * r0 + e2 * qn
    inv = 1.0 / jnp.maximum(jnp.abs(den), jnp.exp(-(b_l + mx)))
    hout = _lane_tile(e1 * inv, dv) * n0 + _lane_tile(e2 * inv, dv) * qc

    m_new = jnp.maximum(b_tot + m, mw)
    d1 = jnp.exp(b_tot + m - m_new)
    d2 = jnp.exp(mw - m_new)
    c_new = _lane_tile(d1, dv) * c_state + _lane_tile(d2, dv) * kv0
    n_new = d1 * n_state + d2 * kn0
    return hout, c_new, n_new, m_new


MLSTM_HEADS_PER_STEP = 2


def _mlstm_kernel(bias_ref, q_ref, k_ref, v_ref, o_ref, z_ref, gate_ref, ng_ref, out_ref,
                  h_ref, c_ref, n_ref, tab_ref, bt_ref, pw_ref, pm_ref, *, seq):
    chunk = MLSTM_CHUNK
    n_chunks = seq // chunk
    dk, dv = MLSTM_QK_DIM, MLSTM_V_DIM
    streams = [(hh, d) for hh in range(MLSTM_HEADS_PER_STEP) for d in range(2)]
    for hh in range(MLSTM_HEADS_PER_STEP):
        _mlstm_gate_table(gate_ref, bias_ref, hh, pl.program_id(1) * MLSTM_HEADS_PER_STEP + hh,
                          tab_ref, bt_ref)
    c_ref[...] = jnp.zeros_like(c_ref)
    n_ref[...] = jnp.zeros_like(n_ref)

    def chunk_of(d, c):
        return (n_chunks - 1 - c) if d else c

    def rows_of(ci):
        return pl.ds(pl.multiple_of(ci * chunk, chunk), chunk)

    def qk_cols(hh):
        return slice(hh * dk, (hh + 1) * dk)

    def v_cols(hh):
        return slice(hh * dv, (hh + 1) * dv)

    def intra_matmuls(s, c):
        hh, d = streams[s]
        rows = rows_of(chunk_of(d, c))
        return _dot_nt(q_ref[0, rows, qk_cols(hh)], k_ref[0, rows, qk_cols(hh)])

    def park_weights(s, c, slot, qk):
        hh, d = streams[s]
        ci = chunk_of(d, c)
        p0, ktw, cm, b_l = _mlstm_intra_weights(d, ci, qk, k_ref[0, rows_of(ci), qk_cols(hh)],
                                                tab_ref.at[s], bt_ref.at[s])
        pw_ref[slot, s, 0] = p0
        pw_ref[slot, s, 1] = ktw
        pm_ref[slot, s, 0] = b_l
        pm_ref[slot, s, 1] = cm

    for s in range(len(streams)):
        park_weights(s, 0, 0, intra_matmuls(s, 0))

    def step(c, carry):
        ms = list(carry)
        slot = c % 2
        c_next = jnp.minimum(c + 1, n_chunks - 1)
        prods, parts = [], []
        for s, (hh, d) in enumerate(streams):
            rows = rows_of(chunk_of(d, c))
            prods.append(_mlstm_state_matmuls(q_ref[0, rows, qk_cols(hh)], v_ref[0, rows, v_cols(hh)],
                                              pw_ref[slot, s, 0], pw_ref[slot, s, 1], c_ref[s], n_ref[s]))
            parts.append(intra_matmuls(s, c_next))
        for s, (hh, d) in enumerate(streams):
            ci = chunk_of(d, c)
            hout, c_new, n_new, ms[s] = _mlstm_combine(
                pm_ref[slot, s, 0], pm_ref[slot, s, 1],
                tab_ref[s, _G_BTOT, pl.ds(ci, 1), :], tab_ref[s, _G_MW, pl.ds(ci, 1), :],
                prods[s], c_ref[s], n_ref[s], ms[s])
            h_ref[s, rows_of(ci), :] = hout
            c_ref[s] = c_new
            n_ref[s] = n_new
            park_weights(s, c_next, 1 - slot, parts[s])
        return tuple(ms)

    def finish(ci):
        rows = rows_of(ci)
        for hh in range(MLSTM_HEADS_PER_STEP):
            sig_o = 0.5 * jnp.tanh(o_ref[0, rows, v_cols(hh)]) + 0.5
            hs = sig_o * (h_ref[2 * hh, rows, :] + h_ref[2 * hh + 1, rows, :])
            ms = jnp.mean(hs * hs, axis=-1, keepdims=True)
            hs = hs * lax.rsqrt(ms + EPS) * ng_ref[:, v_cols(hh)]
            half_z = z_ref[0, rows, v_cols(hh)]
            out_ref[0, rows, v_cols(hh)] = (hs * (half_z * jnp.tanh(half_z) + half_z)).astype(out_ref.dtype)

    def step_and_finish(c, carry):
        carry = step(c, carry)
        finish(c)
        finish(n_chunks - 1 - c)
        return carry

    half = n_chunks // 2
    m0 = (jnp.zeros((1, LANES), F32),) * len(streams)
    ms_half = lax.fori_loop(0, half, step, m0)
    lax.fori_loop(half, n_chunks, step_and_finish, ms_half)


def _mlstm(qkv, oz, gates_t, gate_bias, norm_gain):
    b, s, _ = qkv.shape
    hps = MLSTM_HEADS_PER_STEP
    dk, dv = hps * MLSTM_QK_DIM, hps * MLSTM_V_DIM
    n_chunks = s // MLSTM_CHUNK
    n_streams = 2 * hps
    k_blk0 = MLSTM_QK_WIDTH // dk
    v_blk0 = 2 * MLSTM_QK_WIDTH // dv
    z_blk0 = MLSTM_WIDTH // dv
    return pl.pallas_call(
        functools.partial(_mlstm_kernel, seq=s),
        out_shape=jax.ShapeDtypeStruct((b, s, MLSTM_WIDTH), BF16),
        grid=(b, MLSTM_HEADS // hps),
        in_specs=[
            pl.BlockSpec(memory_space=pltpu.SMEM),
            pl.BlockSpec((1, s, dk), lambda i, h: (i, 0, h)),
            pl.BlockSpec((1, s, dk), lambda i, h: (i, 0, k_blk0 + h)),
            pl.BlockSpec((1, s, dv), lambda i, h: (i, 0, v_blk0 + h)),
            pl.BlockSpec((1, s, dv), lambda i, h: (i, 0, h)),
            pl.BlockSpec((1, s, dv), lambda i, h: (i, 0, z_blk0 + h)),
            pl.BlockSpec((N_GATE_SETS, hps, 1, n_chunks, MLSTM_CHUNK), lambda i, h: (0, h, i, 0, 0)),
            pl.BlockSpec((1, dv), lambda i, h: (0, h)),
        ],
        out_specs=pl.BlockSpec((1, s, dv), lambda i, h: (i, 0, h)),
        scratch_shapes=[
            pltpu.VMEM((n_streams, s, MLSTM_V_DIM), F32),
            pltpu.VMEM((n_streams, MLSTM_QK_DIM, MLSTM_V_DIM), F32),
            pltpu.VMEM((n_streams, MLSTM_QK_DIM, LANES), F32),
            pltpu.VMEM((n_streams, 4, n_chunks, MLSTM_CHUNK), F32),
            pltpu.VMEM((n_streams, MLSTM_CHUNK, MLSTM_CHUNK), F32),
            pltpu.VMEM((2, n_streams, 2, MLSTM_CHUNK, MLSTM_CHUNK), BF16),
            pltpu.VMEM((2, n_streams, 2, MLSTM_CHUNK, MLSTM_CHUNK), F32),
        ],
        compiler_params=_params("parallel", "parallel"),
        name="mlstm",
    )(gate_bias, qkv, qkv, qkv, oz, oz, gates_t, norm_gain)


def _row_tile(t, want):
    return want if t % want == 0 else t


def kernel(x, norm_gain, final_gain, even_w_in, even_w_out, q_norm_gain, k_norm_gain, conv_w, conv_b,
           lru_wa, lru_ba, lru_wx, lru_bx, lru_lambda, odd_w_in, odd_gate_bias, odd_norm_gain, odd_w_out):
    b, s, d = x.shape
    t = b * s
    assert s % (SCAN_SEGMENTS * SUBLANES) == 0 and s % (2 * MLSTM_CHUNK) == 0 and s % GRID_W == 0
    x2 = x.reshape(t, d)
    tn = 512

    proj0 = _norm_matmul(x2, norm_gain[0][None], even_w_in[0].astype(BF16), F32,
                         tm=_row_tile(t, 1024), tn=tn, half_from_col=EVEN_IN - LRU_WIDTH).reshape(b, s, EVEN_IN)
    cos_t, sin_t = _rope_tables(s)
    attn = _attention(proj0, q_norm_gain[0][None], k_norm_gain[0][None], cos_t, sin_t,
                      tq=_row_tile(s, 256))
    lru = _lru(proj0, 0.5 * conv_w[0], 0.5 * conv_b[0][None],
               _lru_gate_weights(lru_wa[0], lru_ba[0], lru_wx[0], lru_bx[0]), lru_lambda[0])
    x1, hn1 = _out_proj0(attn.reshape(t, ATTN_WIDTH), lru.reshape(t, LRU_WIDTH),
                         even_w_out[0].astype(BF16), x2, norm_gain[1][None], tm=_row_tile(t, 512))

    w1 = odd_w_in[0]
    qkv_w = 2 * MLSTM_QK_WIDTH + MLSTM_WIDTH
    oz_w = 2 * MLSTM_WIDTH
    tm1 = _row_tile(t, 1024)
    tn1 = 1024
    qkv = _matmul(hn1, w1, qkv_w, 0, BF16, tm1, tn1, "in_proj1_qkv").reshape(b, s, qkv_w)
    oz = _matmul(hn1, w1, oz_w, qkv_w // tn1, F32, tm1, tn1, "in_proj1_oz", w_scale=0.5).reshape(b, s, oz_w)
    gates_t = _matmul_t(w1[:, qkv_w + oz_w:].T.astype(BF16), hn1, tm1, "in_proj1_gates")
    gates_t = gates_t.reshape(N_GATE_SETS, MLSTM_HEADS, b, s // MLSTM_CHUNK, MLSTM_CHUNK)
    mix1 = _mlstm(qkv, oz, gates_t, odd_gate_bias[0], odd_norm_gain[0][None])
    out = _out_proj1(mix1.reshape(t, MLSTM_WIDTH), odd_w_out[0].astype(BF16), x1, final_gain[None],
                     tm=_row_tile(t, 512))
    return out.reshape(b, s, d)
```

```python
import functools

import jax
import jax.numpy as jnp
from jax import lax
from jax.experimental import pallas as pl
from jax.experimental.pallas import tpu as pltpu

F32 = jnp.float32
BF16 = jnp.bfloat16

EPS = 1e-6
GRID_W = 64
HEAD_DIM = 128
ATTN_HEADS = 8
KV_HEADS = 2
Q_GROUP = ATTN_HEADS // KV_HEADS
ATTN_WIDTH = ATTN_HEADS * HEAD_DIM
KV_WIDTH = KV_HEADS * HEAD_DIM
ROPE_THETA = 10000.0
LRU_WIDTH = 1024
LRU_BLOCKS = 8
LRU_BLOCK = 128
LRU_C = 8.0
CONV_W = 4
CONV_LEFT = 2
EVEN_IN = ATTN_WIDTH + 2 * KV_WIDTH + ATTN_WIDTH + 2 * LRU_WIDTH
MLSTM_HEADS = 8
MLSTM_V_DIM = 256
MLSTM_QK_DIM = 128
MLSTM_WIDTH = MLSTM_HEADS * MLSTM_V_DIM
MLSTM_QK_WIDTH = MLSTM_HEADS * MLSTM_QK_DIM
MLSTM_CHUNK = 128
N_GATE_SETS = 4

LOG2_E = 1.4426950408889634

LANES = 128
SUBLANES = 8
VMEM_LIMIT = 56 * 1024 * 1024
SCAN_SEGMENTS = 4 * SUBLANES


def _sigmoid(x):
    return 1.0 / (1.0 + jnp.exp(-x))


def _log_sigmoid(x):
    return jnp.minimum(x, 0.0) - jnp.log1p(jnp.exp(-jnp.abs(x)))


def _params(*sem):
    return pltpu.CompilerParams(dimension_semantics=sem, vmem_limit_bytes=VMEM_LIMIT)


def _rmsnorm_rows_to(x_ref, g_ref, dst_ref, rows, chunk):
    g = g_ref[...]

    def body(c, carry):
        r0 = pl.multiple_of(c * chunk, chunk)
        x = x_ref[pl.ds(r0, chunk), :]
        ms = jnp.mean(x * x, axis=-1, keepdims=True)
        dst_ref[pl.ds(r0, chunk), :] = (x * lax.rsqrt(ms + EPS) * g).astype(dst_ref.dtype)
        return carry

    lax.fori_loop(0, rows // chunk, body, 0)


def _norm_matmul_kernel(x_ref, g_ref, w_ref, o_ref, hn_ref, *, tm, half_from):
    j = pl.program_id(1)

    @pl.when(j == 0)
    def _():
        _rmsnorm_rows_to(x_ref, g_ref, hn_ref, tm, 64)

    scale = jnp.where(j >= half_from, 0.5, 1.0).astype(F32)
    o_ref[...] = (jnp.dot(hn_ref[...], w_ref[...], preferred_element_type=F32) * scale).astype(o_ref.dtype)


def _norm_matmul(x, gain, w, out_dtype, tm, tn, half_from_col):
    t, d = x.shape
    n = w.shape[1]
    return pl.pallas_call(
        functools.partial(_norm_matmul_kernel, tm=tm, half_from=half_from_col // tn),
        out_shape=jax.ShapeDtypeStruct((t, n), out_dtype),
        grid=(t // tm, n // tn),
        in_specs=[
            pl.BlockSpec((tm, d), lambda i, j: (i, 0)),
            pl.BlockSpec((1, d), lambda i, j: (0, 0)),
            pl.BlockSpec((d, tn), lambda i, j: (0, j)),
        ],
        out_specs=pl.BlockSpec((tm, tn), lambda i, j: (i, j)),
        scratch_shapes=[pltpu.VMEM((tm, d), BF16)],
        compiler_params=_params("parallel", "arbitrary"),
        name="norm_in_proj",
    )(x, gain, w)


def _matmul_kernel(a_ref, wt_ref, o_ref, wb_ref, *, w_scale, n_chunk):
    @pl.when(pl.program_id(1) == 0)
    def _():
        def cast(c, carry):
            rows = pl.ds(pl.multiple_of(c * n_chunk, n_chunk), n_chunk)
            wb_ref[rows, :] = (wt_ref[rows, :] * w_scale).astype(wb_ref.dtype)
            return carry

        lax.fori_loop(0, wt_ref.shape[0] // n_chunk, cast, 0)

    o_ref[...] = lax.dot_general(a_ref[...], wb_ref[...], (((1,), (1,)), ((), ())),
                                 preferred_element_type=F32).astype(o_ref.dtype)


def _matmul(a, wt, n, row_block0, out_dtype, tm, tn, name, w_scale=1.0):
    t, k = a.shape
    return pl.pallas_call(
        functools.partial(_matmul_kernel, w_scale=w_scale, n_chunk=64),
        out_shape=jax.ShapeDtypeStruct((t, n), out_dtype),
        grid=(n // tn, t // tm),
        in_specs=[
            pl.BlockSpec((tm, k), lambda j, i: (i, 0)),
            pl.BlockSpec((tn, k), lambda j, i: (j + row_block0, 0)),
        ],
        out_specs=pl.BlockSpec((tm, tn), lambda j, i: (i, j)),
        scratch_shapes=[pltpu.VMEM((tn, k), BF16)],
        compiler_params=_params("parallel", "arbitrary"),
        name=name,
    )(a, wt)


def _matmul_t_kernel(wt_ref, a_ref, o_ref):
    o_ref[...] = lax.dot_general(wt_ref[...], a_ref[...], (((1,), (1,)), ((), ())),
                                 preferred_element_type=F32)


def _matmul_t(wt, a, tm, name):
    n, k = wt.shape
    t = a.shape[0]
    return pl.pallas_call(
        _matmul_t_kernel,
        out_shape=jax.ShapeDtypeStruct((n, t), F32),
        grid=(t // tm,),
        in_specs=[
            pl.BlockSpec((n, k), lambda i: (0, 0)),
            pl.BlockSpec((tm, k), lambda i: (i, 0)),
        ],
        out_specs=pl.BlockSpec((n, tm), lambda i: (0, i)),
        compiler_params=_params("parallel"),
        name=name,
    )(wt, a)


def _out_proj_epilogue(y_ref, x_ref, g_ref, xo_ref, no_ref, rows, chunk):
    g = g_ref[...]

    def body(c, carry):
        r0 = pl.multiple_of(c * chunk, chunk)
        xn = x_ref[pl.ds(r0, chunk), :] + y_ref[pl.ds(r0, chunk), :]
        if xo_ref is not None:
            xo_ref[pl.ds(r0, chunk), :] = xn
        ms = jnp.mean(xn * xn, axis=-1, keepdims=True)
        no_ref[pl.ds(r0, chunk), :] = (xn * lax.rsqrt(ms + EPS) * g).astype(no_ref.dtype)
        return carry

    lax.fori_loop(0, rows // chunk, body, 0)


def _out_proj0_kernel(a_ref, b_ref, wa_ref, wb_ref, x_ref, g_ref, xo_ref, no_ref, y_ref, *, tm):
    y_ref[...] = (jnp.dot(a_ref[...], wa_ref[...], preferred_element_type=F32)
                  + jnp.dot(b_ref[...], wb_ref[...], preferred_element_type=F32))
    _out_proj_epilogue(y_ref, x_ref, g_ref, xo_ref, no_ref, tm, 64)


def _out_proj0(a, b, w, x, gain, tm):
    t, ka = a.shape
    kb = b.shape[1]
    d = w.shape[1]
    return pl.pallas_call(
        functools.partial(_out_proj0_kernel, tm=tm),
        out_shape=(jax.ShapeDtypeStruct((t, d), F32), jax.ShapeDtypeStruct((t, d), BF16)),
        grid=(t // tm,),
        in_specs=[
            pl.BlockSpec((tm, ka), lambda i: (i, 0)),
            pl.BlockSpec((tm, kb), lambda i: (i, 0)),
            pl.BlockSpec((ka, d), lambda i: (0, 0)),
            pl.BlockSpec((kb, d), lambda i: (ka // kb, 0)),
            pl.BlockSpec((tm, d), lambda i: (i, 0)),
            pl.BlockSpec((1, d), lambda i: (0, 0)),
        ],
        out_specs=(pl.BlockSpec((tm, d), lambda i: (i, 0)), pl.BlockSpec((tm, d), lambda i: (i, 0))),
        scratch_shapes=[pltpu.VMEM((tm, d), F32)],
        compiler_params=_params("parallel"),
        name="out_proj0",
    )(a, b, w, w, x, gain)


def _out_proj1_kernel(a_ref, w_ref, x_ref, g_ref, no_ref, y_ref, *, tm):
    y_ref[...] = jnp.dot(a_ref[...], w_ref[...], preferred_element_type=F32)
    _out_proj_epilogue(y_ref, x_ref, g_ref, None, no_ref, tm, 64)


def _out_proj1(a, w, x, gain, tm):
    t, k = a.shape
    d = w.shape[1]
    return pl.pallas_call(
        functools.partial(_out_proj1_kernel, tm=tm),
        out_shape=jax.ShapeDtypeStruct((t, d), F32),
        grid=(t // tm,),
        in_specs=[
            pl.BlockSpec((tm, k), lambda i: (i, 0)),
            pl.BlockSpec((k, d), lambda i: (0, 0)),
            pl.BlockSpec((tm, d), lambda i: (i, 0)),
            pl.BlockSpec((1, d), lambda i: (0, 0)),
        ],
        out_specs=pl.BlockSpec((tm, d), lambda i: (i, 0)),
        scratch_shapes=[pltpu.VMEM((tm, d), F32)],
        compiler_params=_params("parallel"),
        name="out_proj1",
    )(a, w, x, gain)


def _rope_tables(seq_len):
    half = HEAD_DIM // 2
    t = jnp.arange(seq_len)
    row = (t // GRID_W).astype(F32)
    col = (t % GRID_W).astype(F32)
    inv = ROPE_THETA ** (-jnp.arange(0, half, 2, dtype=F32) / half)
    ar = row[:, None] * inv
    ac = col[:, None] * inv
    cos_t = jnp.concatenate([jnp.cos(ar), jnp.cos(ar), jnp.cos(ac), jnp.cos(ac)], axis=-1)
    sin_t = jnp.concatenate([-jnp.sin(ar), jnp.sin(ar), -jnp.sin(ac), jnp.sin(ac)], axis=-1)
    return cos_t, sin_t


def _norm_rope(x, gain, cos_v, sin_v):
    quarter = HEAD_DIM // 4
    ms = jnp.mean(x * x, axis=-1, keepdims=True)
    xn = x * lax.rsqrt(ms + EPS) * gain
    lane = lax.broadcasted_iota(jnp.int32, xn.shape, 1)
    first = (lane % (2 * quarter)) < quarter
    partner = jnp.where(first, pltpu.roll(xn, HEAD_DIM - quarter, 1), pltpu.roll(xn, quarter, 1))
    return xn * cos_v + partner * sin_v


def _attention_kernel(q_ref, k_ref, v_ref, g_ref, qg_ref, kg_ref, cos_ref, sin_ref, o_ref,
                      kr_ref, vt_ref, qr_ref, s_ref, of_ref, *, seq, tq, kc):
    n_blk = seq // tq
    n_kc = seq // kc
    n_total = Q_GROUP * n_blk
    kg = kg_ref[...]
    qg = qg_ref[...]
    q_scale = HEAD_DIM ** -0.5 * LOG2_E
    eye = jnp.where(lax.broadcasted_iota(jnp.int32, (HEAD_DIM, HEAD_DIM), 0)
                    == lax.broadcasted_iota(jnp.int32, (HEAD_DIM, HEAD_DIM), 1), 1.0, 0.0).astype(BF16)

    def prep(c, carry):
        rows = pl.ds(pl.multiple_of(c * tq, tq), tq)
        cos_v, sin_v = cos_ref[rows, :], sin_ref[rows, :]
        kr_ref[rows, :] = _norm_rope(k_ref[0, rows, :], kg, cos_v, sin_v).astype(BF16)
        vt_ref[:, rows] = lax.dot_general(eye, v_ref[0, rows, :].astype(BF16), (((1,), (1,)), ((), ())),
                                          preferred_element_type=F32).astype(BF16)
        for g in range(Q_GROUP):
            q = _norm_rope(q_ref[0, rows, g * HEAD_DIM:(g + 1) * HEAD_DIM], qg, cos_v, sin_v) * q_scale
            qr_ref[pl.ds(pl.multiple_of(g * seq + c * tq, tq), tq), :] = q.astype(BF16)
        return carry

    lax.fori_loop(0, n_blk, prep, 0, unroll=2 if n_blk % 2 == 0 else 1)

    def blk_rows(i):
        return pl.ds(pl.multiple_of(i * tq, tq), tq)

    def scores_chunk(i, j):
        return lax.dot_general(kr_ref[j * kc:(j + 1) * kc, :], qr_ref[blk_rows(i), :],
                               (((1,), (1,)), ((), ())), preferred_element_type=F32)

    def col_max8(x):
        return jnp.max(x.reshape(x.shape[0] // SUBLANES, SUBLANES, x.shape[1]), axis=0)

    def col_sum8(x):
        return jnp.sum(x.reshape(x.shape[0] // SUBLANES, SUBLANES, x.shape[1]), axis=0)

    neg_inf8 = jnp.full((SUBLANES, tq), -jnp.inf, F32)
    m8 = neg_inf8
    for j in range(n_kc):
        s_t = scores_chunk(0, j)
        s_ref[0, j] = s_t
        m8 = jnp.maximum(m8, col_max8(s_t))

    def block(i, m8_cur, slot):
        nxt = jnp.minimum(i + 1, n_total - 1)
        m_cur = jnp.max(m8_cur, axis=0, keepdims=True)
        m8_next = neg_inf8
        l8 = jnp.zeros((SUBLANES, tq), F32)
        acc = jnp.zeros((HEAD_DIM, tq), F32)
        for j in range(n_kc):
            s_next = scores_chunk(nxt, j)
            s_ref[1 - slot, j] = s_next
            m8_next = jnp.maximum(m8_next, col_max8(s_next))
            p_t = jnp.exp2(s_ref[slot, j] - m_cur)
            l8 = l8 + col_sum8(p_t)
            acc = acc + jnp.dot(vt_ref[:, j * kc:(j + 1) * kc], p_t.astype(BF16),
                                preferred_element_type=F32)
        l = jnp.sum(l8, axis=0, keepdims=True)
        of_ref[blk_rows(i), :] = (acc * (1.0 / l)).T
        return m8_next

    def block_pair(ii, m8_cur):
        return block(2 * ii + 1, block(2 * ii, m8_cur, 0), 1)

    lax.fori_loop(0, n_total // 2, block_pair, m8)

    def finish(c, carry):
        rows = pl.ds(pl.multiple_of(c * tq, tq), tq)
        for g in range(Q_GROUP):
            cols = slice(g * HEAD_DIM, (g + 1) * HEAD_DIM)
            gate = g_ref[0, rows, cols]
            o = of_ref[pl.ds(pl.multiple_of(g * seq + c * tq, tq), tq), :]
            o_ref[0, rows, cols] = (o * (gate * _sigmoid(gate))).astype(o_ref.dtype)
        return carry

    lax.fori_loop(0, n_blk, finish, 0)


def _attention(proj, q_gain, k_gain, cos_t, sin_t, tq):
    b, s, _ = proj.shape
    gw = Q_GROUP * HEAD_DIM
    k_blk0 = ATTN_WIDTH // HEAD_DIM
    v_blk0 = (ATTN_WIDTH + KV_WIDTH) // HEAD_DIM
    g_blk0 = (ATTN_WIDTH + 2 * KV_WIDTH) // gw
    kc = _row_tile(s, 512)
    return pl.pallas_call(
        functools.partial(_attention_kernel, seq=s, tq=tq, kc=kc),
        out_shape=jax.ShapeDtypeStruct((b, s, ATTN_WIDTH), BF16),
        grid=(b, KV_HEADS),
        in_specs=[
            pl.BlockSpec((1, s, gw), lambda i, j: (i, 0, j)),
            pl.BlockSpec((1, s, HEAD_DIM), lambda i, j: (i, 0, k_blk0 + j)),
            pl.BlockSpec((1, s, HEAD_DIM), lambda i, j: (i, 0, v_blk0 + j)),
            pl.BlockSpec((1, s, gw), lambda i, j: (i, 0, g_blk0 + j)),
            pl.BlockSpec((1, HEAD_DIM), lambda i, j: (0, 0)),
            pl.BlockSpec((1, HEAD_DIM), lambda i, j: (0, 0)),
            pl.BlockSpec((s, HEAD_DIM), lambda i, j: (0, 0)),
            pl.BlockSpec((s, HEAD_DIM), lambda i, j: (0, 0)),
        ],
        out_specs=pl.BlockSpec((1, s, gw), lambda i, j: (i, 0, j)),
        scratch_shapes=[
            pltpu.VMEM((s, HEAD_DIM), BF16),
            pltpu.VMEM((HEAD_DIM, s), BF16),
            pltpu.VMEM((Q_GROUP * s, HEAD_DIM), BF16),
            pltpu.VMEM((2, s // kc, kc, tq), F32),
            pltpu.VMEM((Q_GROUP * s, HEAD_DIM), F32),
        ],
        compiler_params=_params("parallel", "parallel"),
        name="attention",
    )(proj, proj, proj, proj, q_gain, k_gain, cos_t, sin_t)


def _lru_kernel(x_ref, g_ref, cw_ref, cb_ref, w_ref, lam_ref, o_ref,
                xpad_ref, af_ref, uf_ref, ab_ref, ub_ref, hf_ref, pf_ref, hb_ref, pb_ref,
                *, seq, n_row_chunks):
    seg = seq // SCAN_SEGMENTS + SUBLANES // 2
    rows_padded = SCAN_SEGMENTS * seg
    chunk = seq // n_row_chunks
    pad = SUBLANES
    a_refs = (af_ref, ab_ref)
    u_refs = (uf_ref, ub_ref)

    zeros = jnp.zeros((pad, LRU_BLOCK), F32)
    xpad_ref[0:pad, :] = zeros
    xpad_ref[pad + seq:pad + seq + pad, :] = zeros
    xpad_ref[pad:pad + seq, :] = x_ref[0]
    for ref in a_refs + u_refs:
        ref[seq:rows_padded, :] = jnp.zeros((rows_padded - seq, LRU_BLOCK), F32)

    cw = cw_ref[...]
    cb = cb_ref[...]
    w = w_ref[0]
    decay = (-0.5 * LRU_C) * _log_sigmoid(lam_ref[...])
    bias_taps = jnp.where(lax.broadcasted_iota(jnp.int32, (chunk, LRU_BLOCK), 1) < 2, 1.0, 0.0).astype(BF16)

    for k in range(n_row_chunks):
        half_xc = cb
        for j in range(CONV_W):
            lo = pad + k * chunk + j - CONV_LEFT
            half_xc = half_xc + cw[j:j + 1, :] * xpad_ref[lo:lo + chunk, :]
        lhs = jnp.concatenate([half_xc.astype(BF16), bias_taps], axis=1)
        th = jnp.tanh(jnp.dot(lhs, w, preferred_element_type=F32))
        for d in range(2):
            c0 = 2 * d * LRU_BLOCK
            th_r = th[:, c0:c0 + LRU_BLOCK]
            th_i = th[:, c0 + LRU_BLOCK:c0 + 2 * LRU_BLOCK]
            dec = decay[d:d + 1, :]
            neg_log_a = dec * th_r + dec
            ix = half_xc * th_i + half_xc
            a = jnp.exp2(neg_log_a * -LOG2_E)
            one_minus_a2 = jnp.tanh(neg_log_a) * (a * a + 1.0)
            a_refs[d][k * chunk:(k + 1) * chunk, :] = a
            u_refs[d][k * chunk:(k + 1) * chunk, :] = jnp.exp2(0.5 * jnp.log2(one_minus_a2)) * ix

    n_groups = SCAN_SEGMENTS // SUBLANES

    def rows_at(t, grp):
        return pl.ds(t + grp * SUBLANES * seg, SUBLANES, stride=seg)

    def pass1(t, carry):
        hf, pf, hb, pb = (list(c) for c in carry)
        tb = seg - 1 - t
        for grp in range(n_groups):
            a = af_ref[rows_at(t, grp), :]
            hf[grp] = a * hf[grp] + uf_ref[rows_at(t, grp), :]
            pf[grp] = pf[grp] * a
            hf_ref[rows_at(t, grp), :] = hf[grp]
            pf_ref[rows_at(t, grp), :] = pf[grp]
            a = ab_ref[rows_at(tb, grp), :]
            hb[grp] = a * hb[grp] + ub_ref[rows_at(tb, grp), :]
            pb[grp] = pb[grp] * a
            hb_ref[rows_at(tb, grp), :] = hb[grp]
            pb_ref[rows_at(tb, grp), :] = pb[grp]
        return tuple(hf), tuple(pf), tuple(hb), tuple(pb)

    unroll = 2 if seg % 2 == 0 else 1
    zero = (jnp.zeros((SUBLANES, LRU_BLOCK), F32),) * n_groups
    one = (jnp.ones((SUBLANES, LRU_BLOCK), F32),) * n_groups
    hf, pf, hb, pb = lax.fori_loop(0, seg, pass1, (zero, one, zero, one), unroll=unroll)

    row = lax.broadcasted_iota(jnp.int32, (SUBLANES, LRU_BLOCK), 0)

    def entering_states(h_end, p_end, order):
        tiles = [jnp.zeros((SUBLANES, LRU_BLOCK), F32)] * n_groups
        c = jnp.zeros((1, LRU_BLOCK), F32)
        for k in order:
            grp, sub = divmod(k, SUBLANES)
            tiles[grp] = jnp.where(row == sub, c, tiles[grp])
            c = h_end[grp][sub:sub + 1, :] + p_end[grp][sub:sub + 1, :] * c
        return tiles

    cf = entering_states(hf, pf, range(SCAN_SEGMENTS))
    cbk = entering_states(hb, pb, reversed(range(SCAN_SEGMENTS)))

    def pass2(t, carry):
        for grp in range(n_groups):
            y = (hf_ref[rows_at(t, grp), :] + pf_ref[rows_at(t, grp), :] * cf[grp]
                 + hb_ref[rows_at(t, grp), :] + pb_ref[rows_at(t, grp), :] * cbk[grp])
            uf_ref[rows_at(t, grp), :] = y
        return carry

    lax.fori_loop(0, seg, pass2, 0, unroll=unroll)

    for k in range(n_row_chunks):
        rows = slice(k * chunk, (k + 1) * chunk)
        half_gate = g_ref[0, rows, :]
        silu = half_gate * jnp.tanh(half_gate) + half_gate
        o_ref[0, rows, :] = (uf_ref[rows, :] * silu).astype(o_ref.dtype)


def _lru_gate_weights(wa, ba, wx, bx):
    w_cat = jnp.concatenate([wa[0], wx[0], wa[1], wx[1]], axis=-1).astype(BF16)
    half_b = 0.5 * jnp.concatenate([ba[0].reshape(LRU_BLOCKS, 1, LRU_BLOCK), bx[0].reshape(LRU_BLOCKS, 1, LRU_BLOCK),
                                    ba[1].reshape(LRU_BLOCKS, 1, LRU_BLOCK), bx[1].reshape(LRU_BLOCKS, 1, LRU_BLOCK)],
                                   axis=-1)
    hi = half_b.astype(BF16)
    lo = (half_b - hi.astype(F32)).astype(BF16)
    fill = jnp.zeros((LRU_BLOCKS, LRU_BLOCK - 2, 4 * LRU_BLOCK), BF16)
    return jnp.concatenate([w_cat, hi, lo, fill], axis=1)


def _lru(proj, half_conv_w, half_conv_b, w_gates, lam):
    b, s, _ = proj.shape
    x_blk0 = (2 * ATTN_WIDTH + 2 * KV_WIDTH) // LRU_BLOCK
    g_blk0 = x_blk0 + LRU_BLOCKS
    seg_rows = s + SCAN_SEGMENTS * (SUBLANES // 2)
    n_row_chunks = 8 if s % (8 * SUBLANES) == 0 else 1
    return pl.pallas_call(
        functools.partial(_lru_kernel, seq=s, n_row_chunks=n_row_chunks),
        out_shape=jax.ShapeDtypeStruct((b, s, LRU_WIDTH), BF16),
        grid=(b, LRU_BLOCKS),
        in_specs=[
            pl.BlockSpec((1, s, LRU_BLOCK), lambda i, n: (i, 0, x_blk0 + n)),
            pl.BlockSpec((1, s, LRU_BLOCK), lambda i, n: (i, 0, g_blk0 + n)),
            pl.BlockSpec((CONV_W, LRU_BLOCK), lambda i, n: (0, n)),
            pl.BlockSpec((1, LRU_BLOCK), lambda i, n: (0, n)),
            pl.BlockSpec((1, 2 * LRU_BLOCK, 4 * LRU_BLOCK), lambda i, n: (n, 0, 0)),
            pl.BlockSpec((2, LRU_BLOCK), lambda i, n: (0, n)),
        ],
        out_specs=pl.BlockSpec((1, s, LRU_BLOCK), lambda i, n: (i, 0, n)),
        scratch_shapes=[pltpu.VMEM((s + 2 * SUBLANES, LRU_BLOCK), F32)]
        + [pltpu.VMEM((seg_rows, LRU_BLOCK), F32) for _ in range(8)],
        compiler_params=_params("parallel", "parallel"),
        name="rg_lru",
    )(proj, proj, half_conv_w, half_conv_b, w_gates, lam)


def _hi_lo(x):
    hi = x.astype(BF16)
    lo = (x - hi.astype(F32)).astype(BF16)
    return hi, lo


def _dot_nt(a, b):
    return lax.dot_general(a, b, (((1,), (1,)), ((), ())), preferred_element_type=F32)


def _lane_tile(x, width):
    return jnp.concatenate([x] * (width // x.shape[1]), axis=1) if width != x.shape[1] else x


_G_C, _G_WS, _G_BTOT, _G_MW = range(4)


def _mlstm_gate_table(gate_ref, bias_ref, hh, head, tab_ref, bt_ref):
    chunk = MLSTM_CHUNK
    ri = lax.broadcasted_iota(jnp.int32, (chunk, chunk), 0)
    si = lax.broadcasted_iota(jnp.int32, (chunk, chunk), 1)
    for d in range(2):
        ig = gate_ref[d, hh, 0] + bias_ref[d, head]
        lf = _log_sigmoid(gate_ref[2 + d, hh, 0] + bias_ref[2 + d, head])
        upper = jnp.where((ri >= si) if d else (ri <= si), 1.0, 0.0).astype(BF16)
        hi, lo = _hi_lo(lf)
        b = (jnp.dot(hi, upper, preferred_element_type=F32)
             + jnp.dot(lo, upper, preferred_element_type=F32))
        b_tot = jnp.broadcast_to(b[:, 0:1] if d else b[:, chunk - 1:chunk], b.shape)
        w = b_tot - b + ig
        mw = jnp.broadcast_to(jnp.max(w, axis=-1, keepdims=True), b.shape)
        tab = tab_ref.at[hh * 2 + d]
        tab[_G_C] = ig - b
        tab[_G_WS] = jnp.exp(w - mw)
        tab[_G_BTOT] = b_tot
        tab[_G_MW] = mw
        b_square = jnp.concatenate([b, jnp.zeros((chunk - b.shape[0], chunk), F32)], axis=0)
        bt_ref[hh * 2 + d] = b_square.T


def _chunk_mask(d, chunk):
    li = lax.broadcasted_iota(jnp.int32, (chunk, chunk), 0)
    si = lax.broadcasted_iota(jnp.int32, (chunk, chunk), 1)
    return (si >= li) if d else (si <= li)


def _mlstm_intra_weights(d, ci, qk, k, tab, bt):
    chunk = qk.shape[0]
    mask = _chunk_mask(d, chunk)
    row = pl.ds(ci, 1)
    c_masked = jnp.where(mask, tab[_G_C, row, :], -jnp.inf)
    cm = jnp.broadcast_to(jnp.max(c_masked, axis=-1, keepdims=True), c_masked.shape)
    p0 = (qk * jnp.exp(c_masked - cm)).astype(BF16)
    ktw = (k.astype(F32).T * tab[_G_WS, row, :]).astype(BF16)
    lane = lax.broadcasted_iota(jnp.int32, (chunk, chunk), 1)
    b_l = jnp.broadcast_to(jnp.sum(jnp.where(lane == ci, bt[...], 0.0), axis=-1, keepdims=True),
                           (chunk, chunk))
    return p0, ktw, cm, b_l


def _mlstm_state_matmuls(q, v, p0, ktw, c_state, n_state):
    chunk = q.shape[0]
    ones = jnp.ones((chunk, LANES), BF16)
    pk = jnp.concatenate([p0, ktw], axis=0)
    nv = jnp.dot(pk, v, preferred_element_type=F32)
    rs = jnp.dot(pk, ones, preferred_element_type=F32)
    qc = jnp.dot(q, c_state.astype(BF16), preferred_element_type=F32)
    qn = jnp.dot(q, n_state.astype(BF16), preferred_element_type=F32)
    return nv[:chunk], rs[:chunk], nv[chunk:], rs[chunk:], qc, qn


def _mlstm_combine(b_l, cm, b_tot, mw, prods, c_state, n_state, m):
    n0, r0, kv0, kn0, qc, qn = prods
    dv = n0.shape[1]
    scale = MLSTM_QK_DIM ** -0.5
    mx = jnp.maximum(m, cm)
    e1 = jnp.exp(cm - mx) * scale
    e2 = jnp.exp(m - mx) * scale
    den = e1 * r0 + e2 * qn
    inv = 1.0 / jnp.maximum(jnp.abs(den), jnp.exp(-(b_l + mx)))
    hout = _lane_tile(e1 * inv, dv) * n0 + _lane_tile(e2 * inv, dv) * qc

    m_new = jnp.maximum(b_tot + m, mw)
    d1 = jnp.exp(b_tot + m - m_new)
    d2 = jnp.exp(mw - m_new)
    c_new = _lane_tile(d1, dv) * c_state + _lane_tile(d2, dv) * kv0
    n_new = d1 * n_state + d2 * kn0
    return hout, c_new, n_new, m_new


MLSTM_HEADS_PER_STEP = 2


def _mlstm_kernel(bias_ref, q_ref, k_ref, v_ref, o_ref, z_ref, gate_ref, ng_ref, out_ref,
                  h_ref, c_ref, n_ref, tab_ref, bt_ref, pw_ref, pm_ref, *, seq):
    chunk = MLSTM_CHUNK
    n_chunks = seq // chunk
    dk, dv = MLSTM_QK_DIM, MLSTM_V_DIM
    streams = [(hh, d) for hh in range(MLSTM_HEADS_PER_STEP) for d in range(2)]
    for hh in range(MLSTM_HEADS_PER_STEP):
        _mlstm_gate_table(gate_ref, bias_ref, hh, pl.program_id(1) * MLSTM_HEADS_PER_STEP + hh,
                          tab_ref, bt_ref)
    c_ref[...] = jnp.zeros_like(c_ref)
    n_ref[...] = jnp.zeros_like(n_ref)

    def chunk_of(d, c):
        return (n_chunks - 1 - c) if d else c

    def rows_of(ci):
        return pl.ds(pl.multiple_of(ci * chunk, chunk), chunk)

    def qk_cols(hh):
        return slice(hh * dk, (hh + 1) * dk)

    def v_cols(hh):
        return slice(hh * dv, (hh + 1) * dv)

    def intra_matmuls(s, c):
        hh, d = streams[s]
        rows = rows_of(chunk_of(d, c))
        return _dot_nt(q_ref[0, rows, qk_cols(hh)], k_ref[0, rows, qk_cols(hh)])

    def park_weights(s, c, slot, qk):
        hh, d = streams[s]
        ci = chunk_of(d, c)
        p0, ktw, cm, b_l = _mlstm_intra_weights(d, ci, qk, k_ref[0, rows_of(ci), qk_cols(hh)],
                                                tab_ref.at[s], bt_ref.at[s])
        pw_ref[slot, s, 0] = p0
        pw_ref[slot, s, 1] = ktw
        pm_ref[slot, s, 0] = b_l
        pm_ref[slot, s, 1] = cm

    for s in range(len(streams)):
        park_weights(s, 0, 0, intra_matmuls(s, 0))

    def step(c, carry):
        ms = list(carry)
        slot = c % 2
        c_next = jnp.minimum(c + 1, n_chunks - 1)
        prods, parts = [], []
        for s, (hh, d) in enumerate(streams):
            rows = rows_of(chunk_of(d, c))
            prods.append(_mlstm_state_matmuls(q_ref[0, rows, qk_cols(hh)], v_ref[0, rows, v_cols(hh)],
                                              pw_ref[slot, s, 0], pw_ref[slot, s, 1], c_ref[s], n_ref[s]))
            parts.append(intra_matmuls(s, c_next))
        for s, (hh, d) in enumerate(streams):
            ci = chunk_of(d, c)
            hout, c_new, n_new, ms[s] = _mlstm_combine(
                pm_ref[slot, s, 0], pm_ref[slot, s, 1],
                tab_ref[s, _G_BTOT, pl.ds(ci, 1), :], tab_ref[s, _G_MW, pl.ds(ci, 1), :],
                prods[s], c_ref[s], n_ref[s], ms[s])
            h_ref[s, rows_of(ci), :] = hout
            c_ref[s] = c_new
            n_ref[s] = n_new
            park_weights(s, c_next, 1 - slot, parts[s])
        return tuple(ms)

    def finish(ci):
        rows = rows_of(ci)
        for hh in range(MLSTM_HEADS_PER_STEP):
            sig_o = 0.5 * jnp.tanh(o_ref[0, rows, v_cols(hh)]) + 0.5
            hs = sig_o * (h_ref[2 * hh, rows, :] + h_ref[2 * hh + 1, rows, :])
            ms = jnp.mean(hs * hs, axis=-1, keepdims=True)
            hs = hs * lax.rsqrt(ms + EPS) * ng_ref[:, v_cols(hh)]
            half_z = z_ref[0, rows, v_cols(hh)]
            out_ref[0, rows, v_cols(hh)] = (hs * (half_z * jnp.tanh(half_z) + half_z)).astype(out_ref.dtype)

    def step_and_finish(c, carry):
        carry = step(c, carry)
        finish(c)
        finish(n_chunks - 1 - c)
        return carry

    half = n_chunks // 2
    m0 = (jnp.zeros((1, LANES), F32),) * len(streams)
    ms_half = lax.fori_loop(0, half, step, m0)
    lax.fori_loop(half, n_chunks, step_and_finish, ms_half)


def _mlstm(qkv, oz, gates_t, gate_bias, norm_gain):
    b, s, _ = qkv.shape
    hps = MLSTM_HEADS_PER_STEP
    dk, dv = hps * MLSTM_QK_DIM, hps * MLSTM_V_DIM
    n_chunks = s // MLSTM_CHUNK
    n_streams = 2 * hps
    k_blk0 = MLSTM_QK_WIDTH // dk
    v_blk0 = 2 * MLSTM_QK_WIDTH // dv
    z_blk0 = MLSTM_WIDTH // dv
    return pl.pallas_call(
        functools.partial(_mlstm_kernel, seq=s),
        out_shape=jax.ShapeDtypeStruct((b, s, MLSTM_WIDTH), BF16),
        grid=(b, MLSTM_HEADS // hps),
        in_specs=[
            pl.BlockSpec(memory_space=pltpu.SMEM),
            pl.BlockSpec((1, s, dk), lambda i, h: (i, 0, h)),
            pl.BlockSpec((1, s, dk), lambda i, h: (i, 0, k_blk0 + h)),
            pl.BlockSpec((1, s, dv), lambda i, h: (i, 0, v_blk0 + h)),
            pl.BlockSpec((1, s, dv), lambda i, h: (i, 0, h)),
            pl.BlockSpec((1, s, dv), lambda i, h: (i, 0, z_blk0 + h)),
            pl.BlockSpec((N_GATE_SETS, hps, 1, n_chunks, MLSTM_CHUNK), lambda i, h: (0, h, i, 0, 0)),
            pl.BlockSpec((1, dv), lambda i, h: (0, h)),
        ],
        out_specs=pl.BlockSpec((1, s, dv), lambda i, h: (i, 0, h)),
        scratch_shapes=[
            pltpu.VMEM((n_streams, s, MLSTM_V_DIM), F32),
            pltpu.VMEM((n_streams, MLSTM_QK_DIM, MLSTM_V_DIM), F32),
            pltpu.VMEM((n_streams, MLSTM_QK_DIM, LANES), F32),
            pltpu.VMEM((n_streams, 4, n_chunks, MLSTM_CHUNK), F32),
            pltpu.VMEM((n_streams, MLSTM_CHUNK, MLSTM_CHUNK), F32),
            pltpu.VMEM((2, n_streams, 2, MLSTM_CHUNK, MLSTM_CHUNK), BF16),
            pltpu.VMEM((2, n_streams, 2, MLSTM_CHUNK, MLSTM_CHUNK), F32),
        ],
        compiler_params=_params("parallel", "parallel"),
        name="mlstm",
    )(gate_bias, qkv, qkv, qkv, oz, oz, gates_t, norm_gain)


def _row_tile(t, want):
    return want if t % want == 0 else t


def kernel(x, norm_gain, final_gain, even_w_in, even_w_out, q_norm_gain, k_norm_gain, conv_w, conv_b,
           lru_wa, lru_ba, lru_wx, lru_bx, lru_lambda, odd_w_in, odd_gate_bias, odd_norm_gain, odd_w_out):
    b, s, d = x.shape
    t = b * s
    assert s % (SCAN_SEGMENTS * SUBLANES) == 0 and s % (2 * MLSTM_CHUNK) == 0 and s % GRID_W == 0
    x2 = x.reshape(t, d)
    tn = 512

    proj0 = _norm_matmul(x2, norm_gain[0][None], even_w_in[0].astype(BF16), F32,
                         tm=_row_tile(t, 1024), tn=tn, half_from_col=EVEN_IN - LRU_WIDTH).reshape(b, s, EVEN_IN)
    cos_t, sin_t = _rope_tables(s)
    attn = _attention(proj0, q_norm_gain[0][None], k_norm_gain[0][None], cos_t, sin_t,
                      tq=_row_tile(s, 256))
    lru = _lru(proj0, 0.5 * conv_w[0], 0.5 * conv_b[0][None],
               _lru_gate_weights(lru_wa[0], lru_ba[0], lru_wx[0], lru_bx[0]), lru_lambda[0])
    x1, hn1 = _out_proj0(attn.reshape(t, ATTN_WIDTH), lru.reshape(t, LRU_WIDTH),
                         even_w_out[0].astype(BF16), x2, norm_gain[1][None], tm=_row_tile(t, 512))

    w1 = odd_w_in[0]
    qkv_w = 2 * MLSTM_QK_WIDTH + MLSTM_WIDTH
    oz_w = 2 * MLSTM_WIDTH
    tm1 = _row_tile(t, 1024)
    tn1 = 1024
    w1t = w1.T
    qkv = _matmul(hn1, w1t, qkv_w, 0, BF16, tm1, tn1, "in_proj1_qkv").reshape(b, s, qkv_w)
    oz = _matmul(hn1, w1t, oz_w, qkv_w // tn1, F32, tm1, tn1, "in_proj1_oz", w_scale=0.5).reshape(b, s, oz_w)
    gates_t = _matmul_t(w1t[qkv_w + oz_w:].astype(BF16), hn1, tm1, "in_proj1_gates")
    gates_t = gates_t.reshape(N_GATE_SETS, MLSTM_HEADS, b, s // MLSTM_CHUNK, MLSTM_CHUNK)
    mix1 = _mlstm(qkv, oz, gates_t, odd_gate_bias[0], odd_norm_gain[0][None])
    out = _out_proj1(mix1.reshape(t, MLSTM_WIDTH), odd_w_out[0].astype(BF16), x1, final_gain[None],
                     tm=_row_tile(t, 512))
    return out.reshape(b, s, d)
```

```python
import functools

import jax
import jax.numpy as jnp
from jax import lax
from jax.experimental import pallas as pl
from jax.experimental.pallas import tpu as pltpu

F32 = jnp.float32
BF16 = jnp.bfloat16

EPS = 1e-6
GRID_W = 64
HEAD_DIM = 128
ATTN_HEADS = 8
KV_HEADS = 2
Q_GROUP = ATTN_HEADS // KV_HEADS
ATTN_WIDTH = ATTN_HEADS * HEAD_DIM
KV_WIDTH = KV_HEADS * HEAD_DIM
ROPE_THETA = 10000.0
LRU_WIDTH = 1024
LRU_BLOCKS = 8
LRU_BLOCK = 128
LRU_C = 8.0
CONV_W = 4
CONV_LEFT = 2
EVEN_IN = ATTN_WIDTH + 2 * KV_WIDTH + ATTN_WIDTH + 2 * LRU_WIDTH
MLSTM_HEADS = 8
MLSTM_V_DIM = 256
MLSTM_QK_DIM = 128
MLSTM_WIDTH = MLSTM_HEADS * MLSTM_V_DIM
MLSTM_QK_WIDTH = MLSTM_HEADS * MLSTM_QK_DIM
MLSTM_CHUNK = 128
N_GATE_SETS = 4

LOG2_E = 1.4426950408889634

LANES = 128
SUBLANES = 8
VMEM_LIMIT = 56 * 1024 * 1024
SCAN_SEGMENTS = 4 * SUBLANES


def _sigmoid(x):
    return 1.0 / (1.0 + jnp.exp(-x))


def _log_sigmoid(x):
    return jnp.minimum(x, 0.0) - jnp.log1p(jnp.exp(-jnp.abs(x)))


def _params(*sem):
    return pltpu.CompilerParams(dimension_semantics=sem, vmem_limit_bytes=VMEM_LIMIT)


def _rmsnorm_rows_to(x_ref, g_ref, dst_ref, rows, chunk):
    g = g_ref[...]

    def body(c, carry):
        r0 = pl.multiple_of(c * chunk, chunk)
        x = x_ref[pl.ds(r0, chunk), :]
        ms = jnp.mean(x * x, axis=-1, keepdims=True)
        dst_ref[pl.ds(r0, chunk), :] = (x * lax.rsqrt(ms + EPS) * g).astype(dst_ref.dtype)
        return carry

    lax.fori_loop(0, rows // chunk, body, 0)


def _rmsnorm_kernel(x_ref, g_ref, o_ref, *, tm):
    _rmsnorm_rows_to(x_ref, g_ref, o_ref, tm, 64)


def _rmsnorm(x, gain, tm):
    t, d = x.shape
    return pl.pallas_call(
        functools.partial(_rmsnorm_kernel, tm=tm),
        out_shape=jax.ShapeDtypeStruct((t, d), BF16),
        grid=(t // tm,),
        in_specs=[pl.BlockSpec((tm, d), lambda i: (i, 0)), pl.BlockSpec((1, d), lambda i: (0, 0))],
        out_specs=pl.BlockSpec((tm, d), lambda i: (i, 0)),
        compiler_params=_params("parallel"),
        name="rmsnorm0",
    )(x, gain)


def _in_proj0_kernel(a_ref, w_ref, o_ref, wb_ref, *, half_from, k_chunk):
    @pl.when(pl.program_id(1) == 0)
    def _():
        scale = jnp.where(pl.program_id(0) >= half_from, 0.5, 1.0).astype(F32)

        def cast(c, carry):
            rows = pl.ds(pl.multiple_of(c * k_chunk, k_chunk), k_chunk)
            wb_ref[rows, :] = (w_ref[rows, :] * scale).astype(wb_ref.dtype)
            return carry

        lax.fori_loop(0, w_ref.shape[0] // k_chunk, cast, 0)

    o_ref[...] = jnp.dot(a_ref[...], wb_ref[...], preferred_element_type=F32).astype(o_ref.dtype)


def _in_proj0(a, w, out_dtype, tm, tn, half_from_col):
    t, k = a.shape
    n = w.shape[1]
    return pl.pallas_call(
        functools.partial(_in_proj0_kernel, half_from=half_from_col // tn, k_chunk=128),
        out_shape=jax.ShapeDtypeStruct((t, n), out_dtype),
        grid=(n // tn, t // tm),
        in_specs=[
            pl.BlockSpec((tm, k), lambda j, i: (i, 0)),
            pl.BlockSpec((k, tn), lambda j, i: (0, j)),
        ],
        out_specs=pl.BlockSpec((tm, tn), lambda j, i: (i, j)),
        scratch_shapes=[pltpu.VMEM((k, tn), BF16)],
        compiler_params=_params("parallel", "arbitrary"),
        name="in_proj0",
    )(a, w)


def _matmul_kernel(a_ref, wt_ref, o_ref, wb_ref, *, w_scale, n_chunk):
    @pl.when(pl.program_id(1) == 0)
    def _():
        def cast(c, carry):
            rows = pl.ds(pl.multiple_of(c * n_chunk, n_chunk), n_chunk)
            wb_ref[rows, :] = (wt_ref[rows, :] * w_scale).astype(wb_ref.dtype)
            return carry

        lax.fori_loop(0, wt_ref.shape[0] // n_chunk, cast, 0)

    o_ref[...] = lax.dot_general(a_ref[...], wb_ref[...], (((1,), (1,)), ((), ())),
                                 preferred_element_type=F32).astype(o_ref.dtype)


def _matmul(a, wt, n, row_block0, out_dtype, tm, tn, name, w_scale=1.0):
    t, k = a.shape
    return pl.pallas_call(
        functools.partial(_matmul_kernel, w_scale=w_scale, n_chunk=64),
        out_shape=jax.ShapeDtypeStruct((t, n), out_dtype),
        grid=(n // tn, t // tm),
        in_specs=[
            pl.BlockSpec((tm, k), lambda j, i: (i, 0)),
            pl.BlockSpec((tn, k), lambda j, i: (j + row_block0, 0)),
        ],
        out_specs=pl.BlockSpec((tm, tn), lambda j, i: (i, j)),
        scratch_shapes=[pltpu.VMEM((tn, k), BF16)],
        compiler_params=_params("parallel", "arbitrary"),
        name=name,
    )(a, wt)


def _out_proj_epilogue(y_ref, x_ref, g_ref, xo_ref, no_ref, rows, chunk):
    g = g_ref[...]

    def body(c, carry):
        r0 = pl.multiple_of(c * chunk, chunk)
        xn = x_ref[pl.ds(r0, chunk), :] + y_ref[pl.ds(r0, chunk), :]
        if xo_ref is not None:
            xo_ref[pl.ds(r0, chunk), :] = xn
        ms = jnp.mean(xn * xn, axis=-1, keepdims=True)
        no_ref[pl.ds(r0, chunk), :] = (xn * lax.rsqrt(ms + EPS) * g).astype(no_ref.dtype)
        return carry

    lax.fori_loop(0, rows // chunk, body, 0)


def _out_proj0_kernel(a_ref, b_ref, wa_ref, wb_ref, x_ref, g_ref, wg_ref, xo_ref, no_ref, go_ref, y_ref,
                      *, tm):
    y_ref[...] = (jnp.dot(a_ref[...], wa_ref[...], preferred_element_type=F32)
                  + jnp.dot(b_ref[...], wb_ref[...], preferred_element_type=F32))
    _out_proj_epilogue(y_ref, x_ref, g_ref, xo_ref, no_ref, tm, 64)
    go_ref[...] = lax.dot_general(wg_ref[...], no_ref[...], (((1,), (1,)), ((), ())),
                                  preferred_element_type=F32)


def _out_proj0(a, b, w, x, gain, wg_t, tm):
    t, ka = a.shape
    kb = b.shape[1]
    d = w.shape[1]
    ng = wg_t.shape[0]
    return pl.pallas_call(
        functools.partial(_out_proj0_kernel, tm=tm),
        out_shape=(jax.ShapeDtypeStruct((t, d), F32), jax.ShapeDtypeStruct((t, d), BF16),
                   jax.ShapeDtypeStruct((ng, t), F32)),
        grid=(t // tm,),
        in_specs=[
            pl.BlockSpec((tm, ka), lambda i: (i, 0)),
            pl.BlockSpec((tm, kb), lambda i: (i, 0)),
            pl.BlockSpec((ka, d), lambda i: (0, 0)),
            pl.BlockSpec((kb, d), lambda i: (ka // kb, 0)),
            pl.BlockSpec((tm, d), lambda i: (i, 0)),
            pl.BlockSpec((1, d), lambda i: (0, 0)),
            pl.BlockSpec((ng, d), lambda i: (0, 0)),
        ],
        out_specs=(pl.BlockSpec((tm, d), lambda i: (i, 0)), pl.BlockSpec((tm, d), lambda i: (i, 0)),
                   pl.BlockSpec((ng, tm), lambda i: (0, i))),
        scratch_shapes=[pltpu.VMEM((tm, d), F32)],
        compiler_params=_params("parallel"),
        name="out_proj0",
    )(a, b, w, w, x, gain, wg_t)


def _out_proj1_kernel(a_ref, w_ref, x_ref, g_ref, no_ref, y_ref, *, tm):
    y_ref[...] = jnp.dot(a_ref[...], w_ref[...], preferred_element_type=F32)
    _out_proj_epilogue(y_ref, x_ref, g_ref, None, no_ref, tm, 64)


def _out_proj1(a, w, x, gain, tm):
    t, k = a.shape
    d = w.shape[1]
    return pl.pallas_call(
        functools.partial(_out_proj1_kernel, tm=tm),
        out_shape=jax.ShapeDtypeStruct((t, d), F32),
        grid=(t // tm,),
        in_specs=[
            pl.BlockSpec((tm, k), lambda i: (i, 0)),
            pl.BlockSpec((k, d), lambda i: (0, 0)),
            pl.BlockSpec((tm, d), lambda i: (i, 0)),
            pl.BlockSpec((1, d), lambda i: (0, 0)),
        ],
        out_specs=pl.BlockSpec((tm, d), lambda i: (i, 0)),
        scratch_shapes=[pltpu.VMEM((tm, d), F32)],
        compiler_params=_params("parallel"),
        name="out_proj1",
    )(a, w, x, gain)


def _rope_tables(seq_len):
    half = HEAD_DIM // 2
    t = jnp.arange(seq_len)
    row = (t // GRID_W).astype(F32)
    col = (t % GRID_W).astype(F32)
    inv = ROPE_THETA ** (-jnp.arange(0, half, 2, dtype=F32) / half)
    ar = row[:, None] * inv
    ac = col[:, None] * inv
    cos_t = jnp.concatenate([jnp.cos(ar), jnp.cos(ar), jnp.cos(ac), jnp.cos(ac)], axis=-1)
    sin_t = jnp.concatenate([-jnp.sin(ar), jnp.sin(ar), -jnp.sin(ac), jnp.sin(ac)], axis=-1)
    return cos_t, sin_t


def _norm_rope(x, gain, cos_v, sin_v):
    quarter = HEAD_DIM // 4
    ms = jnp.mean(x * x, axis=-1, keepdims=True)
    xn = x * lax.rsqrt(ms + EPS) * gain
    lane = lax.broadcasted_iota(jnp.int32, xn.shape, 1)
    first = (lane % (2 * quarter)) < quarter
    partner = jnp.where(first, pltpu.roll(xn, HEAD_DIM - quarter, 1), pltpu.roll(xn, quarter, 1))
    return xn * cos_v + partner * sin_v


def _attention_kernel(q_ref, k_ref, v_ref, g_ref, qg_ref, kg_ref, cos_ref, sin_ref, o_ref,
                      kr_ref, vt_ref, qr_ref, s_ref, of_ref, *, seq, tq, kc):
    n_blk = seq // tq
    n_kc = seq // kc
    n_total = Q_GROUP * n_blk
    kg = kg_ref[...]
    qg = qg_ref[...]
    q_scale = HEAD_DIM ** -0.5 * LOG2_E
    eye = jnp.where(lax.broadcasted_iota(jnp.int32, (HEAD_DIM, HEAD_DIM), 0)
                    == lax.broadcasted_iota(jnp.int32, (HEAD_DIM, HEAD_DIM), 1), 1.0, 0.0).astype(BF16)

    def prep(c, carry):
        rows = pl.ds(pl.multiple_of(c * tq, tq), tq)
        cos_v, sin_v = cos_ref[rows, :], sin_ref[rows, :]
        kr_ref[rows, :] = _norm_rope(k_ref[0, rows, :], kg, cos_v, sin_v).astype(BF16)
        vt_ref[:, rows] = lax.dot_general(eye, v_ref[0, rows, :].astype(BF16), (((1,), (1,)), ((), ())),
                                          preferred_element_type=F32).astype(BF16)
        for g in range(Q_GROUP):
            q = _norm_rope(q_ref[0, rows, g * HEAD_DIM:(g + 1) * HEAD_DIM], qg, cos_v, sin_v) * q_scale
            qr_ref[pl.ds(pl.multiple_of(g * seq + c * tq, tq), tq), :] = q.astype(BF16)
        return carry

    lax.fori_loop(0, n_blk, prep, 0, unroll=2 if n_blk % 2 == 0 else 1)

    def blk_rows(i):
        return pl.ds(pl.multiple_of(i * tq, tq), tq)

    def scores_chunk(i, j):
        return lax.dot_general(kr_ref[j * kc:(j + 1) * kc, :], qr_ref[blk_rows(i), :],
                               (((1,), (1,)), ((), ())), preferred_element_type=F32)

    def col_max8(x):
        return jnp.max(x.reshape(x.shape[0] // SUBLANES, SUBLANES, x.shape[1]), axis=0)

    def col_sum8(x):
        return jnp.sum(x.reshape(x.shape[0] // SUBLANES, SUBLANES, x.shape[1]), axis=0)

    neg_inf8 = jnp.full((SUBLANES, tq), -jnp.inf, F32)
    m8 = neg_inf8
    for j in range(n_kc):
        s_t = scores_chunk(0, j)
        s_ref[0, j] = s_t
        m8 = jnp.maximum(m8, col_max8(s_t))

    def block(i, m8_cur, slot):
        nxt = jnp.minimum(i + 1, n_total - 1)
        m_cur = jnp.max(m8_cur, axis=0, keepdims=True)
        m8_next = neg_inf8
        l8 = jnp.zeros((SUBLANES, tq), F32)
        acc = jnp.zeros((HEAD_DIM, tq), F32)
        for j in range(n_kc):
            s_next = scores_chunk(nxt, j)
            s_ref[1 - slot, j] = s_next
            m8_next = jnp.maximum(m8_next, col_max8(s_next))
            p_t = jnp.exp2(s_ref[slot, j] - m_cur)
            l8 = l8 + col_sum8(p_t)
            acc = acc + jnp.dot(vt_ref[:, j * kc:(j + 1) * kc], p_t.astype(BF16),
                                preferred_element_type=F32)
        l = jnp.sum(l8, axis=0, keepdims=True)
        of_ref[blk_rows(i), :] = (acc * (1.0 / l)).T
        return m8_next

    def block_pair(ii, m8_cur):
        return block(2 * ii + 1, block(2 * ii, m8_cur, 0), 1)

    lax.fori_loop(0, n_total // 2, block_pair, m8)

    def finish(c, carry):
        rows = pl.ds(pl.multiple_of(c * tq, tq), tq)
        for g in range(Q_GROUP):
            cols = slice(g * HEAD_DIM, (g + 1) * HEAD_DIM)
            gate = g_ref[0, rows, cols]
            o = of_ref[pl.ds(pl.multiple_of(g * seq + c * tq, tq), tq), :]
            o_ref[0, rows, cols] = (o * (gate * _sigmoid(gate))).astype(o_ref.dtype)
        return carry

    lax.fori_loop(0, n_blk, finish, 0)


def _attention(proj, q_gain, k_gain, cos_t, sin_t, tq):
    b, s, _ = proj.shape
    gw = Q_GROUP * HEAD_DIM
    k_blk0 = ATTN_WIDTH // HEAD_DIM
    v_blk0 = (ATTN_WIDTH + KV_WIDTH) // HEAD_DIM
    g_blk0 = (ATTN_WIDTH + 2 * KV_WIDTH) // gw
    kc = _row_tile(s, 512)
    return pl.pallas_call(
        functools.partial(_attention_kernel, seq=s, tq=tq, kc=kc),
        out_shape=jax.ShapeDtypeStruct((b, s, ATTN_WIDTH), BF16),
        grid=(b, KV_HEADS),
        in_specs=[
            pl.BlockSpec((1, s, gw), lambda i, j: (i, 0, j)),
            pl.BlockSpec((1, s, HEAD_DIM), lambda i, j: (i, 0, k_blk0 + j)),
            pl.BlockSpec((1, s, HEAD_DIM), lambda i, j: (i, 0, v_blk0 + j)),
            pl.BlockSpec((1, s, gw), lambda i, j: (i, 0, g_blk0 + j)),
            pl.BlockSpec((1, HEAD_DIM), lambda i, j: (0, 0)),
            pl.BlockSpec((1, HEAD_DIM), lambda i, j: (0, 0)),
            pl.BlockSpec((s, HEAD_DIM), lambda i, j: (0, 0)),
            pl.BlockSpec((s, HEAD_DIM), lambda i, j: (0, 0)),
        ],
        out_specs=pl.BlockSpec((1, s, gw), lambda i, j: (i, 0, j)),
        scratch_shapes=[
            pltpu.VMEM((s, HEAD_DIM), BF16),
            pltpu.VMEM((HEAD_DIM, s), BF16),
            pltpu.VMEM((Q_GROUP * s, HEAD_DIM), BF16),
            pltpu.VMEM((2, s // kc, kc, tq), F32),
            pltpu.VMEM((Q_GROUP * s, HEAD_DIM), F32),
        ],
        compiler_params=_params("parallel", "parallel"),
        name="attention",
    )(proj, proj, proj, proj, q_gain, k_gain, cos_t, sin_t)


def _lru_kernel(x_ref, g_ref, cw_ref, cb_ref, w_ref, lam_ref, o_ref,
                xpad_ref, af_ref, uf_ref, ab_ref, ub_ref, hf_ref, pf_ref, hb_ref, pb_ref,
                *, seq, n_row_chunks):
    seg = seq // SCAN_SEGMENTS + SUBLANES // 2
    rows_padded = SCAN_SEGMENTS * seg
    chunk = seq // n_row_chunks
    pad = SUBLANES
    a_refs = (af_ref, ab_ref)
    u_refs = (uf_ref, ub_ref)

    zeros = jnp.zeros((pad, LRU_BLOCK), F32)
    xpad_ref[0:pad, :] = zeros
    xpad_ref[pad + seq:pad + seq + pad, :] = zeros
    xpad_ref[pad:pad + seq, :] = x_ref[0]
    for ref in a_refs + u_refs:
        ref[seq:rows_padded, :] = jnp.zeros((rows_padded - seq, LRU_BLOCK), F32)

    cw = cw_ref[...]
    cb = cb_ref[...]
    w = w_ref[0]
    decay = (-0.5 * LRU_C) * _log_sigmoid(lam_ref[...])
    bias_taps = jnp.where(lax.broadcasted_iota(jnp.int32, (chunk, LRU_BLOCK), 1) < 2, 1.0, 0.0).astype(BF16)

    for k in range(n_row_chunks):
        half_xc = cb
        for j in range(CONV_W):
            lo = pad + k * chunk + j - CONV_LEFT
            half_xc = half_xc + cw[j:j + 1, :] * xpad_ref[lo:lo + chunk, :]
        lhs = jnp.concatenate([half_xc.astype(BF16), bias_taps], axis=1)
        th = jnp.tanh(jnp.dot(lhs, w, preferred_element_type=F32))
        for d in range(2):
            c0 = 2 * d * LRU_BLOCK
            th_r = th[:, c0:c0 + LRU_BLOCK]
            th_i = th[:, c0 + LRU_BLOCK:c0 + 2 * LRU_BLOCK]
            dec = decay[d:d + 1, :]
            neg_log_a = dec * th_r + dec
            ix = half_xc * th_i + half_xc
            a = jnp.exp2(neg_log_a * -LOG2_E)
            one_minus_a2 = jnp.tanh(neg_log_a) * (a * a + 1.0)
            a_refs[d][k * chunk:(k + 1) * chunk, :] = a
            u_refs[d][k * chunk:(k + 1) * chunk, :] = jnp.exp2(0.5 * jnp.log2(one_minus_a2)) * ix

    n_groups = SCAN_SEGMENTS // SUBLANES

    def rows_at(t, grp):
        return pl.ds(t + grp * SUBLANES * seg, SUBLANES, stride=seg)

    def pass1(t, carry):
        hf, pf, hb, pb = (list(c) for c in carry)
        tb = seg - 1 - t
        for grp in range(n_groups):
            a = af_ref[rows_at(t, grp), :]
            hf[grp] = a * hf[grp] + uf_ref[rows_at(t, grp), :]
            pf[grp] = pf[grp] * a
            hf_ref[rows_at(t, grp), :] = hf[grp]
            pf_ref[rows_at(t, grp), :] = pf[grp]
            a = ab_ref[rows_at(tb, grp), :]
            hb[grp] = a * hb[grp] + ub_ref[rows_at(tb, grp), :]
            pb[grp] = pb[grp] * a
            hb_ref[rows_at(tb, grp), :] = hb[grp]
            pb_ref[rows_at(tb, grp), :] = pb[grp]
        return tuple(hf), tuple(pf), tuple(hb), tuple(pb)

    unroll = 2 if seg % 2 == 0 else 1
    zero = (jnp.zeros((SUBLANES, LRU_BLOCK), F32),) * n_groups
    one = (jnp.ones((SUBLANES, LRU_BLOCK), F32),) * n_groups
    hf, pf, hb, pb = lax.fori_loop(0, seg, pass1, (zero, one, zero, one), unroll=unroll)

    row = lax.broadcasted_iota(jnp.int32, (SUBLANES, LRU_BLOCK), 0)

    def entering_states(h_end, p_end, order):
        tiles = [jnp.zeros((SUBLANES, LRU_BLOCK), F32)] * n_groups
        c = jnp.zeros((1, LRU_BLOCK), F32)
        for k in order:
            grp, sub = divmod(k, SUBLANES)
            tiles[grp] = jnp.where(row == sub, c, tiles[grp])
            c = h_end[grp][sub:sub + 1, :] + p_end[grp][sub:sub + 1, :] * c
        return tiles

    cf = entering_states(hf, pf, range(SCAN_SEGMENTS))
    cbk = entering_states(hb, pb, reversed(range(SCAN_SEGMENTS)))

    def pass2(t, carry):
        for grp in range(n_groups):
            y = (hf_ref[rows_at(t, grp), :] + pf_ref[rows_at(t, grp), :] * cf[grp]
                 + hb_ref[rows_at(t, grp), :] + pb_ref[rows_at(t, grp), :] * cbk[grp])
            uf_ref[rows_at(t, grp), :] = y
        return carry

    lax.fori_loop(0, seg, pass2, 0, unroll=unroll)

    for k in range(n_row_chunks):
        rows = slice(k * chunk, (k + 1) * chunk)
        half_gate = g_ref[0, rows, :]
        silu = half_gate * jnp.tanh(half_gate) + half_gate
        o_ref[0, rows, :] = (uf_ref[rows, :] * silu).astype(o_ref.dtype)


def _lru_gate_weights(wa, ba, wx, bx):
    w_cat = jnp.concatenate([wa[0], wx[0], wa[1], wx[1]], axis=-1).astype(BF16)
    half_b = 0.5 * jnp.concatenate([ba[0].reshape(LRU_BLOCKS, 1, LRU_BLOCK), bx[0].reshape(LRU_BLOCKS, 1, LRU_BLOCK),
                                    ba[1].reshape(LRU_BLOCKS, 1, LRU_BLOCK), bx[1].reshape(LRU_BLOCKS, 1, LRU_BLOCK)],
                                   axis=-1)
    hi = half_b.astype(BF16)
    lo = (half_b - hi.astype(F32)).astype(BF16)
    fill = jnp.zeros((LRU_BLOCKS, LRU_BLOCK - 2, 4 * LRU_BLOCK), BF16)
    return jnp.concatenate([w_cat, hi, lo, fill], axis=1)


def _lru(proj, half_conv_w, half_conv_b, w_gates, lam):
    b, s, _ = proj.shape
    x_blk0 = (2 * ATTN_WIDTH + 2 * KV_WIDTH) // LRU_BLOCK
    g_blk0 = x_blk0 + LRU_BLOCKS
    seg_rows = s + SCAN_SEGMENTS * (SUBLANES // 2)
    n_row_chunks = 8 if s % (8 * SUBLANES) == 0 else 1
    return pl.pallas_call(
        functools.partial(_lru_kernel, seq=s, n_row_chunks=n_row_chunks),
        out_shape=jax.ShapeDtypeStruct((b, s, LRU_WIDTH), BF16),
        grid=(b, LRU_BLOCKS),
        in_specs=[
            pl.BlockSpec((1, s, LRU_BLOCK), lambda i, n: (i, 0, x_blk0 + n)),
            pl.BlockSpec((1, s, LRU_BLOCK), lambda i, n: (i, 0, g_blk0 + n)),
            pl.BlockSpec((CONV_W, LRU_BLOCK), lambda i, n: (0, n)),
            pl.BlockSpec((1, LRU_BLOCK), lambda i, n: (0, n)),
            pl.BlockSpec((1, 2 * LRU_BLOCK, 4 * LRU_BLOCK), lambda i, n: (n, 0, 0)),
            pl.BlockSpec((2, LRU_BLOCK), lambda i, n: (0, n)),
        ],
        out_specs=pl.BlockSpec((1, s, LRU_BLOCK), lambda i, n: (i, 0, n)),
        scratch_shapes=[pltpu.VMEM((s + 2 * SUBLANES, LRU_BLOCK), F32)]
        + [pltpu.VMEM((seg_rows, LRU_BLOCK), F32) for _ in range(8)],
        compiler_params=_params("parallel", "parallel"),
        name="rg_lru",
    )(proj, proj, half_conv_w, half_conv_b, w_gates, lam)


def _hi_lo(x):
    hi = x.astype(BF16)
    lo = (x - hi.astype(F32)).astype(BF16)
    return hi, lo


def _dot_nt(a, b):
    return lax.dot_general(a, b, (((1,), (1,)), ((), ())), preferred_element_type=F32)


def _lane_tile(x, width):
    return jnp.concatenate([x] * (width // x.shape[1]), axis=1) if width != x.shape[1] else x


_G_C, _G_WS, _G_BTOT, _G_MW = range(4)


def _mlstm_gate_table(gate_ref, bias_ref, hh, head, tab_ref, bt_ref):
    chunk = MLSTM_CHUNK
    ri = lax.broadcasted_iota(jnp.int32, (chunk, chunk), 0)
    si = lax.broadcasted_iota(jnp.int32, (chunk, chunk), 1)
    for d in range(2):
        ig = gate_ref[d, hh, 0] + bias_ref[d, head]
        lf = _log_sigmoid(gate_ref[2 + d, hh, 0] + bias_ref[2 + d, head])
        upper = jnp.where((ri >= si) if d else (ri <= si), 1.0, 0.0).astype(BF16)
        hi, lo = _hi_lo(lf)
        b = (jnp.dot(hi, upper, preferred_element_type=F32)
             + jnp.dot(lo, upper, preferred_element_type=F32))
        b_tot = jnp.broadcast_to(b[:, 0:1] if d else b[:, chunk - 1:chunk], b.shape)
        w = b_tot - b + ig
        mw = jnp.broadcast_to(jnp.max(w, axis=-1, keepdims=True), b.shape)
        tab = tab_ref.at[hh * 2 + d]
        tab[_G_C] = ig - b
        tab[_G_WS] = jnp.exp(w - mw)
        tab[_G_BTOT] = b_tot
        tab[_G_MW] = mw
        b_square = jnp.concatenate([b, jnp.zeros((chunk - b.shape[0], chunk), F32)], axis=0)
        bt_ref[hh * 2 + d] = b_square.T


def _chunk_mask(d, chunk):
    li = lax.broadcasted_iota(jnp.int32, (chunk, chunk), 0)
    si = lax.broadcasted_iota(jnp.int32, (chunk, chunk), 1)
    return (si >= li) if d else (si <= li)


def _mlstm_intra_weights(d, ci, qk, k, tab, bt):
    chunk = qk.shape[0]
    mask = _chunk_mask(d, chunk)
    row = pl.ds(ci, 1)
    c_masked = jnp.where(mask, tab[_G_C, row, :], -jnp.inf)
    cm = jnp.broadcast_to(jnp.max(c_masked, axis=-1, keepdims=True), c_masked.shape)
    p0 = (qk * jnp.exp(c_masked - cm)).astype(BF16)
    ktw = (k.astype(F32).T * tab[_G_WS, row, :]).astype(BF16)
    lane = lax.broadcasted_iota(jnp.int32, (chunk, chunk), 1)
    b_l = jnp.broadcast_to(jnp.sum(jnp.where(lane == ci, bt[...], 0.0), axis=-1, keepdims=True),
                           (chunk, chunk))
    return p0, ktw, cm, b_l


def _mlstm_state_matmuls(q, v, p0, ktw, c_state, n_state):
    chunk = q.shape[0]
    ones = jnp.ones((chunk, LANES), BF16)
    pk = jnp.concatenate([p0, ktw], axis=0)
    nv = jnp.dot(pk, v, preferred_element_type=F32)
    rs = jnp.dot(pk, ones, preferred_element_type=F32)
    qc = jnp.dot(q, c_state.astype(BF16), preferred_element_type=F32)
    qn = jnp.dot(q, n_state.astype(BF16), preferred_element_type=F32)
    return nv[:chunk], rs[:chunk], nv[chunk:], rs[chunk:], qc, qn


def _mlstm_combine(b_l, cm, b_tot, mw, prods, c_state, n_state, m):
    n0, r0, kv0, kn0, qc, qn = prods
    dv = n0.shape[1]
    scale = MLSTM_QK_DIM ** -0.5
    mx = jnp.maximum(m, cm)
    e1 = jnp.exp(cm - mx) * scale
    e2 = jnp.exp(m - mx) * scale
    den = e1 * r0 + e2 * qn
    inv = 1.0 / jnp.maximum(jnp.abs(den), jnp.exp(-(b_l + mx)))
    hout = _lane_tile(e1 * inv, dv) * n0 + _lane_tile(e2 * inv, dv) * qc

    m_new = jnp.maximum(b_tot + m, mw)
    d1 = jnp.exp(b_tot + m - m_new)
    d2 = jnp.exp(mw - m_new)
    c_new = _lane_tile(d1, dv) * c_state + _lane_tile(d2, dv) * kv0
    n_new = d1 * n_state + d2 * kn0
    return hout, c_new, n_new, m_new


MLSTM_HEADS_PER_STEP = 2


def _mlstm_kernel(bias_ref, q_ref, k_ref, v_ref, o_ref, z_ref, gate_ref, ng_ref, out_ref,
                  h_ref, c_ref, n_ref, tab_ref, bt_ref, pw_ref, pm_ref, *, seq):
    chunk = MLSTM_CHUNK
    n_chunks = seq // chunk
    dk, dv = MLSTM_QK_DIM, MLSTM_V_DIM
    streams = [(hh, d) for hh in range(MLSTM_HEADS_PER_STEP) for d in range(2)]
    for hh in range(MLSTM_HEADS_PER_STEP):
        _mlstm_gate_table(gate_ref, bias_ref, hh, pl.program_id(1) * MLSTM_HEADS_PER_STEP + hh,
                          tab_ref, bt_ref)
    c_ref[...] = jnp.zeros_like(c_ref)
    n_ref[...] = jnp.zeros_like(n_ref)

    def chunk_of(d, c):
        return (n_chunks - 1 - c) if d else c

    def rows_of(ci):
        return pl.ds(pl.multiple_of(ci * chunk, chunk), chunk)

    def qk_cols(hh):
        return slice(hh * dk, (hh + 1) * dk)

    def v_cols(hh):
        return slice(hh * dv, (hh + 1) * dv)

    def intra_matmuls(s, c):
        hh, d = streams[s]
        rows = rows_of(chunk_of(d, c))
        return _dot_nt(q_ref[0, rows, qk_cols(hh)], k_ref[0, rows, qk_cols(hh)])

    def park_weights(s, c, slot, qk):
        hh, d = streams[s]
        ci = chunk_of(d, c)
        p0, ktw, cm, b_l = _mlstm_intra_weights(d, ci, qk, k_ref[0, rows_of(ci), qk_cols(hh)],
                                                tab_ref.at[s], bt_ref.at[s])
        pw_ref[slot, s, 0] = p0
        pw_ref[slot, s, 1] = ktw
        pm_ref[slot, s, 0] = b_l
        pm_ref[slot, s, 1] = cm

    for s in range(len(streams)):
        park_weights(s, 0, 0, intra_matmuls(s, 0))

    def step(c, carry):
        ms = list(carry)
        slot = c % 2
        c_next = jnp.minimum(c + 1, n_chunks - 1)
        prods, parts = [], []
        for s, (hh, d) in enumerate(streams):
            rows = rows_of(chunk_of(d, c))
            prods.append(_mlstm_state_matmuls(q_ref[0, rows, qk_cols(hh)], v_ref[0, rows, v_cols(hh)],
                                              pw_ref[slot, s, 0], pw_ref[slot, s, 1], c_ref[s], n_ref[s]))
            parts.append(intra_matmuls(s, c_next))
        for s, (hh, d) in enumerate(streams):
            ci = chunk_of(d, c)
            hout, c_new, n_new, ms[s] = _mlstm_combine(
                pm_ref[slot, s, 0], pm_ref[slot, s, 1],
                tab_ref[s, _G_BTOT, pl.ds(ci, 1), :], tab_ref[s, _G_MW, pl.ds(ci, 1), :],
                prods[s], c_ref[s], n_ref[s], ms[s])
            h_ref[s, rows_of(ci), :] = hout
            c_ref[s] = c_new
            n_ref[s] = n_new
            park_weights(s, c_next, 1 - slot, parts[s])
        return tuple(ms)

    def finish(ci):
        rows = rows_of(ci)
        for hh in range(MLSTM_HEADS_PER_STEP):
            sig_o = 0.5 * jnp.tanh(o_ref[0, rows, v_cols(hh)]) + 0.5
            hs = sig_o * (h_ref[2 * hh, rows, :] + h_ref[2 * hh + 1, rows, :])
            ms = jnp.mean(hs * hs, axis=-1, keepdims=True)
            hs = hs * lax.rsqrt(ms + EPS) * ng_ref[:, v_cols(hh)]
            half_z = z_ref[0, rows, v_cols(hh)]
            out_ref[0, rows, v_cols(hh)] = (hs * (half_z * jnp.tanh(half_z) + half_z)).astype(out_ref.dtype)

    def step_and_finish(c, carry):
        carry = step(c, carry)
        finish(c)
        finish(n_chunks - 1 - c)
        return carry

    half = n_chunks // 2
    m0 = (jnp.zeros((1, LANES), F32),) * len(streams)
    ms_half = lax.fori_loop(0, half, step, m0)
    lax.fori_loop(half, n_chunks, step_and_finish, ms_half)


def _mlstm(qkv, oz, gates_t, gate_bias, norm_gain):
    b, s, _ = qkv.shape
    hps = MLSTM_HEADS_PER_STEP
    dk, dv = hps * MLSTM_QK_DIM, hps * MLSTM_V_DIM
    n_chunks = s // MLSTM_CHUNK
    n_streams = 2 * hps
    k_blk0 = MLSTM_QK_WIDTH // dk
    v_blk0 = 2 * MLSTM_QK_WIDTH // dv
    z_blk0 = MLSTM_WIDTH // dv
    return pl.pallas_call(
        functools.partial(_mlstm_kernel, seq=s),
        out_shape=jax.ShapeDtypeStruct((b, s, MLSTM_WIDTH), BF16),
        grid=(b, MLSTM_HEADS // hps),
        in_specs=[
            pl.BlockSpec(memory_space=pltpu.SMEM),
            pl.BlockSpec((1, s, dk), lambda i, h: (i, 0, h)),
            pl.BlockSpec((1, s, dk), lambda i, h: (i, 0, k_blk0 + h)),
            pl.BlockSpec((1, s, dv), lambda i, h: (i, 0, v_blk0 + h)),
            pl.BlockSpec((1, s, dv), lambda i, h: (i, 0, h)),
            pl.BlockSpec((1, s, dv), lambda i, h: (i, 0, z_blk0 + h)),
            pl.BlockSpec((N_GATE_SETS, hps, 1, n_chunks, MLSTM_CHUNK), lambda i, h: (0, h, i, 0, 0)),
            pl.BlockSpec((1, dv), lambda i, h: (0, h)),
        ],
        out_specs=pl.BlockSpec((1, s, dv), lambda i, h: (i, 0, h)),
        scratch_shapes=[
            pltpu.VMEM((n_streams, s, MLSTM_V_DIM), F32),
            pltpu.VMEM((n_streams, MLSTM_QK_DIM, MLSTM_V_DIM), F32),
            pltpu.VMEM((n_streams, MLSTM_QK_DIM, LANES), F32),
            pltpu.VMEM((n_streams, 4, n_chunks, MLSTM_CHUNK), F32),
            pltpu.VMEM((n_streams, MLSTM_CHUNK, MLSTM_CHUNK), F32),
            pltpu.VMEM((2, n_streams, 2, MLSTM_CHUNK, MLSTM_CHUNK), BF16),
            pltpu.VMEM((2, n_streams, 2, MLSTM_CHUNK, MLSTM_CHUNK), F32),
        ],
        compiler_params=_params("parallel", "parallel"),
        name="mlstm",
    )(gate_bias, qkv, qkv, qkv, oz, oz, gates_t, norm_gain)


def _row_tile(t, want):
    return want if t % want == 0 else t


def kernel(x, norm_gain, final_gain, even_w_in, even_w_out, q_norm_gain, k_norm_gain, conv_w, conv_b,
           lru_wa, lru_ba, lru_wx, lru_bx, lru_lambda, odd_w_in, odd_gate_bias, odd_norm_gain, odd_w_out):
    b, s, d = x.shape
    t = b * s
    assert s % (SCAN_SEGMENTS * SUBLANES) == 0 and s % (2 * MLSTM_CHUNK) == 0 and s % GRID_W == 0
    x2 = x.reshape(t, d)
    tn = 512

    hn0 = _rmsnorm(x2, norm_gain[0][None], tm=_row_tile(t, 512))
    proj0 = _in_proj0(hn0, even_w_in[0], F32, tm=_row_tile(t, 1024), tn=tn,
                      half_from_col=EVEN_IN - LRU_WIDTH).reshape(b, s, EVEN_IN)
    cos_t, sin_t = _rope_tables(s)
    attn = _attention(proj0, q_norm_gain[0][None], k_norm_gain[0][None], cos_t, sin_t,
                      tq=_row_tile(s, 256))
    lru = _lru(proj0, 0.5 * conv_w[0], 0.5 * conv_b[0][None],
               _lru_gate_weights(lru_wa[0], lru_ba[0], lru_wx[0], lru_bx[0]), lru_lambda[0])
    qkv_w = 2 * MLSTM_QK_WIDTH + MLSTM_WIDTH
    oz_w = 2 * MLSTM_WIDTH
    w1t = odd_w_in[0].T
    x1, hn1, gates_t = _out_proj0(attn.reshape(t, ATTN_WIDTH), lru.reshape(t, LRU_WIDTH),
                                  even_w_out[0].astype(BF16), x2, norm_gain[1][None],
                                  w1t[qkv_w + oz_w:].astype(BF16), tm=_row_tile(t, 512))

    tm1 = _row_tile(t, 1024)
    tn1 = 1024
    qkv = _matmul(hn1, w1t, qkv_w, 0, BF16, tm1, tn1, "in_proj1_qkv").reshape(b, s, qkv_w)
    oz = _matmul(hn1, w1t, oz_w, qkv_w // tn1, F32, tm1, tn1, "in_proj1_oz", w_scale=0.5).reshape(b, s, oz_w)
    gates_t = gates_t.reshape(N_GATE_SETS, MLSTM_HEADS, b, s // MLSTM_CHUNK, MLSTM_CHUNK)
    mix1 = _mlstm(qkv, oz, gates_t, odd_gate_bias[0], odd_norm_gain[0][None])
    out = _out_proj1(mix1.reshape(t, MLSTM_WIDTH), odd_w_out[0].astype(BF16), x1, final_gain[None],
                     tm=_row_tile(t, 512))
    return out.reshape(b, s, d)
```

```python
import functools

import jax
import jax.numpy as jnp
from jax import lax
from jax.experimental import pallas as pl
from jax.experimental.pallas import tpu as pltpu

F32 = jnp.float32
BF16 = jnp.bfloat16

EPS = 1e-6
GRID_W = 64
HEAD_DIM = 128
ATTN_HEADS = 8
KV_HEADS = 2
Q_GROUP = ATTN_HEADS // KV_HEADS
ATTN_WIDTH = ATTN_HEADS * HEAD_DIM
KV_WIDTH = KV_HEADS * HEAD_DIM
ROPE_THETA = 10000.0
LRU_WIDTH = 1024
LRU_BLOCKS = 8
LRU_BLOCK = 128
LRU_C = 8.0
CONV_W = 4
CONV_LEFT = 2
EVEN_IN = ATTN_WIDTH + 2 * KV_WIDTH + ATTN_WIDTH + 2 * LRU_WIDTH
MLSTM_HEADS = 8
MLSTM_V_DIM = 256
MLSTM_QK_DIM = 128
MLSTM_WIDTH = MLSTM_HEADS * MLSTM_V_DIM
MLSTM_QK_WIDTH = MLSTM_HEADS * MLSTM_QK_DIM
MLSTM_CHUNK = 128
N_GATE_SETS = 4

LOG2_E = 1.4426950408889634

LANES = 128
SUBLANES = 8
VMEM_LIMIT = 56 * 1024 * 1024
SCAN_SEGMENTS = 4 * SUBLANES


def _sigmoid(x):
    return 1.0 / (1.0 + jnp.exp(-x))


def _log_sigmoid(x):
    return jnp.minimum(x, 0.0) - jnp.log1p(jnp.exp(-jnp.abs(x)))


def _params(*sem):
    return pltpu.CompilerParams(dimension_semantics=sem, vmem_limit_bytes=VMEM_LIMIT)


def _rmsnorm_rows_to(x_ref, g_ref, dst_ref, rows, chunk):
    g = g_ref[...]

    def body(c, carry):
        r0 = pl.multiple_of(c * chunk, chunk)
        x = x_ref[pl.ds(r0, chunk), :]
        ms = jnp.mean(x * x, axis=-1, keepdims=True)
        dst_ref[pl.ds(r0, chunk), :] = (x * lax.rsqrt(ms + EPS) * g).astype(dst_ref.dtype)
        return carry

    lax.fori_loop(0, rows // chunk, body, 0)


def _rmsnorm_kernel(x_ref, g_ref, o_ref, *, tm):
    _rmsnorm_rows_to(x_ref, g_ref, o_ref, tm, 64)


def _rmsnorm(x, gain, tm):
    t, d = x.shape
    return pl.pallas_call(
        functools.partial(_rmsnorm_kernel, tm=tm),
        out_shape=jax.ShapeDtypeStruct((t, d), BF16),
        grid=(t // tm,),
        in_specs=[pl.BlockSpec((tm, d), lambda i: (i, 0)), pl.BlockSpec((1, d), lambda i: (0, 0))],
        out_specs=pl.BlockSpec((tm, d), lambda i: (i, 0)),
        compiler_params=_params("parallel"),
        name="rmsnorm0",
    )(x, gain)


def _in_proj0_kernel(a_ref, w_ref, o_ref, wb_ref, *, half_from, k_chunk):
    @pl.when(pl.program_id(1) == 0)
    def _():
        tn = w_ref.shape[1]
        col = pl.program_id(0) * tn + lax.broadcasted_iota(jnp.int32, (1, tn), 1)
        scale = jnp.where(col >= half_from, 0.5, 1.0).astype(F32)

        def cast(c, carry):
            rows = pl.ds(pl.multiple_of(c * k_chunk, k_chunk), k_chunk)
            wb_ref[rows, :] = (w_ref[rows, :] * scale).astype(wb_ref.dtype)
            return carry

        lax.fori_loop(0, w_ref.shape[0] // k_chunk, cast, 0)

    o_ref[...] = jnp.dot(a_ref[...], wb_ref[...], preferred_element_type=F32).astype(o_ref.dtype)


def _in_proj0(a, w, out_dtype, tm, tn, half_from_col):
    t, k = a.shape
    n = w.shape[1]
    return pl.pallas_call(
        functools.partial(_in_proj0_kernel, half_from=half_from_col, k_chunk=128),
        out_shape=jax.ShapeDtypeStruct((t, n), out_dtype),
        grid=(n // tn, t // tm),
        in_specs=[
            pl.BlockSpec((tm, k), lambda j, i: (i, 0)),
            pl.BlockSpec((k, tn), lambda j, i: (0, j)),
        ],
        out_specs=pl.BlockSpec((tm, tn), lambda j, i: (i, j)),
        scratch_shapes=[pltpu.VMEM((k, tn), BF16)],
        compiler_params=_params("parallel", "arbitrary"),
        name="in_proj0",
    )(a, w)


def _matmul_kernel(a_ref, wt_ref, o_ref, wb_ref, *, w_scale, n_chunk):
    @pl.when(pl.program_id(1) == 0)
    def _():
        def cast(c, carry):
            rows = pl.ds(pl.multiple_of(c * n_chunk, n_chunk), n_chunk)
            wb_ref[rows, :] = (wt_ref[rows, :] * w_scale).astype(wb_ref.dtype)
            return carry

        lax.fori_loop(0, wt_ref.shape[0] // n_chunk, cast, 0)

    o_ref[...] = lax.dot_general(a_ref[...], wb_ref[...], (((1,), (1,)), ((), ())),
                                 preferred_element_type=F32).astype(o_ref.dtype)


def _matmul(a, wt, n, row_block0, out_dtype, tm, tn, name, w_scale=1.0):
    t, k = a.shape
    return pl.pallas_call(
        functools.partial(_matmul_kernel, w_scale=w_scale, n_chunk=64),
        out_shape=jax.ShapeDtypeStruct((t, n), out_dtype),
        grid=(n // tn, t // tm),
        in_specs=[
            pl.BlockSpec((tm, k), lambda j, i: (i, 0)),
            pl.BlockSpec((tn, k), lambda j, i: (j + row_block0, 0)),
        ],
        out_specs=pl.BlockSpec((tm, tn), lambda j, i: (i, j)),
        scratch_shapes=[pltpu.VMEM((tn, k), BF16)],
        compiler_params=_params("parallel", "arbitrary"),
        name=name,
    )(a, wt)


def _out_proj_kernel(*refs, n_lhs, with_x_out, with_gates, tm, chunk):
    lhs = refs[:n_lhs]
    ws = refs[n_lhs:2 * n_lhs]
    x_ref, g_ref = refs[2 * n_lhs:2 * n_lhs + 2]
    pos = 2 * n_lhs + 2
    wg_ref = refs[pos] if with_gates else None
    pos += int(with_gates)
    xo_ref = refs[pos] if with_x_out else None
    pos += int(with_x_out)
    no_ref = refs[pos]
    go_ref = refs[pos + 1] if with_gates else None
    y_refs = refs[-2:]
    step = pl.program_id(0)

    @pl.when(step == 0)
    def _():
        y_refs[1][...] = jnp.zeros_like(y_refs[1])

    def work(cur_ref, prev_ref):
        acc = jnp.dot(lhs[0][...], ws[0][...], preferred_element_type=F32)
        for a_ref, w_ref in zip(lhs[1:], ws[1:]):
            acc = acc + jnp.dot(a_ref[...], w_ref[...], preferred_element_type=F32)
        cur_ref[...] = acc
        g = g_ref[...]
        for c in range(tm // chunk):
            rows = slice(c * chunk, (c + 1) * chunk)
            xn = x_ref[rows, :] + prev_ref[rows, :]
            if with_x_out:
                xo_ref[rows, :] = xn
            ms = jnp.mean(xn * xn, axis=-1, keepdims=True)
            no_ref[rows, :] = (xn * lax.rsqrt(ms + EPS) * g).astype(no_ref.dtype)
        if with_gates:
            go_ref[...] = lax.dot_general(wg_ref[...], no_ref[...], (((1,), (1,)), ((), ())),
                                          preferred_element_type=F32)

    @pl.when(step % 2 == 0)
    def _():
        work(y_refs[0], y_refs[1])

    @pl.when(step % 2 == 1)
    def _():
        work(y_refs[1], y_refs[0])


def _out_proj(lhs, w, x, gain, tm, name, x_out, wg_t=None):
    t, d = x.shape
    n_tiles = t // tm
    ks = [a.shape[1] for a in lhs]
    assert all(k == ks[0] for k in ks)

    def cur(i):
        return jnp.minimum(i, n_tiles - 1)

    def prev(i):
        return jnp.maximum(i - 1, 0)

    in_specs = [pl.BlockSpec((tm, k), lambda i: (cur(i), 0)) for k in ks]
    in_specs += [pl.BlockSpec((k, d), lambda i, j=j: (j, 0)) for j, k in enumerate(ks)]
    in_specs += [pl.BlockSpec((tm, d), lambda i: (prev(i), 0)), pl.BlockSpec((1, d), lambda i: (0, 0))]
    args = list(lhs) + [w] * len(lhs) + [x, gain]
    out_shape, out_specs = [], []
    if wg_t is not None:
        in_specs.append(pl.BlockSpec(wg_t.shape, lambda i: (0, 0)))
        args.append(wg_t)
    if x_out:
        out_shape.append(jax.ShapeDtypeStruct((t, d), F32))
        out_specs.append(pl.BlockSpec((tm, d), lambda i: (prev(i), 0)))
    out_shape.append(jax.ShapeDtypeStruct((t, d), BF16 if x_out else F32))
    out_specs.append(pl.BlockSpec((tm, d), lambda i: (prev(i), 0)))
    if wg_t is not None:
        out_shape.append(jax.ShapeDtypeStruct((wg_t.shape[0], t), F32))
        out_specs.append(pl.BlockSpec((wg_t.shape[0], tm), lambda i: (0, prev(i))))
    return pl.pallas_call(
        functools.partial(_out_proj_kernel, n_lhs=len(lhs), with_x_out=x_out, with_gates=wg_t is not None,
                          tm=tm, chunk=64),
        out_shape=tuple(out_shape),
        grid=(n_tiles + 1,),
        in_specs=in_specs,
        out_specs=tuple(out_specs),
        scratch_shapes=[pltpu.VMEM((tm, d), F32), pltpu.VMEM((tm, d), F32)],
        compiler_params=_params("arbitrary"),
        name=name,
    )(*args)


def _rope_tables(seq_len):
    half = HEAD_DIM // 2
    t = jnp.arange(seq_len)
    row = (t // GRID_W).astype(F32)
    col = (t % GRID_W).astype(F32)
    inv = ROPE_THETA ** (-jnp.arange(0, half, 2, dtype=F32) / half)
    ar = row[:, None] * inv
    ac = col[:, None] * inv
    cos_t = jnp.concatenate([jnp.cos(ar), jnp.cos(ar), jnp.cos(ac), jnp.cos(ac)], axis=-1)
    sin_t = jnp.concatenate([-jnp.sin(ar), jnp.sin(ar), -jnp.sin(ac), jnp.sin(ac)], axis=-1)
    return cos_t, sin_t


def _norm_rope(x, gain, cos_v, sin_v):
    quarter = HEAD_DIM // 4
    ms = jnp.mean(x * x, axis=-1, keepdims=True)
    xn = x * lax.rsqrt(ms + EPS) * gain
    lane = lax.broadcasted_iota(jnp.int32, xn.shape, 1)
    first = (lane % (2 * quarter)) < quarter
    partner = jnp.where(first, pltpu.roll(xn, HEAD_DIM - quarter, 1), pltpu.roll(xn, quarter, 1))
    return xn * cos_v + partner * sin_v


def _attention_kernel(q_ref, k_ref, v_ref, g_ref, qg_ref, kg_ref, cos_ref, sin_ref, o_ref,
                      kr_ref, vt_ref, qr_ref, s_ref, of_ref, *, seq, tq, kc):
    n_blk = seq // tq
    n_kc = seq // kc
    n_total = Q_GROUP * n_blk
    kg = kg_ref[...]
    qg = qg_ref[...]
    q_scale = HEAD_DIM ** -0.5 * LOG2_E
    eye = jnp.where(lax.broadcasted_iota(jnp.int32, (HEAD_DIM, HEAD_DIM), 0)
                    == lax.broadcasted_iota(jnp.int32, (HEAD_DIM, HEAD_DIM), 1), 1.0, 0.0).astype(BF16)

    def prep(c, carry):
        rows = pl.ds(pl.multiple_of(c * tq, tq), tq)
        cos_v, sin_v = cos_ref[rows, :], sin_ref[rows, :]
        kr_ref[rows, :] = _norm_rope(k_ref[0, rows, :], kg, cos_v, sin_v).astype(BF16)
        vt_ref[:, rows] = lax.dot_general(eye, v_ref[0, rows, :].astype(BF16), (((1,), (1,)), ((), ())),
                                          preferred_element_type=F32).astype(BF16)
        for g in range(Q_GROUP):
            q = _norm_rope(q_ref[0, rows, g * HEAD_DIM:(g + 1) * HEAD_DIM], qg, cos_v, sin_v) * q_scale
            qr_ref[pl.ds(pl.multiple_of(g * seq + c * tq, tq), tq), :] = q.astype(BF16)
        return carry

    lax.fori_loop(0, n_blk, prep, 0, unroll=2 if n_blk % 2 == 0 else 1)

    def blk_rows(i):
        return pl.ds(pl.multiple_of(i * tq, tq), tq)

    def scores_chunk(i, j):
        return lax.dot_general(kr_ref[j * kc:(j + 1) * kc, :], qr_ref[blk_rows(i), :],
                               (((1,), (1,)), ((), ())), preferred_element_type=F32)

    def col_max8(x):
        return jnp.max(x.reshape(x.shape[0] // SUBLANES, SUBLANES, x.shape[1]), axis=0)

    def col_sum8(x):
        return jnp.sum(x.reshape(x.shape[0] // SUBLANES, SUBLANES, x.shape[1]), axis=0)

    neg_inf8 = jnp.full((SUBLANES, tq), -jnp.inf, F32)
    m8 = neg_inf8
    for j in range(n_kc):
        s_t = scores_chunk(0, j)
        s_ref[0, j] = s_t
        m8 = jnp.maximum(m8, col_max8(s_t))

    def block(i, m8_cur, slot):
        nxt = jnp.minimum(i + 1, n_total - 1)
        m_cur = jnp.max(m8_cur, axis=0, keepdims=True)
        m8_next = neg_inf8
        l8 = jnp.zeros((SUBLANES, tq), F32)
        acc = jnp.zeros((HEAD_DIM, tq), F32)
        for j in range(n_kc):
            s_next = scores_chunk(nxt, j)
            s_ref[1 - slot, j] = s_next
            m8_next = jnp.maximum(m8_next, col_max8(s_next))
            p_t = jnp.exp2(s_ref[slot, j] - m_cur)
            l8 = l8 + col_sum8(p_t)
            acc = acc + jnp.dot(vt_ref[:, j * kc:(j + 1) * kc], p_t.astype(BF16),
                                preferred_element_type=F32)
        l = jnp.sum(l8, axis=0, keepdims=True)
        of_ref[blk_rows(i), :] = (acc * (1.0 / l)).T
        return m8_next

    def block_pair(ii, m8_cur):
        return block(2 * ii + 1, block(2 * ii, m8_cur, 0), 1)

    lax.fori_loop(0, n_total // 2, block_pair, m8)

    def finish(c, carry):
        rows = pl.ds(pl.multiple_of(c * tq, tq), tq)
        for g in range(Q_GROUP):
            cols = slice(g * HEAD_DIM, (g + 1) * HEAD_DIM)
            gate = g_ref[0, rows, cols]
            o = of_ref[pl.ds(pl.multiple_of(g * seq + c * tq, tq), tq), :]
            o_ref[0, rows, cols] = (o * (gate * _sigmoid(gate))).astype(o_ref.dtype)
        return carry

    lax.fori_loop(0, n_blk, finish, 0)


def _attention(proj, q_gain, k_gain, cos_t, sin_t, tq):
    b, s, _ = proj.shape
    gw = Q_GROUP * HEAD_DIM
    k_blk0 = ATTN_WIDTH // HEAD_DIM
    v_blk0 = (ATTN_WIDTH + KV_WIDTH) // HEAD_DIM
    g_blk0 = (ATTN_WIDTH + 2 * KV_WIDTH) // gw
    kc = _row_tile(s, 512)
    return pl.pallas_call(
        functools.partial(_attention_kernel, seq=s, tq=tq, kc=kc),
        out_shape=jax.ShapeDtypeStruct((b, s, ATTN_WIDTH), BF16),
        grid=(b, KV_HEADS),
        in_specs=[
            pl.BlockSpec((1, s, gw), lambda i, j: (i, 0, j)),
            pl.BlockSpec((1, s, HEAD_DIM), lambda i, j: (i, 0, k_blk0 + j)),
            pl.BlockSpec((1, s, HEAD_DIM), lambda i, j: (i, 0, v_blk0 + j)),
            pl.BlockSpec((1, s, gw), lambda i, j: (i, 0, g_blk0 + j)),
            pl.BlockSpec((1, HEAD_DIM), lambda i, j: (0, 0)),
            pl.BlockSpec((1, HEAD_DIM), lambda i, j: (0, 0)),
            pl.BlockSpec((s, HEAD_DIM), lambda i, j: (0, 0)),
            pl.BlockSpec((s, HEAD_DIM), lambda i, j: (0, 0)),
        ],
        out_specs=pl.BlockSpec((1, s, gw), lambda i, j: (i, 0, j)),
        scratch_shapes=[
            pltpu.VMEM((s, HEAD_DIM), BF16),
            pltpu.VMEM((HEAD_DIM, s), BF16),
            pltpu.VMEM((Q_GROUP * s, HEAD_DIM), BF16),
            pltpu.VMEM((2, s // kc, kc, tq), F32),
            pltpu.VMEM((Q_GROUP * s, HEAD_DIM), F32),
        ],
        compiler_params=_params("parallel", "parallel"),
        name="attention",
    )(proj, proj, proj, proj, q_gain, k_gain, cos_t, sin_t)


def _lru_kernel(x_ref, g_ref, cw_ref, cb_ref, w_ref, lam_ref, o_ref,
                xpad_ref, af_ref, uf_ref, ab_ref, ub_ref, hf_ref, pf_ref, hb_ref, pb_ref,
                *, seq, n_row_chunks):
    seg = seq // SCAN_SEGMENTS + SUBLANES // 2
    rows_padded = SCAN_SEGMENTS * seg
    chunk = seq // n_row_chunks
    pad = SUBLANES
    a_refs = (af_ref, ab_ref)
    u_refs = (uf_ref, ub_ref)

    zeros = jnp.zeros((pad, LRU_BLOCK), F32)
    xpad_ref[0:pad, :] = zeros
    xpad_ref[pad + seq:pad + seq + pad, :] = zeros
    xpad_ref[pad:pad + seq, :] = x_ref[0]
    for ref in a_refs + u_refs:
        ref[seq:rows_padded, :] = jnp.zeros((rows_padded - seq, LRU_BLOCK), F32)

    cw = cw_ref[...]
    cb = cb_ref[...]
    w = w_ref[0]
    decay = (-0.5 * LRU_C) * _log_sigmoid(lam_ref[...])
    bias_taps = jnp.where(lax.broadcasted_iota(jnp.int32, (chunk, LRU_BLOCK), 1) < 2, 1.0, 0.0).astype(BF16)

    for k in range(n_row_chunks):
        half_xc = cb
        for j in range(CONV_W):
            lo = pad + k * chunk + j - CONV_LEFT
            half_xc = half_xc + cw[j:j + 1, :] * xpad_ref[lo:lo + chunk, :]
        lhs = jnp.concatenate([half_xc.astype(BF16), bias_taps], axis=1)
        th = jnp.tanh(jnp.dot(lhs, w, preferred_element_type=F32))
        for d in range(2):
            c0 = 2 * d * LRU_BLOCK
            th_r = th[:, c0:c0 + LRU_BLOCK]
            th_i = th[:, c0 + LRU_BLOCK:c0 + 2 * LRU_BLOCK]
            dec = decay[d:d + 1, :]
            neg_log_a = dec * th_r + dec
            ix = half_xc * th_i + half_xc
            a = jnp.exp2(neg_log_a * -LOG2_E)
            one_minus_a2 = jnp.tanh(neg_log_a) * (a * a + 1.0)
            a_refs[d][k * chunk:(k + 1) * chunk, :] = a
            u_refs[d][k * chunk:(k + 1) * chunk, :] = jnp.exp2(0.5 * jnp.log2(one_minus_a2)) * ix

    n_groups = SCAN_SEGMENTS // SUBLANES

    def rows_at(t, grp):
        return pl.ds(t + grp * SUBLANES * seg, SUBLANES, stride=seg)

    def pass1(t, carry):
        hf, pf, hb, pb = (list(c) for c in carry)
        tb = seg - 1 - t
        for grp in range(n_groups):
            a = af_ref[rows_at(t, grp), :]
            hf[grp] = a * hf[grp] + uf_ref[rows_at(t, grp), :]
            pf[grp] = pf[grp] * a
            hf_ref[rows_at(t, grp), :] = hf[grp]
            pf_ref[rows_at(t, grp), :] = pf[grp]
            a = ab_ref[rows_at(tb, grp), :]
            hb[grp] = a * hb[grp] + ub_ref[rows_at(tb, grp), :]
            pb[grp] = pb[grp] * a
            hb_ref[rows_at(tb, grp), :] = hb[grp]
            pb_ref[rows_at(tb, grp), :] = pb[grp]
        return tuple(hf), tuple(pf), tuple(hb), tuple(pb)

    unroll = 2 if seg % 2 == 0 else 1
    zero = (jnp.zeros((SUBLANES, LRU_BLOCK), F32),) * n_groups
    one = (jnp.ones((SUBLANES, LRU_BLOCK), F32),) * n_groups
    hf, pf, hb, pb = lax.fori_loop(0, seg, pass1, (zero, one, zero, one), unroll=unroll)

    row = lax.broadcasted_iota(jnp.int32, (SUBLANES, LRU_BLOCK), 0)

    def entering_states(h_end, p_end, order):
        tiles = [jnp.zeros((SUBLANES, LRU_BLOCK), F32)] * n_groups
        c = jnp.zeros((1, LRU_BLOCK), F32)
        for k in order:
            grp, sub = divmod(k, SUBLANES)
            tiles[grp] = jnp.where(row == sub, c, tiles[grp])
            c = h_end[grp][sub:sub + 1, :] + p_end[grp][sub:sub + 1, :] * c
        return tiles

    cf = entering_states(hf, pf, range(SCAN_SEGMENTS))
    cbk = entering_states(hb, pb, reversed(range(SCAN_SEGMENTS)))

    def pass2(t, carry):
        for grp in range(n_groups):
            y = (hf_ref[rows_at(t, grp), :] + pf_ref[rows_at(t, grp), :] * cf[grp]
                 + hb_ref[rows_at(t, grp), :] + pb_ref[rows_at(t, grp), :] * cbk[grp])
            uf_ref[rows_at(t, grp), :] = y
        return carry

    lax.fori_loop(0, seg, pass2, 0, unroll=unroll)

    for k in range(n_row_chunks):
        rows = slice(k * chunk, (k + 1) * chunk)
        half_gate = g_ref[0, rows, :]
        silu = half_gate * jnp.tanh(half_gate) + half_gate
        o_ref[0, rows, :] = (uf_ref[rows, :] * silu).astype(o_ref.dtype)


def _lru_gate_weights(wa, ba, wx, bx):
    w_cat = jnp.concatenate([wa[0], wx[0], wa[1], wx[1]], axis=-1).astype(BF16)
    half_b = 0.5 * jnp.concatenate([ba[0].reshape(LRU_BLOCKS, 1, LRU_BLOCK), bx[0].reshape(LRU_BLOCKS, 1, LRU_BLOCK),
                                    ba[1].reshape(LRU_BLOCKS, 1, LRU_BLOCK), bx[1].reshape(LRU_BLOCKS, 1, LRU_BLOCK)],
                                   axis=-1)
    hi = half_b.astype(BF16)
    lo = (half_b - hi.astype(F32)).astype(BF16)
    fill = jnp.zeros((LRU_BLOCKS, LRU_BLOCK - 2, 4 * LRU_BLOCK), BF16)
    return jnp.concatenate([w_cat, hi, lo, fill], axis=1)


def _lru(proj, half_conv_w, half_conv_b, w_gates, lam):
    b, s, _ = proj.shape
    x_blk0 = (2 * ATTN_WIDTH + 2 * KV_WIDTH) // LRU_BLOCK
    g_blk0 = x_blk0 + LRU_BLOCKS
    seg_rows = s + SCAN_SEGMENTS * (SUBLANES // 2)
    n_row_chunks = 8 if s % (8 * SUBLANES) == 0 else 1
    return pl.pallas_call(
        functools.partial(_lru_kernel, seq=s, n_row_chunks=n_row_chunks),
        out_shape=jax.ShapeDtypeStruct((b, s, LRU_WIDTH), BF16),
        grid=(b, LRU_BLOCKS),
        in_specs=[
            pl.BlockSpec((1, s, LRU_BLOCK), lambda i, n: (i, 0, x_blk0 + n)),
            pl.BlockSpec((1, s, LRU_BLOCK), lambda i, n: (i, 0, g_blk0 + n)),
            pl.BlockSpec((CONV_W, LRU_BLOCK), lambda i, n: (0, n)),
            pl.BlockSpec((1, LRU_BLOCK), lambda i, n: (0, n)),
            pl.BlockSpec((1, 2 * LRU_BLOCK, 4 * LRU_BLOCK), lambda i, n: (n, 0, 0)),
            pl.BlockSpec((2, LRU_BLOCK), lambda i, n: (0, n)),
        ],
        out_specs=pl.BlockSpec((1, s, LRU_BLOCK), lambda i, n: (i, 0, n)),
        scratch_shapes=[pltpu.VMEM((s + 2 * SUBLANES, LRU_BLOCK), F32)]
        + [pltpu.VMEM((seg_rows, LRU_BLOCK), F32) for _ in range(8)],
        compiler_params=_params("parallel", "parallel"),
        name="rg_lru",
    )(proj, proj, half_conv_w, half_conv_b, w_gates, lam)


def _hi_lo(x):
    hi = x.astype(BF16)
    lo = (x - hi.astype(F32)).astype(BF16)
    return hi, lo


def _dot_nt(a, b):
    return lax.dot_general(a, b, (((1,), (1,)), ((), ())), preferred_element_type=F32)


def _lane_tile(x, width):
    return jnp.concatenate([x] * (width // x.shape[1]), axis=1) if width != x.shape[1] else x


_G_C, _G_WS, _G_BTOT, _G_MW = range(4)


def _mlstm_gate_table(gate_ref, bias_ref, hh, head, tab_ref, bt_ref):
    chunk = MLSTM_CHUNK
    ri = lax.broadcasted_iota(jnp.int32, (chunk, chunk), 0)
    si = lax.broadcasted_iota(jnp.int32, (chunk, chunk), 1)
    for d in range(2):
        ig = gate_ref[d, hh, 0] + bias_ref[d, head]
        lf = _log_sigmoid(gate_ref[2 + d, hh, 0] + bias_ref[2 + d, head])
        upper = jnp.where((ri >= si) if d else (ri <= si), 1.0, 0.0).astype(BF16)
        hi, lo = _hi_lo(lf)
        b = (jnp.dot(hi, upper, preferred_element_type=F32)
             + jnp.dot(lo, upper, preferred_element_type=F32))
        b_tot = jnp.broadcast_to(b[:, 0:1] if d else b[:, chunk - 1:chunk], b.shape)
        w = b_tot - b + ig
        mw = jnp.broadcast_to(jnp.max(w, axis=-1, keepdims=True), b.shape)
        tab = tab_ref.at[hh * 2 + d]
        tab[_G_C] = ig - b
        tab[_G_WS] = jnp.exp(w - mw)
        tab[_G_BTOT] = b_tot
        tab[_G_MW] = mw
        b_square = jnp.concatenate([b, jnp.zeros((chunk - b.shape[0], chunk), F32)], axis=0)
        bt_ref[hh * 2 + d] = b_square.T


def _chunk_mask(d, chunk):
    li = lax.broadcasted_iota(jnp.int32, (chunk, chunk), 0)
    si = lax.broadcasted_iota(jnp.int32, (chunk, chunk), 1)
    return (si >= li) if d else (si <= li)


def _mlstm_intra_weights(d, ci, qk, k, tab, bt):
    chunk = qk.shape[0]
    mask = _chunk_mask(d, chunk)
    row = pl.ds(ci, 1)
    c_masked = jnp.where(mask, tab[_G_C, row, :], -jnp.inf)
    cm = jnp.broadcast_to(jnp.max(c_masked, axis=-1, keepdims=True), c_masked.shape)
    p0 = (qk * jnp.exp(c_masked - cm)).astype(BF16)
    ktw = (k.astype(F32).T * tab[_G_WS, row, :]).astype(BF16)
    lane = lax.broadcasted_iota(jnp.int32, (chunk, chunk), 1)
    b_l = jnp.broadcast_to(jnp.sum(jnp.where(lane == ci, bt[...], 0.0), axis=-1, keepdims=True),
                           (chunk, chunk))
    return p0, ktw, cm, b_l


def _mlstm_state_matmuls(q, v, p0, ktw, c_state, n_state):
    chunk = q.shape[0]
    ones = jnp.ones((chunk, LANES), BF16)
    pk = jnp.concatenate([p0, ktw], axis=0)
    nv = jnp.dot(pk, v, preferred_element_type=F32)
    rs = jnp.dot(pk, ones, preferred_element_type=F32)
    qc = jnp.dot(q, c_state.astype(BF16), preferred_element_type=F32)
    qn = jnp.dot(q, n_state.astype(BF16), preferred_element_type=F32)
    return nv[:chunk], rs[:chunk], nv[chunk:], rs[chunk:], qc, qn


def _mlstm_combine(b_l, cm, b_tot, mw, prods, c_state, n_state, m):
    n0, r0, kv0, kn0, qc, qn = prods
    dv = n0.shape[1]
    scale = MLSTM_QK_DIM ** -0.5
    mx = jnp.maximum(m, cm)
    e1 = jnp.exp(cm - mx) * scale
    e2 = jnp.exp(m - mx) * scale
    den = e1 * r0 + e2 * qn
    inv = 1.0 / jnp.maximum(jnp.abs(den), jnp.exp(-(b_l + mx)))
    hout = _lane_tile(e1 * inv, dv) * n0 + _lane_tile(e2 * inv, dv) * qc

    m_new = jnp.maximum(b_tot + m, mw)
    d1 = jnp.exp(b_tot + m - m_new)
    d2 = jnp.exp(mw - m_new)
    c_new = _lane_tile(d1, dv) * c_state + _lane_tile(d2, dv) * kv0
    n_new = d1 * n_state + d2 * kn0
    return hout, c_new, n_new, m_new


MLSTM_HEADS_PER_STEP = 2


def _mlstm_kernel(bias_ref, q_ref, k_ref, v_ref, o_ref, z_ref, gate_ref, ng_ref, out_ref,
                  h_ref, c_ref, n_ref, tab_ref, bt_ref, pw_ref, pm_ref, *, seq):
    chunk = MLSTM_CHUNK
    n_chunks = seq // chunk
    dk, dv = MLSTM_QK_DIM, MLSTM_V_DIM
    streams = [(hh, d) for hh in range(MLSTM_HEADS_PER_STEP) for d in range(2)]
    for hh in range(MLSTM_HEADS_PER_STEP):
        _mlstm_gate_table(gate_ref, bias_ref, hh, pl.program_id(1) * MLSTM_HEADS_PER_STEP + hh,
                          tab_ref, bt_ref)
    c_ref[...] = jnp.zeros_like(c_ref)
    n_ref[...] = jnp.zeros_like(n_ref)

    def chunk_of(d, c):
        return (n_chunks - 1 - c) if d else c

    def rows_of(ci):
        return pl.ds(pl.multiple_of(ci * chunk, chunk), chunk)

    def qk_cols(hh):
        return slice(hh * dk, (hh + 1) * dk)

    def v_cols(hh):
        return slice(hh * dv, (hh + 1) * dv)

    def intra_matmuls(s, c):
        hh, d = streams[s]
        rows = rows_of(chunk_of(d, c))
        return _dot_nt(q_ref[0, rows, qk_cols(hh)], k_ref[0, rows, qk_cols(hh)])

    def park_weights(s, c, slot, qk):
        hh, d = streams[s]
        ci = chunk_of(d, c)
        p0, ktw, cm, b_l = _mlstm_intra_weights(d, ci, qk, k_ref[0, rows_of(ci), qk_cols(hh)],
                                                tab_ref.at[s], bt_ref.at[s])
        pw_ref[slot, s, 0] = p0
        pw_ref[slot, s, 1] = ktw
        pm_ref[slot, s, 0] = b_l
        pm_ref[slot, s, 1] = cm

    for s in range(len(streams)):
        park_weights(s, 0, 0, intra_matmuls(s, 0))

    def step(c, carry):
        ms = list(carry)
        slot = c % 2
        c_next = jnp.minimum(c + 1, n_chunks - 1)
        prods, parts = [], []
        for s, (hh, d) in enumerate(streams):
            rows = rows_of(chunk_of(d, c))
            prods.append(_mlstm_state_matmuls(q_ref[0, rows, qk_cols(hh)], v_ref[0, rows, v_cols(hh)],
                                              pw_ref[slot, s, 0], pw_ref[slot, s, 1], c_ref[s], n_ref[s]))
            parts.append(intra_matmuls(s, c_next))
        for s, (hh, d) in enumerate(streams):
            ci = chunk_of(d, c)
            hout, c_new, n_new, ms[s] = _mlstm_combine(
                pm_ref[slot, s, 0], pm_ref[slot, s, 1],
                tab_ref[s, _G_BTOT, pl.ds(ci, 1), :], tab_ref[s, _G_MW, pl.ds(ci, 1), :],
                prods[s], c_ref[s], n_ref[s], ms[s])
            h_ref[s, rows_of(ci), :] = hout
            c_ref[s] = c_new
            n_ref[s] = n_new
            park_weights(s, c_next, 1 - slot, parts[s])
        return tuple(ms)

    def finish(ci):
        rows = rows_of(ci)
        for hh in range(MLSTM_HEADS_PER_STEP):
            sig_o = 0.5 * jnp.tanh(o_ref[0, rows, v_cols(hh)]) + 0.5
            hs = sig_o * (h_ref[2 * hh, rows, :] + h_ref[2 * hh + 1, rows, :])
            ms = jnp.mean(hs * hs, axis=-1, keepdims=True)
            hs = hs * lax.rsqrt(ms + EPS) * ng_ref[:, v_cols(hh)]
            half_z = z_ref[0, rows, v_cols(hh)]
            out_ref[0, rows, v_cols(hh)] = (hs * (half_z * jnp.tanh(half_z) + half_z)).astype(out_ref.dtype)

    def step_and_finish(c, carry):
        carry = step(c, carry)
        finish(c)
        finish(n_chunks - 1 - c)
        return carry

    half = n_chunks // 2
    m0 = (jnp.zeros((1, LANES), F32),) * len(streams)
    ms_half = lax.fori_loop(0, half, step, m0)
    lax.fori_loop(half, n_chunks, step_and_finish, ms_half)


def _mlstm(qkv, oz, gates_t, gate_bias, norm_gain):
    b, s, _ = qkv.shape
    hps = MLSTM_HEADS_PER_STEP
    dk, dv = hps * MLSTM_QK_DIM, hps * MLSTM_V_DIM
    n_chunks = s // MLSTM_CHUNK
    n_streams = 2 * hps
    k_blk0 = MLSTM_QK_WIDTH // dk
    v_blk0 = 2 * MLSTM_QK_WIDTH // dv
    z_blk0 = MLSTM_WIDTH // dv
    return pl.pallas_call(
        functools.partial(_mlstm_kernel, seq=s),
        out_shape=jax.ShapeDtypeStruct((b, s, MLSTM_WIDTH), BF16),
        grid=(b, MLSTM_HEADS // hps),
        in_specs=[
            pl.BlockSpec(memory_space=pltpu.SMEM),
            pl.BlockSpec((1, s, dk), lambda i, h: (i, 0, h)),
            pl.BlockSpec((1, s, dk), lambda i, h: (i, 0, k_blk0 + h)),
            pl.BlockSpec((1, s, dv), lambda i, h: (i, 0, v_blk0 + h)),
            pl.BlockSpec((1, s, dv), lambda i, h: (i, 0, h)),
            pl.BlockSpec((1, s, dv), lambda i, h: (i, 0, z_blk0 + h)),
            pl.BlockSpec((N_GATE_SETS, hps, 1, n_chunks, MLSTM_CHUNK), lambda i, h: (0, h, i, 0, 0)),
            pl.BlockSpec((1, dv), lambda i, h: (0, h)),
        ],
        out_specs=pl.BlockSpec((1, s, dv), lambda i, h: (i, 0, h)),
        scratch_shapes=[
            pltpu.VMEM((n_streams, s, MLSTM_V_DIM), F32),
            pltpu.VMEM((n_streams, MLSTM_QK_DIM, MLSTM_V_DIM), F32),
            pltpu.VMEM((n_streams, MLSTM_QK_DIM, LANES), F32),
            pltpu.VMEM((n_streams, 4, n_chunks, MLSTM_CHUNK), F32),
            pltpu.VMEM((n_streams, MLSTM_CHUNK, MLSTM_CHUNK), F32),
            pltpu.VMEM((2, n_streams, 2, MLSTM_CHUNK, MLSTM_CHUNK), BF16),
            pltpu.VMEM((2, n_streams, 2, MLSTM_CHUNK, MLSTM_CHUNK), F32),
        ],
        compiler_params=_params("parallel", "parallel"),
        name="mlstm",
    )(gate_bias, qkv, qkv, qkv, oz, oz, gates_t, norm_gain)


def _row_tile(t, want):
    return want if t % want == 0 else t


def kernel(x, norm_gain, final_gain, even_w_in, even_w_out, q_norm_gain, k_norm_gain, conv_w, conv_b,
           lru_wa, lru_ba, lru_wx, lru_bx, lru_lambda, odd_w_in, odd_gate_bias, odd_norm_gain, odd_w_out):
    b, s, d = x.shape
    t = b * s
    assert s % (SCAN_SEGMENTS * SUBLANES) == 0 and s % (2 * MLSTM_CHUNK) == 0 and s % GRID_W == 0
    x2 = x.reshape(t, d)

    hn0 = _rmsnorm(x2, norm_gain[0][None], tm=_row_tile(t, 512))
    proj0 = _in_proj0(hn0, even_w_in[0], F32, tm=_row_tile(t, 1024), tn=768,
                      half_from_col=EVEN_IN - LRU_WIDTH).reshape(b, s, EVEN_IN)
    cos_t, sin_t = _rope_tables(s)
    attn = _attention(proj0, q_norm_gain[0][None], k_norm_gain[0][None], cos_t, sin_t,
                      tq=_row_tile(s, 256))
    lru = _lru(proj0, 0.5 * conv_w[0], 0.5 * conv_b[0][None],
               _lru_gate_weights(lru_wa[0], lru_ba[0], lru_wx[0], lru_bx[0]), lru_lambda[0])
    qkv_w = 2 * MLSTM_QK_WIDTH + MLSTM_WIDTH
    oz_w = 2 * MLSTM_WIDTH
    w1t = odd_w_in[0].T
    x1, hn1, gates_t = _out_proj([attn.reshape(t, ATTN_WIDTH), lru.reshape(t, LRU_WIDTH)],
                                 even_w_out[0].astype(BF16), x2, norm_gain[1][None], _row_tile(t, 512),
                                 "out_proj0", x_out=True, wg_t=w1t[qkv_w + oz_w:].astype(BF16))

    tm1 = _row_tile(t, 1024)
    tn1 = 1024
    qkv = _matmul(hn1, w1t, qkv_w, 0, BF16, tm1, tn1, "in_proj1_qkv").reshape(b, s, qkv_w)
    oz = _matmul(hn1, w1t, oz_w, qkv_w // tn1, F32, tm1, tn1, "in_proj1_oz", w_scale=0.5).reshape(b, s, oz_w)
    gates_t = gates_t.reshape(N_GATE_SETS, MLSTM_HEADS, b, s // MLSTM_CHUNK, MLSTM_CHUNK)
    mix1 = _mlstm(qkv, oz, gates_t, odd_gate_bias[0], odd_norm_gain[0][None])
    (out,) = _out_proj([mix1.reshape(t, MLSTM_WIDTH)], odd_w_out[0].astype(BF16), x1, final_gain[None],
                       _row_tile(t, 512), "out_proj1", x_out=False)
    return out.reshape(b, s, d)
```

```python
import functools

import jax
import jax.numpy as jnp
from jax import lax
from jax.experimental import pallas as pl
from jax.experimental.pallas import tpu as pltpu

F32 = jnp.float32
BF16 = jnp.bfloat16

EPS = 1e-6
GRID_W = 64
HEAD_DIM = 128
ATTN_HEADS = 8
KV_HEADS = 2
Q_GROUP = ATTN_HEADS // KV_HEADS
ATTN_WIDTH = ATTN_HEADS * HEAD_DIM
KV_WIDTH = KV_HEADS * HEAD_DIM
ROPE_THETA = 10000.0
LRU_WIDTH = 1024
LRU_BLOCKS = 8
LRU_BLOCK = 128
LRU_C = 8.0
CONV_W = 4
CONV_LEFT = 2
EVEN_IN = ATTN_WIDTH + 2 * KV_WIDTH + ATTN_WIDTH + 2 * LRU_WIDTH
MLSTM_HEADS = 8
MLSTM_V_DIM = 256
MLSTM_QK_DIM = 128
MLSTM_WIDTH = MLSTM_HEADS * MLSTM_V_DIM
MLSTM_QK_WIDTH = MLSTM_HEADS * MLSTM_QK_DIM
MLSTM_CHUNK = 128
N_GATE_SETS = 4

LOG2_E = 1.4426950408889634

LANES = 128
SUBLANES = 8
VMEM_LIMIT = 56 * 1024 * 1024
SCAN_SEGMENTS = 4 * SUBLANES


def _sigmoid(x):
    return 1.0 / (1.0 + jnp.exp(-x))


def _log_sigmoid(x):
    return jnp.minimum(x, 0.0) - jnp.log1p(jnp.exp(-jnp.abs(x)))


def _params(*sem):
    return pltpu.CompilerParams(dimension_semantics=sem, vmem_limit_bytes=VMEM_LIMIT)


def _rmsnorm_rows_to(x_ref, g_ref, dst_ref, rows, chunk):
    g = g_ref[...]

    def body(c, carry):
        r0 = pl.multiple_of(c * chunk, chunk)
        x = x_ref[pl.ds(r0, chunk), :]
        ms = jnp.mean(x * x, axis=-1, keepdims=True)
        dst_ref[pl.ds(r0, chunk), :] = (x * lax.rsqrt(ms + EPS) * g).astype(dst_ref.dtype)
        return carry

    lax.fori_loop(0, rows // chunk, body, 0)


def _rmsnorm_kernel(x_ref, g_ref, o_ref, *, tm):
    _rmsnorm_rows_to(x_ref, g_ref, o_ref, tm, 64)


def _rmsnorm(x, gain, tm):
    t, d = x.shape
    return pl.pallas_call(
        functools.partial(_rmsnorm_kernel, tm=tm),
        out_shape=jax.ShapeDtypeStruct((t, d), BF16),
        grid=(t // tm,),
        in_specs=[pl.BlockSpec((tm, d), lambda i: (i, 0)), pl.BlockSpec((1, d), lambda i: (0, 0))],
        out_specs=pl.BlockSpec((tm, d), lambda i: (i, 0)),
        compiler_params=_params("parallel"),
        name="rmsnorm0",
    )(x, gain)


def _in_proj0_kernel(a_ref, w_ref, o_ref, wb_ref, *, half_from, k_chunk):
    @pl.when(pl.program_id(1) == 0)
    def _():
        tn = w_ref.shape[1]
        col = pl.program_id(0) * tn + lax.broadcasted_iota(jnp.int32, (1, tn), 1)
        scale = jnp.where(col >= half_from, 0.5, 1.0).astype(F32)

        def cast(c, carry):
            rows = pl.ds(pl.multiple_of(c * k_chunk, k_chunk), k_chunk)
            wb_ref[rows, :] = (w_ref[rows, :] * scale).astype(wb_ref.dtype)
            return carry

        lax.fori_loop(0, w_ref.shape[0] // k_chunk, cast, 0)

    o_ref[...] = jnp.dot(a_ref[...], wb_ref[...], preferred_element_type=F32).astype(o_ref.dtype)


def _in_proj0(a, w, out_dtype, tm, tn, half_from_col):
    t, k = a.shape
    n = w.shape[1]
    return pl.pallas_call(
        functools.partial(_in_proj0_kernel, half_from=half_from_col, k_chunk=128),
        out_shape=jax.ShapeDtypeStruct((t, n), out_dtype),
        grid=(n // tn, t // tm),
        in_specs=[
            pl.BlockSpec((tm, k), lambda j, i: (i, 0)),
            pl.BlockSpec((k, tn), lambda j, i: (0, j)),
        ],
        out_specs=pl.BlockSpec((tm, tn), lambda j, i: (i, j)),
        scratch_shapes=[pltpu.VMEM((k, tn), BF16)],
        compiler_params=_params("parallel", "arbitrary"),
        name="in_proj0",
    )(a, w)


def _matmul_kernel(a_ref, wt_ref, o_ref, wb_ref, *, w_scale, n_chunk):
    @pl.when(pl.program_id(1) == 0)
    def _():
        def cast(c, carry):
            rows = pl.ds(pl.multiple_of(c * n_chunk, n_chunk), n_chunk)
            wb_ref[rows, :] = (wt_ref[rows, :] * w_scale).astype(wb_ref.dtype)
            return carry

        lax.fori_loop(0, wt_ref.shape[0] // n_chunk, cast, 0)

    o_ref[...] = lax.dot_general(a_ref[...], wb_ref[...], (((1,), (1,)), ((), ())),
                                 preferred_element_type=F32).astype(o_ref.dtype)


def _matmul(a, wt, n, row_block0, out_dtype, tm, tn, name, w_scale=1.0):
    t, k = a.shape
    return pl.pallas_call(
        functools.partial(_matmul_kernel, w_scale=w_scale, n_chunk=64),
        out_shape=jax.ShapeDtypeStruct((t, n), out_dtype),
        grid=(n // tn, t // tm),
        in_specs=[
            pl.BlockSpec((tm, k), lambda j, i: (i, 0)),
            pl.BlockSpec((tn, k), lambda j, i: (j + row_block0, 0)),
        ],
        out_specs=pl.BlockSpec((tm, tn), lambda j, i: (i, j)),
        scratch_shapes=[pltpu.VMEM((tn, k), BF16)],
        compiler_params=_params("parallel", "arbitrary"),
        name=name,
    )(a, wt)


def _out_proj_kernel(*refs, n_lhs, with_x_out, with_gates, tm, chunk):
    lhs = refs[:n_lhs]
    ws = refs[n_lhs:2 * n_lhs]
    x_ref, g_ref = refs[2 * n_lhs:2 * n_lhs + 2]
    pos = 2 * n_lhs + 2
    wg_ref = refs[pos] if with_gates else None
    pos += int(with_gates)
    xo_ref = refs[pos] if with_x_out else None
    pos += int(with_x_out)
    no_ref = refs[pos]
    go_ref = refs[pos + 1] if with_gates else None
    y_refs = refs[-2:]
    step = pl.program_id(0)

    @pl.when(step == 0)
    def _():
        y_refs[1][...] = jnp.zeros_like(y_refs[1])

    def work(cur_ref, prev_ref):
        acc = jnp.dot(lhs[0][...], ws[0][...], preferred_element_type=F32)
        for a_ref, w_ref in zip(lhs[1:], ws[1:]):
            acc = acc + jnp.dot(a_ref[...], w_ref[...], preferred_element_type=F32)
        cur_ref[...] = acc
        g = g_ref[...]
        for c in range(tm // chunk):
            rows = slice(c * chunk, (c + 1) * chunk)
            xn = x_ref[rows, :] + prev_ref[rows, :]
            if with_x_out:
                xo_ref[rows, :] = xn
            ms = jnp.mean(xn * xn, axis=-1, keepdims=True)
            no_ref[rows, :] = (xn * lax.rsqrt(ms + EPS) * g).astype(no_ref.dtype)
        if with_gates:
            go_ref[...] = lax.dot_general(wg_ref[...], no_ref[...], (((1,), (1,)), ((), ())),
                                          preferred_element_type=F32)

    @pl.when(step % 2 == 0)
    def _():
        work(y_refs[0], y_refs[1])

    @pl.when(step % 2 == 1)
    def _():
        work(y_refs[1], y_refs[0])


def _out_proj(lhs, w, x, gain, tm, name, x_out, wg_t=None):
    t, d = x.shape
    n_tiles = t // tm
    ks = [a.shape[1] for a in lhs]
    assert all(k == ks[0] for k in ks)

    def cur(i):
        return jnp.minimum(i, n_tiles - 1)

    def prev(i):
        return jnp.maximum(i - 1, 0)

    in_specs = [pl.BlockSpec((tm, k), lambda i: (cur(i), 0)) for k in ks]
    in_specs += [pl.BlockSpec((k, d), lambda i, j=j: (j, 0)) for j, k in enumerate(ks)]
    in_specs += [pl.BlockSpec((tm, d), lambda i: (prev(i), 0)), pl.BlockSpec((1, d), lambda i: (0, 0))]
    args = list(lhs) + [w] * len(lhs) + [x, gain]
    out_shape, out_specs = [], []
    if wg_t is not None:
        in_specs.append(pl.BlockSpec(wg_t.shape, lambda i: (0, 0)))
        args.append(wg_t)
    if x_out:
        out_shape.append(jax.ShapeDtypeStruct((t, d), F32))
        out_specs.append(pl.BlockSpec((tm, d), lambda i: (prev(i), 0)))
    out_shape.append(jax.ShapeDtypeStruct((t, d), BF16 if x_out else F32))
    out_specs.append(pl.BlockSpec((tm, d), lambda i: (prev(i), 0)))
    if wg_t is not None:
        out_shape.append(jax.ShapeDtypeStruct((wg_t.shape[0], t), F32))
        out_specs.append(pl.BlockSpec((wg_t.shape[0], tm), lambda i: (0, prev(i))))
    return pl.pallas_call(
        functools.partial(_out_proj_kernel, n_lhs=len(lhs), with_x_out=x_out, with_gates=wg_t is not None,
                          tm=tm, chunk=64),
        out_shape=tuple(out_shape),
        grid=(n_tiles + 1,),
        in_specs=in_specs,
        out_specs=tuple(out_specs),
        scratch_shapes=[pltpu.VMEM((tm, d), F32), pltpu.VMEM((tm, d), F32)],
        compiler_params=_params("arbitrary"),
        name=name,
    )(*args)


def _rope_tables(seq_len):
    half = HEAD_DIM // 2
    t = jnp.arange(seq_len)
    row = (t // GRID_W).astype(F32)
    col = (t % GRID_W).astype(F32)
    inv = ROPE_THETA ** (-jnp.arange(0, half, 2, dtype=F32) / half)
    ar = row[:, None] * inv
    ac = col[:, None] * inv
    cos_t = jnp.concatenate([jnp.cos(ar), jnp.cos(ar), jnp.cos(ac), jnp.cos(ac)], axis=-1)
    sin_t = jnp.concatenate([-jnp.sin(ar), jnp.sin(ar), -jnp.sin(ac), jnp.sin(ac)], axis=-1)
    return cos_t, sin_t


def _norm_rope(x, gain, cos_v, sin_v):
    quarter = HEAD_DIM // 4
    sq = x * x
    hi = sq.astype(BF16)
    lo = (sq - hi.astype(F32)).astype(BF16)
    ms = jnp.dot(jnp.concatenate([hi, lo], axis=1), jnp.ones((2 * HEAD_DIM, HEAD_DIM), BF16),
                 preferred_element_type=F32) * (1.0 / HEAD_DIM)
    xn = x * lax.rsqrt(ms + EPS) * gain
    lane = lax.broadcasted_iota(jnp.int32, xn.shape, 1)
    first = (lane % (2 * quarter)) < quarter
    partner = jnp.where(first, pltpu.roll(xn, HEAD_DIM - quarter, 1), pltpu.roll(xn, quarter, 1))
    return xn * cos_v + partner * sin_v


def _attention_kernel(q_ref, k_ref, v_ref, g_ref, qg_ref, kg_ref, cos_ref, sin_ref, o_ref,
                      kr_ref, vt_ref, qr_ref, s_ref, of_ref, *, seq, tq, kc):
    n_blk = seq // tq
    n_kc = seq // kc
    n_total = Q_GROUP * n_blk
    kg = kg_ref[...]
    qg = qg_ref[...]
    q_scale = HEAD_DIM ** -0.5 * LOG2_E
    eye = jnp.where(lax.broadcasted_iota(jnp.int32, (HEAD_DIM, HEAD_DIM), 0)
                    == lax.broadcasted_iota(jnp.int32, (HEAD_DIM, HEAD_DIM), 1), 1.0, 0.0).astype(BF16)

    def prep(c, carry):
        rows = pl.ds(pl.multiple_of(c * tq, tq), tq)
        cos_v, sin_v = cos_ref[rows, :], sin_ref[rows, :]
        kr_ref[rows, :] = _norm_rope(k_ref[0, rows, :], kg, cos_v, sin_v).astype(BF16)
        vt_ref[:, rows] = lax.dot_general(eye, v_ref[0, rows, :].astype(BF16), (((1,), (1,)), ((), ())),
                                          preferred_element_type=F32).astype(BF16)
        for g in range(Q_GROUP):
            q = _norm_rope(q_ref[0, rows, g * HEAD_DIM:(g + 1) * HEAD_DIM], qg, cos_v, sin_v) * q_scale
            qr_ref[pl.ds(pl.multiple_of(g * seq + c * tq, tq), tq), :] = q.astype(BF16)
        return carry

    lax.fori_loop(0, n_blk, prep, 0, unroll=2 if n_blk % 2 == 0 else 1)

    def blk_rows(i):
        return pl.ds(pl.multiple_of(i * tq, tq), tq)

    def scores_chunk(i, j):
        return lax.dot_general(kr_ref[j * kc:(j + 1) * kc, :], qr_ref[blk_rows(i), :],
                               (((1,), (1,)), ((), ())), preferred_element_type=F32)

    def col_max8(x):
        return jnp.max(x.reshape(x.shape[0] // SUBLANES, SUBLANES, x.shape[1]), axis=0)

    def col_sum8(x):
        return jnp.sum(x.reshape(x.shape[0] // SUBLANES, SUBLANES, x.shape[1]), axis=0)

    neg_inf8 = jnp.full((SUBLANES, tq), -jnp.inf, F32)
    m8 = neg_inf8
    for j in range(n_kc):
        s_t = scores_chunk(0, j)
        s_ref[0, j] = s_t
        m8 = jnp.maximum(m8, col_max8(s_t))

    def block(i, m8_cur, slot):
        nxt = jnp.minimum(i + 1, n_total - 1)
        m_cur = jnp.max(m8_cur, axis=0, keepdims=True)
        m8_next = neg_inf8
        l8 = jnp.zeros((SUBLANES, tq), F32)
        acc = jnp.zeros((HEAD_DIM, tq), F32)
        for j in range(n_kc):
            s_next = scores_chunk(nxt, j)
            s_ref[1 - slot, j] = s_next
            m8_next = jnp.maximum(m8_next, col_max8(s_next))
            p_t = jnp.exp2(s_ref[slot, j] - m_cur)
            l8 = l8 + col_sum8(p_t)
            acc = acc + jnp.dot(vt_ref[:, j * kc:(j + 1) * kc], p_t.astype(BF16),
                                preferred_element_type=F32)
        l = jnp.sum(l8, axis=0, keepdims=True)
        of_ref[blk_rows(i), :] = (acc * (1.0 / l)).T
        return m8_next

    def block_group(ii, m8_cur):
        for r in range(blocks_per_step):
            m8_cur = block(blocks_per_step * ii + r, m8_cur, r % 2)
        return m8_cur

    blocks_per_step = 4
    lax.fori_loop(0, n_total // blocks_per_step, block_group, m8)

    def finish(c, carry):
        rows = pl.ds(pl.multiple_of(c * tq, tq), tq)
        for g in range(Q_GROUP):
            cols = slice(g * HEAD_DIM, (g + 1) * HEAD_DIM)
            gate = g_ref[0, rows, cols]
            o = of_ref[pl.ds(pl.multiple_of(g * seq + c * tq, tq), tq), :]
            o_ref[0, rows, cols] = (o * (gate * _sigmoid(gate))).astype(o_ref.dtype)
        return carry

    lax.fori_loop(0, n_blk, finish, 0)


def _attention(proj, q_gain, k_gain, cos_t, sin_t, tq):
    b, s, _ = proj.shape
    gw = Q_GROUP * HEAD_DIM
    k_blk0 = ATTN_WIDTH // HEAD_DIM
    v_blk0 = (ATTN_WIDTH + KV_WIDTH) // HEAD_DIM
    g_blk0 = (ATTN_WIDTH + 2 * KV_WIDTH) // gw
    kc = _row_tile(s, 512)
    return pl.pallas_call(
        functools.partial(_attention_kernel, seq=s, tq=tq, kc=kc),
        out_shape=jax.ShapeDtypeStruct((b, s, ATTN_WIDTH), BF16),
        grid=(b, KV_HEADS),
        in_specs=[
            pl.BlockSpec((1, s, gw), lambda i, j: (i, 0, j)),
            pl.BlockSpec((1, s, HEAD_DIM), lambda i, j: (i, 0, k_blk0 + j)),
            pl.BlockSpec((1, s, HEAD_DIM), lambda i, j: (i, 0, v_blk0 + j)),
            pl.BlockSpec((1, s, gw), lambda i, j: (i, 0, g_blk0 + j)),
            pl.BlockSpec((1, HEAD_DIM), lambda i, j: (0, 0)),
            pl.BlockSpec((1, HEAD_DIM), lambda i, j: (0, 0)),
            pl.BlockSpec((s, HEAD_DIM), lambda i, j: (0, 0)),
            pl.BlockSpec((s, HEAD_DIM), lambda i, j: (0, 0)),
        ],
        out_specs=pl.BlockSpec((1, s, gw), lambda i, j: (i, 0, j)),
        scratch_shapes=[
            pltpu.VMEM((s, HEAD_DIM), BF16),
            pltpu.VMEM((HEAD_DIM, s), BF16),
            pltpu.VMEM((Q_GROUP * s, HEAD_DIM), BF16),
            pltpu.VMEM((2, s // kc, kc, tq), F32),
            pltpu.VMEM((Q_GROUP * s, HEAD_DIM), F32),
        ],
        compiler_params=_params("parallel", "parallel"),
        name="attention",
    )(proj, proj, proj, proj, q_gain, k_gain, cos_t, sin_t)


def _lru_kernel(x_ref, g_ref, cw_ref, cb_ref, w_ref, lam_ref, o_ref,
                xpad_ref, af_ref, uf_ref, ab_ref, ub_ref, hf_ref, pf_ref, hb_ref, pb_ref,
                *, seq, n_row_chunks):
    seg = seq // SCAN_SEGMENTS + SUBLANES // 2
    rows_padded = SCAN_SEGMENTS * seg
    chunk = seq // n_row_chunks
    pad = SUBLANES
    a_refs = (af_ref, ab_ref)
    u_refs = (uf_ref, ub_ref)

    zeros = jnp.zeros((pad, LRU_BLOCK), F32)
    xpad_ref[0:pad, :] = zeros
    xpad_ref[pad + seq:pad + seq + pad, :] = zeros
    xpad_ref[pad:pad + seq, :] = x_ref[0]
    for ref in a_refs + u_refs:
        ref[seq:rows_padded, :] = jnp.zeros((rows_padded - seq, LRU_BLOCK), F32)

    cw = cw_ref[...]
    cb = cb_ref[...]
    w = w_ref[0]
    decay = (-0.5 * LRU_C) * _log_sigmoid(lam_ref[...])
    bias_taps = jnp.where(lax.broadcasted_iota(jnp.int32, (chunk, LRU_BLOCK), 1) < 2, 1.0, 0.0).astype(BF16)

    for k in range(n_row_chunks):
        half_xc = cb
        for j in range(CONV_W):
            lo = pad + k * chunk + j - CONV_LEFT
            half_xc = half_xc + cw[j:j + 1, :] * xpad_ref[lo:lo + chunk, :]
        lhs = jnp.concatenate([half_xc.astype(BF16), bias_taps], axis=1)
        th = jnp.tanh(jnp.dot(lhs, w, preferred_element_type=F32))
        for d in range(2):
            c0 = 2 * d * LRU_BLOCK
            th_r = th[:, c0:c0 + LRU_BLOCK]
            th_i = th[:, c0 + LRU_BLOCK:c0 + 2 * LRU_BLOCK]
            dec = decay[d:d + 1, :]
            neg_log_a = dec * th_r + dec
            ix = half_xc * th_i + half_xc
            a = jnp.exp2(neg_log_a * -LOG2_E)
            one_minus_a2 = jnp.tanh(neg_log_a) * (a * a + 1.0)
            a_refs[d][k * chunk:(k + 1) * chunk, :] = a
            u_refs[d][k * chunk:(k + 1) * chunk, :] = jnp.exp2(0.5 * jnp.log2(one_minus_a2)) * ix

    n_groups = SCAN_SEGMENTS // SUBLANES

    def rows_at(t, grp):
        return pl.ds(t + grp * SUBLANES * seg, SUBLANES, stride=seg)

    def pass1(t, carry):
        hf, pf, hb, pb = (list(c) for c in carry[:4])
        tb = carry[4]
        for grp in range(n_groups):
            a = af_ref[rows_at(t, grp), :]
            hf[grp] = a * hf[grp] + uf_ref[rows_at(t, grp), :]
            pf[grp] = pf[grp] * a
            hf_ref[rows_at(t, grp), :] = hf[grp]
            pf_ref[rows_at(t, grp), :] = pf[grp]
            a = ab_ref[rows_at(tb, grp), :]
            hb[grp] = a * hb[grp] + ub_ref[rows_at(tb, grp), :]
            pb[grp] = pb[grp] * a
            hb_ref[rows_at(tb, grp), :] = hb[grp]
            pb_ref[rows_at(tb, grp), :] = pb[grp]
        return tuple(hf), tuple(pf), tuple(hb), tuple(pb), tb - 1

    unroll = 2 if seg % 2 == 0 else 1
    zero = (jnp.zeros((SUBLANES, LRU_BLOCK), F32),) * n_groups
    one = (jnp.ones((SUBLANES, LRU_BLOCK), F32),) * n_groups
    hf, pf, hb, pb, _ = lax.fori_loop(0, seg, pass1, (zero, one, zero, one, jnp.int32(seg - 1)),
                                      unroll=unroll)

    row = lax.broadcasted_iota(jnp.int32, (SUBLANES, LRU_BLOCK), 0)

    def entering_states(h_end, p_end, order):
        tiles = [jnp.zeros((SUBLANES, LRU_BLOCK), F32)] * n_groups
        c = jnp.zeros((1, LRU_BLOCK), F32)
        for k in order:
            grp, sub = divmod(k, SUBLANES)
            tiles[grp] = jnp.where(row == sub, c, tiles[grp])
            c = h_end[grp][sub:sub + 1, :] + p_end[grp][sub:sub + 1, :] * c
        return tiles

    cf = entering_states(hf, pf, range(SCAN_SEGMENTS))
    cbk = entering_states(hb, pb, reversed(range(SCAN_SEGMENTS)))

    def pass2(t, carry):
        for grp in range(n_groups):
            y = (hf_ref[rows_at(t, grp), :] + pf_ref[rows_at(t, grp), :] * cf[grp]
                 + hb_ref[rows_at(t, grp), :] + pb_ref[rows_at(t, grp), :] * cbk[grp])
            uf_ref[rows_at(t, grp), :] = y
        return carry

    lax.fori_loop(0, seg, pass2, 0, unroll=unroll)

    for k in range(n_row_chunks):
        rows = slice(k * chunk, (k + 1) * chunk)
        half_gate = g_ref[0, rows, :]
        silu = half_gate * jnp.tanh(half_gate) + half_gate
        o_ref[0, rows, :] = (uf_ref[rows, :] * silu).astype(o_ref.dtype)


def _lru_gate_weights(wa, ba, wx, bx):
    w_cat = jnp.concatenate([wa[0], wx[0], wa[1], wx[1]], axis=-1).astype(BF16)
    half_b = 0.5 * jnp.concatenate([ba[0].reshape(LRU_BLOCKS, 1, LRU_BLOCK), bx[0].reshape(LRU_BLOCKS, 1, LRU_BLOCK),
                                    ba[1].reshape(LRU_BLOCKS, 1, LRU_BLOCK), bx[1].reshape(LRU_BLOCKS, 1, LRU_BLOCK)],
                                   axis=-1)
    hi = half_b.astype(BF16)
    lo = (half_b - hi.astype(F32)).astype(BF16)
    fill = jnp.zeros((LRU_BLOCKS, LRU_BLOCK - 2, 4 * LRU_BLOCK), BF16)
    return jnp.concatenate([w_cat, hi, lo, fill], axis=1)


def _lru(proj, half_conv_w, half_conv_b, w_gates, lam):
    b, s, _ = proj.shape
    x_blk0 = (2 * ATTN_WIDTH + 2 * KV_WIDTH) // LRU_BLOCK
    g_blk0 = x_blk0 + LRU_BLOCKS
    seg_rows = s + SCAN_SEGMENTS * (SUBLANES // 2)
    n_row_chunks = 8 if s % (8 * SUBLANES) == 0 else 1
    return pl.pallas_call(
        functools.partial(_lru_kernel, seq=s, n_row_chunks=n_row_chunks),
        out_shape=jax.ShapeDtypeStruct((b, s, LRU_WIDTH), BF16),
        grid=(b, LRU_BLOCKS),
        in_specs=[
            pl.BlockSpec((1, s, LRU_BLOCK), lambda i, n: (i, 0, x_blk0 + n)),
            pl.BlockSpec((1, s, LRU_BLOCK), lambda i, n: (i, 0, g_blk0 + n)),
            pl.BlockSpec((CONV_W, LRU_BLOCK), lambda i, n: (0, n)),
            pl.BlockSpec((1, LRU_BLOCK), lambda i, n: (0, n)),
            pl.BlockSpec((1, 2 * LRU_BLOCK, 4 * LRU_BLOCK), lambda i, n: (n, 0, 0)),
            pl.BlockSpec((2, LRU_BLOCK), lambda i, n: (0, n)),
        ],
        out_specs=pl.BlockSpec((1, s, LRU_BLOCK), lambda i, n: (i, 0, n)),
        scratch_shapes=[pltpu.VMEM((s + 2 * SUBLANES, LRU_BLOCK), F32)]
        + [pltpu.VMEM((seg_rows, LRU_BLOCK), F32) for _ in range(8)],
        compiler_params=_params("parallel", "parallel"),
        name="rg_lru",
    )(proj, proj, half_conv_w, half_conv_b, w_gates, lam)


def _hi_lo(x):
    hi = x.astype(BF16)
    lo = (x - hi.astype(F32)).astype(BF16)
    return hi, lo


def _dot_nt(a, b):
    return lax.dot_general(a, b, (((1,), (1,)), ((), ())), preferred_element_type=F32)


def _lane_tile(x, width):
    return jnp.concatenate([x] * (width // x.shape[1]), axis=1) if width != x.shape[1] else x


_G_C, _G_WS, _G_BTOT, _G_MW = range(4)


def _mlstm_gate_table(gate_ref, bias_ref, hh, head, tab_ref, bt_ref):
    chunk = MLSTM_CHUNK
    ri = lax.broadcasted_iota(jnp.int32, (chunk, chunk), 0)
    si = lax.broadcasted_iota(jnp.int32, (chunk, chunk), 1)
    for d in range(2):
        ig = gate_ref[d, hh, 0] + bias_ref[d, head]
        lf = _log_sigmoid(gate_ref[2 + d, hh, 0] + bias_ref[2 + d, head])
        upper = jnp.where((ri >= si) if d else (ri <= si), 1.0, 0.0).astype(BF16)
        hi, lo = _hi_lo(lf)
        b = (jnp.dot(hi, upper, preferred_element_type=F32)
             + jnp.dot(lo, upper, preferred_element_type=F32))
        b_tot = jnp.broadcast_to(b[:, 0:1] if d else b[:, chunk - 1:chunk], b.shape)
        w = b_tot - b + ig
        mw = jnp.broadcast_to(jnp.max(w, axis=-1, keepdims=True), b.shape)
        tab = tab_ref.at[hh * 2 + d]
        tab[_G_C] = ig - b
        tab[_G_WS] = jnp.exp(w - mw)
        tab[_G_BTOT] = b_tot
        tab[_G_MW] = mw
        b_square = jnp.concatenate([b, jnp.zeros((chunk - b.shape[0], chunk), F32)], axis=0)
        bt_ref[hh * 2 + d] = b_square.T


def _chunk_mask(d, chunk):
    li = lax.broadcasted_iota(jnp.int32, (chunk, chunk), 0)
    si = lax.broadcasted_iota(jnp.int32, (chunk, chunk), 1)
    return (si >= li) if d else (si <= li)


def _mlstm_intra_weights(d, ci, qk, k, tab, bt):
    chunk = qk.shape[0]
    mask = _chunk_mask(d, chunk)
    row = pl.ds(ci, 1)
    c_masked = jnp.where(mask, tab[_G_C, row, :], -jnp.inf)
    cm = jnp.broadcast_to(jnp.max(c_masked, axis=-1, keepdims=True), c_masked.shape)
    p0 = (qk * jnp.exp(c_masked - cm)).astype(BF16)
    ktw = (k.astype(F32).T * tab[_G_WS, row, :]).astype(BF16)
    lane = lax.broadcasted_iota(jnp.int32, (chunk, chunk), 1)
    b_l = jnp.broadcast_to(jnp.sum(jnp.where(lane == ci, bt[...], 0.0), axis=-1, keepdims=True),
                           (chunk, chunk))
    return p0, ktw, cm, b_l


def _mlstm_state_matmuls(q, v, p0, ktw, c_state, n_state):
    chunk = q.shape[0]
    ones = jnp.ones((chunk, LANES), BF16)
    pk = jnp.concatenate([p0, ktw], axis=0)
    nv = jnp.dot(pk, v, preferred_element_type=F32)
    rs = jnp.dot(pk, ones, preferred_element_type=F32)
    qc = jnp.dot(q, c_state.astype(BF16), preferred_element_type=F32)
    qn = jnp.dot(q, n_state.astype(BF16), preferred_element_type=F32)
    return nv[:chunk], rs[:chunk], nv[chunk:], rs[chunk:], qc, qn


def _mlstm_combine(b_l, cm, b_tot, mw, prods, c_state, n_state, m):
    n0, r0, kv0, kn0, qc, qn = prods
    dv = n0.shape[1]
    scale = MLSTM_QK_DIM ** -0.5
    mx = jnp.maximum(m, cm)
    e1 = jnp.exp(cm - mx) * scale
    e2 = jnp.exp(m - mx) * scale
    den = e1 * r0 + e2 * qn
    inv = 1.0 / jnp.maximum(jnp.abs(den), jnp.exp(-(b_l + mx)))
    hout = _lane_tile(e1 * inv, dv) * n0 + _lane_tile(e2 * inv, dv) * qc

    m_new = jnp.maximum(b_tot + m, mw)
    d1 = jnp.exp(b_tot + m - m_new)
    d2 = jnp.exp(mw - m_new)
    c_new = _lane_tile(d1, dv) * c_state + _lane_tile(d2, dv) * kv0
    n_new = d1 * n_state + d2 * kn0
    return hout, c_new, n_new, m_new


MLSTM_HEADS_PER_STEP = 2


def _mlstm_kernel(bias_ref, q_ref, k_ref, v_ref, o_ref, z_ref, gate_ref, ng_ref, out_ref,
                  h_ref, c_ref, n_ref, tab_ref, bt_ref, pw_ref, pm_ref, *, seq):
    chunk = MLSTM_CHUNK
    n_chunks = seq // chunk
    dk, dv = MLSTM_QK_DIM, MLSTM_V_DIM
    streams = [(hh, d) for hh in range(MLSTM_HEADS_PER_STEP) for d in range(2)]
    for hh in range(MLSTM_HEADS_PER_STEP):
        _mlstm_gate_table(gate_ref, bias_ref, hh, pl.program_id(1) * MLSTM_HEADS_PER_STEP + hh,
                          tab_ref, bt_ref)
    c_ref[...] = jnp.zeros_like(c_ref)
    n_ref[...] = jnp.zeros_like(n_ref)

    def chunk_of(d, c):
        return (n_chunks - 1 - c) if d else c

    def rows_of(ci):
        return pl.ds(pl.multiple_of(ci * chunk, chunk), chunk)

    def qk_cols(hh):
        return slice(hh * dk, (hh + 1) * dk)

    def v_cols(hh):
        return slice(hh * dv, (hh + 1) * dv)

    def intra_matmuls(s, c):
        hh, d = streams[s]
        rows = rows_of(chunk_of(d, c))
        return _dot_nt(q_ref[0, rows, qk_cols(hh)], k_ref[0, rows, qk_cols(hh)])

    def park_weights(s, c, slot, qk):
        hh, d = streams[s]
        ci = chunk_of(d, c)
        p0, ktw, cm, b_l = _mlstm_intra_weights(d, ci, qk, k_ref[0, rows_of(ci), qk_cols(hh)],
                                                tab_ref.at[s], bt_ref.at[s])
        pw_ref[slot, s, 0] = p0
        pw_ref[slot, s, 1] = ktw
        pm_ref[slot, s, 0] = b_l
        pm_ref[slot, s, 1] = cm

    for s in range(len(streams)):
        park_weights(s, 0, 0, intra_matmuls(s, 0))

    def step(c, carry):
        ms = list(carry)
        slot = c % 2
        c_next = jnp.minimum(c + 1, n_chunks - 1)
        prods, parts = [], []
        for s, (hh, d) in enumerate(streams):
            rows = rows_of(chunk_of(d, c))
            prods.append(_mlstm_state_matmuls(q_ref[0, rows, qk_cols(hh)], v_ref[0, rows, v_cols(hh)],
                                              pw_ref[slot, s, 0], pw_ref[slot, s, 1], c_ref[s], n_ref[s]))
            parts.append(intra_matmuls(s, c_next))
        for s, (hh, d) in enumerate(streams):
            ci = chunk_of(d, c)
            hout, c_new, n_new, ms[s] = _mlstm_combine(
                pm_ref[slot, s, 0], pm_ref[slot, s, 1],
                tab_ref[s, _G_BTOT, pl.ds(ci, 1), :], tab_ref[s, _G_MW, pl.ds(ci, 1), :],
                prods[s], c_ref[s], n_ref[s], ms[s])
            h_ref[s, rows_of(ci), :] = hout
            c_ref[s] = c_new
            n_ref[s] = n_new
            park_weights(s, c_next, 1 - slot, parts[s])
        return tuple(ms)

    def finish(ci):
        rows = rows_of(ci)
        for hh in range(MLSTM_HEADS_PER_STEP):
            sig_o = 0.5 * jnp.tanh(o_ref[0, rows, v_cols(hh)]) + 0.5
            hs = sig_o * (h_ref[2 * hh, rows, :] + h_ref[2 * hh + 1, rows, :])
            ms = jnp.mean(hs * hs, axis=-1, keepdims=True)
            hs = hs * lax.rsqrt(ms + EPS) * ng_ref[:, v_cols(hh)]
            half_z = z_ref[0, rows, v_cols(hh)]
            out_ref[0, rows, v_cols(hh)] = (hs * (half_z * jnp.tanh(half_z) + half_z)).astype(out_ref.dtype)

    def step_and_finish(c, carry):
        carry = step(c, carry)
        finish(c)
        finish(n_chunks - 1 - c)
        return carry

    half = n_chunks // 2
    m0 = (jnp.zeros((1, LANES), F32),) * len(streams)
    ms_half = lax.fori_loop(0, half, step, m0)
    lax.fori_loop(half, n_chunks, step_and_finish, ms_half)


def _mlstm(qkv, oz, gates_t, gate_bias, norm_gain):
    b, s, _ = qkv.shape
    hps = MLSTM_HEADS_PER_STEP
    dk, dv = hps * MLSTM_QK_DIM, hps * MLSTM_V_DIM
    n_chunks = s // MLSTM_CHUNK
    n_streams = 2 * hps
    k_blk0 = MLSTM_QK_WIDTH // dk
    v_blk0 = 2 * MLSTM_QK_WIDTH // dv
    z_blk0 = MLSTM_WIDTH // dv
    return pl.pallas_call(
        functools.partial(_mlstm_kernel, seq=s),
        out_shape=jax.ShapeDtypeStruct((b, s, MLSTM_WIDTH), BF16),
        grid=(b, MLSTM_HEADS // hps),
        in_specs=[
            pl.BlockSpec(memory_space=pltpu.SMEM),
            pl.BlockSpec((1, s, dk), lambda i, h: (i, 0, h)),
            pl.BlockSpec((1, s, dk), lambda i, h: (i, 0, k_blk0 + h)),
            pl.BlockSpec((1, s, dv), lambda i, h: (i, 0, v_blk0 + h)),
            pl.BlockSpec((1, s, dv), lambda i, h: (i, 0, h)),
            pl.BlockSpec((1, s, dv), lambda i, h: (i, 0, z_blk0 + h)),
            pl.BlockSpec((N_GATE_SETS, hps, 1, n_chunks, MLSTM_CHUNK), lambda i, h: (0, h, i, 0, 0)),
            pl.BlockSpec((1, dv), lambda i, h: (0, h)),
        ],
        out_specs=pl.BlockSpec((1, s, dv), lambda i, h: (i, 0, h)),
        scratch_shapes=[
            pltpu.VMEM((n_streams, s, MLSTM_V_DIM), F32),
            pltpu.VMEM((n_streams, MLSTM_QK_DIM, MLSTM_V_DIM), F32),
            pltpu.VMEM((n_streams, MLSTM_QK_DIM, LANES), F32),
            pltpu.VMEM((n_streams, 4, n_chunks, MLSTM_CHUNK), F32),
            pltpu.VMEM((n_streams, MLSTM_CHUNK, MLSTM_CHUNK), F32),
            pltpu.VMEM((2, n_streams, 2, MLSTM_CHUNK, MLSTM_CHUNK), BF16),
            pltpu.VMEM((2, n_streams, 2, MLSTM_CHUNK, MLSTM_CHUNK), F32),
        ],
        compiler_params=_params("parallel", "parallel"),
        name="mlstm",
    )(gate_bias, qkv, qkv, qkv, oz, oz, gates_t, norm_gain)


def _row_tile(t, want):
    return want if t % want == 0 else t


def kernel(x, norm_gain, final_gain, even_w_in, even_w_out, q_norm_gain, k_norm_gain, conv_w, conv_b,
           lru_wa, lru_ba, lru_wx, lru_bx, lru_lambda, odd_w_in, odd_gate_bias, odd_norm_gain, odd_w_out):
    b, s, d = x.shape
    t = b * s
    assert s % (SCAN_SEGMENTS * SUBLANES) == 0 and s % (2 * MLSTM_CHUNK) == 0 and s % GRID_W == 0
    x2 = x.reshape(t, d)

    hn0 = _rmsnorm(x2, norm_gain[0][None], tm=_row_tile(t, 512))
    proj0 = _in_proj0(hn0, even_w_in[0], F32, tm=_row_tile(t, 1024), tn=768,
                      half_from_col=EVEN_IN - LRU_WIDTH).reshape(b, s, EVEN_IN)
    cos_t, sin_t = _rope_tables(s)
    attn = _attention(proj0, q_norm_gain[0][None], k_norm_gain[0][None], cos_t, sin_t,
                      tq=_row_tile(s, 256))
    lru = _lru(proj0, 0.5 * conv_w[0], 0.5 * conv_b[0][None],
               _lru_gate_weights(lru_wa[0], lru_ba[0], lru_wx[0], lru_bx[0]), lru_lambda[0])
    qkv_w = 2 * MLSTM_QK_WIDTH + MLSTM_WIDTH
    oz_w = 2 * MLSTM_WIDTH
    w1t = odd_w_in[0].T
    x1, hn1, gates_t = _out_proj([attn.reshape(t, ATTN_WIDTH), lru.reshape(t, LRU_WIDTH)],
                                 even_w_out[0].astype(BF16), x2, norm_gain[1][None], _row_tile(t, 512),
                                 "out_proj0", x_out=True, wg_t=w1t[qkv_w + oz_w:].astype(BF16))

    tm1 = _row_tile(t, 1024)
    tn1 = 1024
    qkv = _matmul(hn1, w1t, qkv_w, 0, BF16, tm1, tn1, "in_proj1_qkv").reshape(b, s, qkv_w)
    oz = _matmul(hn1, w1t, oz_w, qkv_w // tn1, F32, tm1, tn1, "in_proj1_oz", w_scale=0.5).reshape(b, s, oz_w)
    gates_t = gates_t.reshape(N_GATE_SETS, MLSTM_HEADS, b, s // MLSTM_CHUNK, MLSTM_CHUNK)
    mix1 = _mlstm(qkv, oz, gates_t, odd_gate_bias[0], odd_norm_gain[0][None])
    (out,) = _out_proj([mix1.reshape(t, MLSTM_WIDTH)], odd_w_out[0].astype(BF16), x1, final_gain[None],
                       _row_tile(t, 512), "out_proj1", x_out=False)
    return out.reshape(b, s, d)
```

```python
import functools

import jax
import jax.numpy as jnp
from jax import lax
from jax.experimental import pallas as pl
from jax.experimental.pallas import tpu as pltpu

F32 = jnp.float32
BF16 = jnp.bfloat16

EPS = 1e-6
GRID_W = 64
HEAD_DIM = 128
ATTN_HEADS = 8
KV_HEADS = 2
Q_GROUP = ATTN_HEADS // KV_HEADS
ATTN_WIDTH = ATTN_HEADS * HEAD_DIM
KV_WIDTH = KV_HEADS * HEAD_DIM
ROPE_THETA = 10000.0
LRU_WIDTH = 1024
LRU_BLOCKS = 8
LRU_BLOCK = 128
LRU_C = 8.0
CONV_W = 4
CONV_LEFT = 2
EVEN_IN = ATTN_WIDTH + 2 * KV_WIDTH + ATTN_WIDTH + 2 * LRU_WIDTH
MLSTM_HEADS = 8
MLSTM_V_DIM = 256
MLSTM_QK_DIM = 128
MLSTM_WIDTH = MLSTM_HEADS * MLSTM_V_DIM
MLSTM_QK_WIDTH = MLSTM_HEADS * MLSTM_QK_DIM
MLSTM_CHUNK = 128
N_GATE_SETS = 4

LOG2_E = 1.4426950408889634

LANES = 128
SUBLANES = 8
VMEM_LIMIT = 56 * 1024 * 1024
SCAN_SEGMENTS = 4 * SUBLANES


def _sigmoid(x):
    return 1.0 / (1.0 + jnp.exp(-x))


def _log_sigmoid(x):
    return jnp.minimum(x, 0.0) - jnp.log1p(jnp.exp(-jnp.abs(x)))


def _params(*sem):
    return pltpu.CompilerParams(dimension_semantics=sem, vmem_limit_bytes=VMEM_LIMIT)


def _rmsnorm_rows_to(x_ref, g_ref, dst_ref, rows, chunk):
    g = g_ref[...]

    def body(c, carry):
        r0 = pl.multiple_of(c * chunk, chunk)
        x = x_ref[pl.ds(r0, chunk), :]
        ms = jnp.mean(x * x, axis=-1, keepdims=True)
        dst_ref[pl.ds(r0, chunk), :] = (x * lax.rsqrt(ms + EPS) * g).astype(dst_ref.dtype)
        return carry

    lax.fori_loop(0, rows // chunk, body, 0)


def _rmsnorm_kernel(x_ref, g_ref, o_ref, *, tm):
    _rmsnorm_rows_to(x_ref, g_ref, o_ref, tm, 64)


def _rmsnorm(x, gain, tm):
    t, d = x.shape
    return pl.pallas_call(
        functools.partial(_rmsnorm_kernel, tm=tm),
        out_shape=jax.ShapeDtypeStruct((t, d), BF16),
        grid=(t // tm,),
        in_specs=[pl.BlockSpec((tm, d), lambda i: (i, 0)), pl.BlockSpec((1, d), lambda i: (0, 0))],
        out_specs=pl.BlockSpec((tm, d), lambda i: (i, 0)),
        compiler_params=_params("parallel"),
        name="rmsnorm0",
    )(x, gain)


def _in_proj0_kernel(a_ref, w_ref, o_ref, wb_ref, *, half_from, k_chunk):
    @pl.when(pl.program_id(1) == 0)
    def _():
        tn = w_ref.shape[1]
        col = pl.program_id(0) * tn + lax.broadcasted_iota(jnp.int32, (1, tn), 1)
        scale = jnp.where(col >= half_from, 0.5, 1.0).astype(F32)

        def cast(c, carry):
            rows = pl.ds(pl.multiple_of(c * k_chunk, k_chunk), k_chunk)
            wb_ref[rows, :] = (w_ref[rows, :] * scale).astype(wb_ref.dtype)
            return carry

        lax.fori_loop(0, w_ref.shape[0] // k_chunk, cast, 0)

    o_ref[...] = jnp.dot(a_ref[...], wb_ref[...], preferred_element_type=F32).astype(o_ref.dtype)


def _in_proj0(a, w, out_dtype, tm, tn, half_from_col):
    t, k = a.shape
    n = w.shape[1]
    return pl.pallas_call(
        functools.partial(_in_proj0_kernel, half_from=half_from_col, k_chunk=128),
        out_shape=jax.ShapeDtypeStruct((t, n), out_dtype),
        grid=(n // tn, t // tm),
        in_specs=[
            pl.BlockSpec((tm, k), lambda j, i: (i, 0)),
            pl.BlockSpec((k, tn), lambda j, i: (0, j)),
        ],
        out_specs=pl.BlockSpec((tm, tn), lambda j, i: (i, j)),
        scratch_shapes=[pltpu.VMEM((k, tn), BF16)],
        compiler_params=_params("parallel", "arbitrary"),
        name="in_proj0",
    )(a, w)


def _matmul_kernel(a_ref, wt_ref, o_ref, wb_ref, *, w_scale, n_chunk):
    @pl.when(pl.program_id(1) == 0)
    def _():
        def cast(c, carry):
            rows = pl.ds(pl.multiple_of(c * n_chunk, n_chunk), n_chunk)
            wb_ref[rows, :] = (wt_ref[rows, :] * w_scale).astype(wb_ref.dtype)
            return carry

        lax.fori_loop(0, wt_ref.shape[0] // n_chunk, cast, 0)

    o_ref[...] = lax.dot_general(a_ref[...], wb_ref[...], (((1,), (1,)), ((), ())),
                                 preferred_element_type=F32).astype(o_ref.dtype)


def _matmul(a, wt, n, row_block0, out_dtype, tm, tn, name, w_scale=1.0):
    t, k = a.shape
    return pl.pallas_call(
        functools.partial(_matmul_kernel, w_scale=w_scale, n_chunk=64),
        out_shape=jax.ShapeDtypeStruct((t, n), out_dtype),
        grid=(n // tn, t // tm),
        in_specs=[
            pl.BlockSpec((tm, k), lambda j, i: (i, 0)),
            pl.BlockSpec((tn, k), lambda j, i: (j + row_block0, 0)),
        ],
        out_specs=pl.BlockSpec((tm, tn), lambda j, i: (i, j)),
        scratch_shapes=[pltpu.VMEM((tn, k), BF16)],
        compiler_params=_params("parallel", "arbitrary"),
        name=name,
    )(a, wt)


def _out_proj_kernel(*refs, n_lhs, with_x_out, with_gates, tm, chunk):
    lhs = refs[:n_lhs]
    ws = refs[n_lhs:2 * n_lhs]
    x_ref, g_ref = refs[2 * n_lhs:2 * n_lhs + 2]
    pos = 2 * n_lhs + 2
    wg_ref = refs[pos] if with_gates else None
    pos += int(with_gates)
    xo_ref = refs[pos] if with_x_out else None
    pos += int(with_x_out)
    no_ref = refs[pos]
    go_ref = refs[pos + 1] if with_gates else None
    y_refs = refs[-2:]
    step = pl.program_id(0)

    @pl.when(step == 0)
    def _():
        y_refs[1][...] = jnp.zeros_like(y_refs[1])

    def work(cur_ref, prev_ref):
        g = g_ref[...]
        d = cur_ref.shape[1]
        n_col = 4
        col_w = d // n_col
        chunks_per_col = tm // chunk // n_col
        for cb in range(n_col):
            cols = slice(cb * col_w, (cb + 1) * col_w)
            acc = jnp.dot(lhs[0][...], ws[0][:, cols], preferred_element_type=F32)
            for a_ref, w_ref in zip(lhs[1:], ws[1:]):
                acc = acc + jnp.dot(a_ref[...], w_ref[:, cols], preferred_element_type=F32)
            cur_ref[:, cols] = acc
            for c in range(cb * chunks_per_col, (cb + 1) * chunks_per_col):
                rows = slice(c * chunk, (c + 1) * chunk)
                xn = x_ref[rows, :] + prev_ref[rows, :]
                if with_x_out:
                    xo_ref[rows, :] = xn
                ms = jnp.mean(xn * xn, axis=-1, keepdims=True)
                no_ref[rows, :] = (xn * lax.rsqrt(ms + EPS) * g).astype(no_ref.dtype)
            if with_gates:
                rows = slice(cb * chunks_per_col * chunk, (cb + 1) * chunks_per_col * chunk)
                go_ref[:, rows] = lax.dot_general(wg_ref[...], no_ref[rows, :], (((1,), (1,)), ((), ())),
                                                  preferred_element_type=F32)

    @pl.when(step % 2 == 0)
    def _():
        work(y_refs[0], y_refs[1])

    @pl.when(step % 2 == 1)
    def _():
        work(y_refs[1], y_refs[0])


def _out_proj(lhs, w, x, gain, tm, name, x_out, wg_t=None):
    t, d = x.shape
    n_tiles = t // tm
    ks = [a.shape[1] for a in lhs]
    assert all(k == ks[0] for k in ks)

    def cur(i):
        return jnp.minimum(i, n_tiles - 1)

    def prev(i):
        return jnp.maximum(i - 1, 0)

    in_specs = [pl.BlockSpec((tm, k), lambda i: (cur(i), 0)) for k in ks]
    in_specs += [pl.BlockSpec((k, d), lambda i, j=j: (j, 0)) for j, k in enumerate(ks)]
    in_specs += [pl.BlockSpec((tm, d), lambda i: (prev(i), 0)), pl.BlockSpec((1, d), lambda i: (0, 0))]
    args = list(lhs) + [w] * len(lhs) + [x, gain]
    out_shape, out_specs = [], []
    if wg_t is not None:
        in_specs.append(pl.BlockSpec(wg_t.shape, lambda i: (0, 0)))
        args.append(wg_t)
    if x_out:
        out_shape.append(jax.ShapeDtypeStruct((t, d), F32))
        out_specs.append(pl.BlockSpec((tm, d), lambda i: (prev(i), 0)))
    out_shape.append(jax.ShapeDtypeStruct((t, d), BF16 if x_out else F32))
    out_specs.append(pl.BlockSpec((tm, d), lambda i: (prev(i), 0)))
    if wg_t is not None:
        out_shape.append(jax.ShapeDtypeStruct((wg_t.shape[0], t), F32))
        out_specs.append(pl.BlockSpec((wg_t.shape[0], tm), lambda i: (0, prev(i))))
    return pl.pallas_call(
        functools.partial(_out_proj_kernel, n_lhs=len(lhs), with_x_out=x_out, with_gates=wg_t is not None,
                          tm=tm, chunk=64),
        out_shape=tuple(out_shape),
        grid=(n_tiles + 1,),
        in_specs=in_specs,
        out_specs=tuple(out_specs),
        scratch_shapes=[pltpu.VMEM((tm, d), F32), pltpu.VMEM((tm, d), F32)],
        compiler_params=_params("arbitrary"),
        name=name,
    )(*args)


def _rope_tables(seq_len):
    half = HEAD_DIM // 2
    t = jnp.arange(seq_len)
    row = (t // GRID_W).astype(F32)
    col = (t % GRID_W).astype(F32)
    inv = ROPE_THETA ** (-jnp.arange(0, half, 2, dtype=F32) / half)
    ar = row[:, None] * inv
    ac = col[:, None] * inv
    cos_t = jnp.concatenate([jnp.cos(ar), jnp.cos(ar), jnp.cos(ac), jnp.cos(ac)], axis=-1)
    sin_t = jnp.concatenate([-jnp.sin(ar), jnp.sin(ar), -jnp.sin(ac), jnp.sin(ac)], axis=-1)
    return cos_t, sin_t


def _norm_rope(x, gain, cos_v, sin_v):
    quarter = HEAD_DIM // 4
    sq = x * x
    hi = sq.astype(BF16)
    lo = (sq - hi.astype(F32)).astype(BF16)
    ms = jnp.dot(jnp.concatenate([hi, lo], axis=1), jnp.ones((2 * HEAD_DIM, HEAD_DIM), BF16),
                 preferred_element_type=F32) * (1.0 / HEAD_DIM)
    xn = x * lax.rsqrt(ms + EPS) * gain
    lane = lax.broadcasted_iota(jnp.int32, xn.shape, 1)
    first = (lane % (2 * quarter)) < quarter
    partner = jnp.where(first, pltpu.roll(xn, HEAD_DIM - quarter, 1), pltpu.roll(xn, quarter, 1))
    return xn * cos_v + partner * sin_v


def _attention_kernel(q_ref, k_ref, v_ref, g_ref, qg_ref, kg_ref, cos_ref, sin_ref, o_ref,
                      kr_ref, vt_ref, qr_ref, s_ref, of_ref, *, seq, tq, kc):
    n_blk = seq // tq
    n_kc = seq // kc
    n_total = Q_GROUP * n_blk
    kg = kg_ref[...]
    qg = qg_ref[...]
    q_scale = HEAD_DIM ** -0.5 * LOG2_E
    eye = jnp.where(lax.broadcasted_iota(jnp.int32, (HEAD_DIM, HEAD_DIM), 0)
                    == lax.broadcasted_iota(jnp.int32, (HEAD_DIM, HEAD_DIM), 1), 1.0, 0.0).astype(BF16)

    def prep(c, carry):
        rows = pl.ds(pl.multiple_of(c * tq, tq), tq)
        cos_v, sin_v = cos_ref[rows, :], sin_ref[rows, :]
        kr_ref[rows, :] = _norm_rope(k_ref[0, rows, :], kg, cos_v, sin_v).astype(BF16)
        vt_ref[:, rows] = lax.dot_general(eye, v_ref[0, rows, :].astype(BF16), (((1,), (1,)), ((), ())),
                                          preferred_element_type=F32).astype(BF16)
        for g in range(Q_GROUP):
            q = _norm_rope(q_ref[0, rows, g * HEAD_DIM:(g + 1) * HEAD_DIM], qg, cos_v, sin_v) * q_scale
            qr_ref[pl.ds(pl.multiple_of(g * seq + c * tq, tq), tq), :] = q.astype(BF16)
        return carry

    lax.fori_loop(0, n_blk, prep, 0, unroll=2 if n_blk % 2 == 0 else 1)

    def blk_rows(i):
        return pl.ds(pl.multiple_of(i * tq, tq), tq)

    def scores_chunk(i, j):
        return lax.dot_general(kr_ref[j * kc:(j + 1) * kc, :], qr_ref[blk_rows(i), :],
                               (((1,), (1,)), ((), ())), preferred_element_type=F32)

    def col_max8(x):
        return jnp.max(x.reshape(x.shape[0] // SUBLANES, SUBLANES, x.shape[1]), axis=0)

    def col_sum8(x):
        return jnp.sum(x.reshape(x.shape[0] // SUBLANES, SUBLANES, x.shape[1]), axis=0)

    neg_inf8 = jnp.full((SUBLANES, tq), -jnp.inf, F32)
    m8 = neg_inf8
    for j in range(n_kc):
        s_t = scores_chunk(0, j)
        s_ref[0, j] = s_t
        m8 = jnp.maximum(m8, col_max8(s_t))

    def block(i, m8_cur, slot):
        nxt = jnp.minimum(i + 1, n_total - 1)
        m_cur = jnp.max(m8_cur, axis=0, keepdims=True)
        m8_next = neg_inf8
        l8 = jnp.zeros((SUBLANES, tq), F32)
        acc = jnp.zeros((HEAD_DIM, tq), F32)
        for j in range(n_kc):
            s_next = scores_chunk(nxt, j)
            s_ref[1 - slot, j] = s_next
            m8_next = jnp.maximum(m8_next, col_max8(s_next))
            p_t = jnp.exp2(s_ref[slot, j] - m_cur)
            l8 = l8 + col_sum8(p_t)
            acc = acc + jnp.dot(vt_ref[:, j * kc:(j + 1) * kc], p_t.astype(BF16),
                                preferred_element_type=F32)
        l = jnp.sum(l8, axis=0, keepdims=True)
        of_ref[blk_rows(i), :] = (acc * (1.0 / l)).T
        return m8_next

    def block_group(ii, m8_cur):
        for r in range(blocks_per_step):
            m8_cur = block(blocks_per_step * ii + r, m8_cur, r % 2)
        return m8_cur

    blocks_per_step = 4
    lax.fori_loop(0, n_total // blocks_per_step, block_group, m8)

    def finish(c, carry):
        rows = pl.ds(pl.multiple_of(c * tq, tq), tq)
        for g in range(Q_GROUP):
            cols = slice(g * HEAD_DIM, (g + 1) * HEAD_DIM)
            gate = g_ref[0, rows, cols]
            o = of_ref[pl.ds(pl.multiple_of(g * seq + c * tq, tq), tq), :]
            o_ref[0, rows, cols] = (o * (gate * _sigmoid(gate))).astype(o_ref.dtype)
        return carry

    lax.fori_loop(0, n_blk, finish, 0)


def _attention(proj, q_gain, k_gain, cos_t, sin_t, tq):
    b, s, _ = proj.shape
    gw = Q_GROUP * HEAD_DIM
    k_blk0 = ATTN_WIDTH // HEAD_DIM
    v_blk0 = (ATTN_WIDTH + KV_WIDTH) // HEAD_DIM
    g_blk0 = (ATTN_WIDTH + 2 * KV_WIDTH) // gw
    kc = _row_tile(s, 512)
    return pl.pallas_call(
        functools.partial(_attention_kernel, seq=s, tq=tq, kc=kc),
        out_shape=jax.ShapeDtypeStruct((b, s, ATTN_WIDTH), BF16),
        grid=(b, KV_HEADS),
        in_specs=[
            pl.BlockSpec((1, s, gw), lambda i, j: (i, 0, j)),
            pl.BlockSpec((1, s, HEAD_DIM), lambda i, j: (i, 0, k_blk0 + j)),
            pl.BlockSpec((1, s, HEAD_DIM), lambda i, j: (i, 0, v_blk0 + j)),
            pl.BlockSpec((1, s, gw), lambda i, j: (i, 0, g_blk0 + j)),
            pl.BlockSpec((1, HEAD_DIM), lambda i, j: (0, 0)),
            pl.BlockSpec((1, HEAD_DIM), lambda i, j: (0, 0)),
            pl.BlockSpec((s, HEAD_DIM), lambda i, j: (0, 0)),
            pl.BlockSpec((s, HEAD_DIM), lambda i, j: (0, 0)),
        ],
        out_specs=pl.BlockSpec((1, s, gw), lambda i, j: (i, 0, j)),
        scratch_shapes=[
            pltpu.VMEM((s, HEAD_DIM), BF16),
            pltpu.VMEM((HEAD_DIM, s), BF16),
            pltpu.VMEM((Q_GROUP * s, HEAD_DIM), BF16),
            pltpu.VMEM((2, s // kc, kc, tq), F32),
            pltpu.VMEM((Q_GROUP * s, HEAD_DIM), F32),
        ],
        compiler_params=_params("parallel", "parallel"),
        name="attention",
    )(proj, proj, proj, proj, q_gain, k_gain, cos_t, sin_t)


def _lru_kernel(x_ref, g_ref, cw_ref, cb_ref, w_ref, lam_ref, o_ref,
                xpad_ref, af_ref, uf_ref, ab_ref, ub_ref, hf_ref, pf_ref, hb_ref, pb_ref,
                *, seq, n_row_chunks):
    seg = seq // SCAN_SEGMENTS + SUBLANES // 2
    rows_padded = SCAN_SEGMENTS * seg
    chunk = seq // n_row_chunks
    pad = SUBLANES
    a_refs = (af_ref, ab_ref)
    u_refs = (uf_ref, ub_ref)

    zeros = jnp.zeros((pad, LRU_BLOCK), F32)
    xpad_ref[0:pad, :] = zeros
    xpad_ref[pad + seq:pad + seq + pad, :] = zeros
    xpad_ref[pad:pad + seq, :] = x_ref[0]
    for ref in a_refs + u_refs:
        ref[seq:rows_padded, :] = jnp.zeros((rows_padded - seq, LRU_BLOCK), F32)

    cw = cw_ref[...]
    cb = cb_ref[...]
    w = w_ref[0]
    decay = (-0.5 * LRU_C) * _log_sigmoid(lam_ref[...])
    bias_taps = jnp.where(lax.broadcasted_iota(jnp.int32, (chunk, LRU_BLOCK), 1) < 2, 1.0, 0.0).astype(BF16)

    for k in range(n_row_chunks):
        half_xc = cb
        for j in range(CONV_W):
            lo = pad + k * chunk + j - CONV_LEFT
            half_xc = half_xc + cw[j:j + 1, :] * xpad_ref[lo:lo + chunk, :]
        lhs = jnp.concatenate([half_xc.astype(BF16), bias_taps], axis=1)
        th = jnp.tanh(jnp.dot(lhs, w, preferred_element_type=F32))
        for d in range(2):
            c0 = 2 * d * LRU_BLOCK
            th_r = th[:, c0:c0 + LRU_BLOCK]
            th_i = th[:, c0 + LRU_BLOCK:c0 + 2 * LRU_BLOCK]
            dec = decay[d:d + 1, :]
            neg_log_a = dec * th_r + dec
            ix = half_xc * th_i + half_xc
            a = jnp.exp2(neg_log_a * -LOG2_E)
            one_minus_a2 = jnp.tanh(neg_log_a) * (a * a + 1.0)
            a_refs[d][k * chunk:(k + 1) * chunk, :] = a
            u_refs[d][k * chunk:(k + 1) * chunk, :] = jnp.exp2(0.5 * jnp.log2(one_minus_a2)) * ix

    n_groups = SCAN_SEGMENTS // SUBLANES

    def rows_at(t, grp):
        return pl.ds(t + grp * SUBLANES * seg, SUBLANES, stride=seg)

    def pass1(t, carry):
        hf, pf, hb, pb = (list(c) for c in carry[:4])
        tb = carry[4]
        for grp in range(n_groups):
            a = af_ref[rows_at(t, grp), :]
            hf[grp] = a * hf[grp] + uf_ref[rows_at(t, grp), :]
            pf[grp] = pf[grp] * a
            hf_ref[rows_at(t, grp), :] = hf[grp]
            pf_ref[rows_at(t, grp), :] = pf[grp]
            a = ab_ref[rows_at(tb, grp), :]
            hb[grp] = a * hb[grp] + ub_ref[rows_at(tb, grp), :]
            pb[grp] = pb[grp] * a
            hb_ref[rows_at(tb, grp), :] = hb[grp]
            pb_ref[rows_at(tb, grp), :] = pb[grp]
        return tuple(hf), tuple(pf), tuple(hb), tuple(pb), tb - 1

    unroll = 2 if seg % 2 == 0 else 1
    zero = (jnp.zeros((SUBLANES, LRU_BLOCK), F32),) * n_groups
    one = (jnp.ones((SUBLANES, LRU_BLOCK), F32),) * n_groups
    hf, pf, hb, pb, _ = lax.fori_loop(0, seg, pass1, (zero, one, zero, one, jnp.int32(seg - 1)),
                                      unroll=unroll)

    row = lax.broadcasted_iota(jnp.int32, (SUBLANES, LRU_BLOCK), 0)

    def entering_states(h_end, p_end, order):
        tiles = [jnp.zeros((SUBLANES, LRU_BLOCK), F32)] * n_groups
        c = jnp.zeros((1, LRU_BLOCK), F32)
        for k in order:
            grp, sub = divmod(k, SUBLANES)
            tiles[grp] = jnp.where(row == sub, c, tiles[grp])
            c = h_end[grp][sub:sub + 1, :] + p_end[grp][sub:sub + 1, :] * c
        return tiles

    cf = entering_states(hf, pf, range(SCAN_SEGMENTS))
    cbk = entering_states(hb, pb, reversed(range(SCAN_SEGMENTS)))

    def pass2(t, carry):
        for grp in range(n_groups):
            y = (hf_ref[rows_at(t, grp), :] + pf_ref[rows_at(t, grp), :] * cf[grp]
                 + hb_ref[rows_at(t, grp), :] + pb_ref[rows_at(t, grp), :] * cbk[grp])
            uf_ref[rows_at(t, grp), :] = y
        return carry

    lax.fori_loop(0, seg, pass2, 0, unroll=unroll)

    for k in range(n_row_chunks):
        rows = slice(k * chunk, (k + 1) * chunk)
        half_gate = g_ref[0, rows, :]
        silu = half_gate * jnp.tanh(half_gate) + half_gate
        o_ref[0, rows, :] = (uf_ref[rows, :] * silu).astype(o_ref.dtype)


def _lru_gate_weights(wa, ba, wx, bx):
    w_cat = jnp.concatenate([wa[0], wx[0], wa[1], wx[1]], axis=-1).astype(BF16)
    half_b = 0.5 * jnp.concatenate([ba[0].reshape(LRU_BLOCKS, 1, LRU_BLOCK), bx[0].reshape(LRU_BLOCKS, 1, LRU_BLOCK),
                                    ba[1].reshape(LRU_BLOCKS, 1, LRU_BLOCK), bx[1].reshape(LRU_BLOCKS, 1, LRU_BLOCK)],
                                   axis=-1)
    hi = half_b.astype(BF16)
    lo = (half_b - hi.astype(F32)).astype(BF16)
    fill = jnp.zeros((LRU_BLOCKS, LRU_BLOCK - 2, 4 * LRU_BLOCK), BF16)
    return jnp.concatenate([w_cat, hi, lo, fill], axis=1)


def _lru(proj, half_conv_w, half_conv_b, w_gates, lam):
    b, s, _ = proj.shape
    x_blk0 = (2 * ATTN_WIDTH + 2 * KV_WIDTH) // LRU_BLOCK
    g_blk0 = x_blk0 + LRU_BLOCKS
    seg_rows = s + SCAN_SEGMENTS * (SUBLANES // 2)
    n_row_chunks = 8 if s % (8 * SUBLANES) == 0 else 1
    return pl.pallas_call(
        functools.partial(_lru_kernel, seq=s, n_row_chunks=n_row_chunks),
        out_shape=jax.ShapeDtypeStruct((b, s, LRU_WIDTH), BF16),
        grid=(b, LRU_BLOCKS),
        in_specs=[
            pl.BlockSpec((1, s, LRU_BLOCK), lambda i, n: (i, 0, x_blk0 + n)),
            pl.BlockSpec((1, s, LRU_BLOCK), lambda i, n: (i, 0, g_blk0 + n)),
            pl.BlockSpec((CONV_W, LRU_BLOCK), lambda i, n: (0, n)),
            pl.BlockSpec((1, LRU_BLOCK), lambda i, n: (0, n)),
            pl.BlockSpec((1, 2 * LRU_BLOCK, 4 * LRU_BLOCK), lambda i, n: (n, 0, 0)),
            pl.BlockSpec((2, LRU_BLOCK), lambda i, n: (0, n)),
        ],
        out_specs=pl.BlockSpec((1, s, LRU_BLOCK), lambda i, n: (i, 0, n)),
        scratch_shapes=[pltpu.VMEM((s + 2 * SUBLANES, LRU_BLOCK), F32)]
        + [pltpu.VMEM((seg_rows, LRU_BLOCK), F32) for _ in range(8)],
        compiler_params=_params("parallel", "parallel"),
        name="rg_lru",
    )(proj, proj, half_conv_w, half_conv_b, w_gates, lam)


def _hi_lo(x):
    hi = x.astype(BF16)
    lo = (x - hi.astype(F32)).astype(BF16)
    return hi, lo


def _dot_nt(a, b):
    return lax.dot_general(a, b, (((1,), (1,)), ((), ())), preferred_element_type=F32)


def _lane_tile(x, width):
    return jnp.concatenate([x] * (width // x.shape[1]), axis=1) if width != x.shape[1] else x


_G_C, _G_WS, _G_BTOT, _G_MW = range(4)


def _mlstm_gate_table(gate_ref, bias_ref, hh, head, tab_ref, bt_ref):
    chunk = MLSTM_CHUNK
    ri = lax.broadcasted_iota(jnp.int32, (chunk, chunk), 0)
    si = lax.broadcasted_iota(jnp.int32, (chunk, chunk), 1)
    for d in range(2):
        ig = gate_ref[d, hh, 0] + bias_ref[d, head]
        lf = _log_sigmoid(gate_ref[2 + d, hh, 0] + bias_ref[2 + d, head])
        upper = jnp.where((ri >= si) if d else (ri <= si), 1.0, 0.0).astype(BF16)
        hi, lo = _hi_lo(lf)
        b = (jnp.dot(hi, upper, preferred_element_type=F32)
             + jnp.dot(lo, upper, preferred_element_type=F32))
        b_tot = jnp.broadcast_to(b[:, 0:1] if d else b[:, chunk - 1:chunk], b.shape)
        w = b_tot - b + ig
        mw = jnp.broadcast_to(jnp.max(w, axis=-1, keepdims=True), b.shape)
        tab = tab_ref.at[hh * 2 + d]
        tab[_G_C] = ig - b
        tab[_G_WS] = jnp.exp(w - mw)
        tab[_G_BTOT] = b_tot
        tab[_G_MW] = mw
        b_square = jnp.concatenate([b, jnp.zeros((chunk - b.shape[0], chunk), F32)], axis=0)
        bt_ref[hh * 2 + d] = b_square.T


def _chunk_mask(d, chunk):
    li = lax.broadcasted_iota(jnp.int32, (chunk, chunk), 0)
    si = lax.broadcasted_iota(jnp.int32, (chunk, chunk), 1)
    return (si >= li) if d else (si <= li)


def _mlstm_intra_weights(d, ci, qk, k, tab, bt):
    chunk = qk.shape[0]
    mask = _chunk_mask(d, chunk)
    row = pl.ds(ci, 1)
    c_masked = jnp.where(mask, tab[_G_C, row, :], -jnp.inf)
    cm = jnp.broadcast_to(jnp.max(c_masked, axis=-1, keepdims=True), c_masked.shape)
    p0 = (qk * jnp.exp(c_masked - cm)).astype(BF16)
    ktw = (k.astype(F32).T * tab[_G_WS, row, :]).astype(BF16)
    lane = lax.broadcasted_iota(jnp.int32, (chunk, chunk), 1)
    b_l = jnp.broadcast_to(jnp.sum(jnp.where(lane == ci, bt[...], 0.0), axis=-1, keepdims=True),
                           (chunk, chunk))
    return p0, ktw, cm, b_l


def _mlstm_state_matmuls(q, v, p0, ktw, c_state, n_state):
    chunk = q.shape[0]
    ones = jnp.ones((chunk, LANES), BF16)
    pk = jnp.concatenate([p0, ktw], axis=0)
    rs = jnp.dot(pk, ones, preferred_element_type=F32)
    qn = jnp.dot(q, n_state.astype(BF16), preferred_element_type=F32)
    qc = jnp.dot(q, c_state.astype(BF16), preferred_element_type=F32)
    nv = jnp.dot(pk, v, preferred_element_type=F32)
    return nv[:chunk], rs[:chunk], nv[chunk:], rs[chunk:], qc, qn


def _mlstm_combine(b_l, cm, b_tot, mw, prods, c_state, n_state, m):
    n0, r0, kv0, kn0, qc, qn = prods
    dv = n0.shape[1]
    scale = MLSTM_QK_DIM ** -0.5
    mx = jnp.maximum(m, cm)
    e1 = jnp.exp(cm - mx) * scale
    e2 = jnp.exp(m - mx) * scale
    den = e1 * r0 + e2 * qn
    inv = 1.0 / jnp.maximum(jnp.abs(den), jnp.exp(-(b_l + mx)))
    hout = _lane_tile(e1 * inv, dv) * n0 + _lane_tile(e2 * inv, dv) * qc

    m_new = jnp.maximum(b_tot + m, mw)
    d1 = jnp.exp(b_tot + m - m_new)
    d2 = jnp.exp(mw - m_new)
    c_new = _lane_tile(d1, dv) * c_state + _lane_tile(d2, dv) * kv0
    n_new = d1 * n_state + d2 * kn0
    return hout, c_new, n_new, m_new


MLSTM_HEADS_PER_STEP = 2


def _mlstm_kernel(bias_ref, q_ref, k_ref, v_ref, o_ref, z_ref, gate_ref, ng_ref, out_ref,
                  h_ref, c_ref, n_ref, tab_ref, bt_ref, pw_ref, pm_ref, *, seq):
    chunk = MLSTM_CHUNK
    n_chunks = seq // chunk
    dk, dv = MLSTM_QK_DIM, MLSTM_V_DIM
    streams = [(hh, d) for hh in range(MLSTM_HEADS_PER_STEP) for d in range(2)]
    for hh in range(MLSTM_HEADS_PER_STEP):
        _mlstm_gate_table(gate_ref, bias_ref, hh, pl.program_id(1) * MLSTM_HEADS_PER_STEP + hh,
                          tab_ref, bt_ref)
    c_ref[...] = jnp.zeros_like(c_ref)
    n_ref[...] = jnp.zeros_like(n_ref)

    def chunk_of(d, c):
        return (n_chunks - 1 - c) if d else c

    def rows_of(ci):
        return pl.ds(pl.multiple_of(ci * chunk, chunk), chunk)

    def qk_cols(hh):
        return slice(hh * dk, (hh + 1) * dk)

    def v_cols(hh):
        return slice(hh * dv, (hh + 1) * dv)

    def intra_matmuls(s, c):
        hh, d = streams[s]
        rows = rows_of(chunk_of(d, c))
        return _dot_nt(q_ref[0, rows, qk_cols(hh)], k_ref[0, rows, qk_cols(hh)])

    def park_weights(s, c, slot, qk):
        hh, d = streams[s]
        ci = chunk_of(d, c)
        p0, ktw, cm, b_l = _mlstm_intra_weights(d, ci, qk, k_ref[0, rows_of(ci), qk_cols(hh)],
                                                tab_ref.at[s], bt_ref.at[s])
        pw_ref[slot, s, 0] = p0
        pw_ref[slot, s, 1] = ktw
        pm_ref[slot, s, 0] = b_l
        pm_ref[slot, s, 1] = cm

    for s in range(len(streams)):
        park_weights(s, 0, 0, intra_matmuls(s, 0))

    def step(c, carry):
        ms = list(carry)
        slot = c % 2
        c_next = jnp.minimum(c + 1, n_chunks - 1)
        prods, parts = [], []
        for s, (hh, d) in enumerate(streams):
            rows = rows_of(chunk_of(d, c))
            prods.append(_mlstm_state_matmuls(q_ref[0, rows, qk_cols(hh)], v_ref[0, rows, v_cols(hh)],
                                              pw_ref[slot, s, 0], pw_ref[slot, s, 1], c_ref[s], n_ref[s]))
            parts.append(intra_matmuls(s, c_next))
        for s, (hh, d) in enumerate(streams):
            ci = chunk_of(d, c)
            hout, c_new, n_new, ms[s] = _mlstm_combine(
                pm_ref[slot, s, 0], pm_ref[slot, s, 1],
                tab_ref[s, _G_BTOT, pl.ds(ci, 1), :], tab_ref[s, _G_MW, pl.ds(ci, 1), :],
                prods[s], c_ref[s], n_ref[s], ms[s])
            h_ref[s, rows_of(ci), :] = hout
            c_ref[s] = c_new
            n_ref[s] = n_new
            park_weights(s, c_next, 1 - slot, parts[s])
        return tuple(ms)

    def finish(ci):
        rows = rows_of(ci)
        for hh in range(MLSTM_HEADS_PER_STEP):
            h_sum = h_ref[2 * hh, rows, :] + h_ref[2 * hh + 1, rows, :]
            hs2 = h_sum * jnp.tanh(o_ref[0, rows, v_cols(hh)]) + h_sum
            ms = jnp.mean(hs2 * hs2, axis=-1, keepdims=True)
            hs = hs2 * lax.rsqrt(ms + 4.0 * EPS) * ng_ref[:, v_cols(hh)]
            half_z = z_ref[0, rows, v_cols(hh)]
            out_ref[0, rows, v_cols(hh)] = (hs * (half_z * jnp.tanh(half_z) + half_z)).astype(out_ref.dtype)

    def step_and_finish(c, carry):
        carry = step(c, carry)
        finish(c)
        finish(n_chunks - 1 - c)
        return carry

    half = n_chunks // 2
    m0 = (jnp.zeros((1, LANES), F32),) * len(streams)
    ms_half = lax.fori_loop(0, half, step, m0)
    lax.fori_loop(half, n_chunks, step_and_finish, ms_half)


def _mlstm(qkv, oz, gates_t, gate_bias, norm_gain):
    b, s, _ = qkv.shape
    hps = MLSTM_HEADS_PER_STEP
    dk, dv = hps * MLSTM_QK_DIM, hps * MLSTM_V_DIM
    n_chunks = s // MLSTM_CHUNK
    n_streams = 2 * hps
    k_blk0 = MLSTM_QK_WIDTH // dk
    v_blk0 = 2 * MLSTM_QK_WIDTH // dv
    z_blk0 = MLSTM_WIDTH // dv
    return pl.pallas_call(
        functools.partial(_mlstm_kernel, seq=s),
        out_shape=jax.ShapeDtypeStruct((b, s, MLSTM_WIDTH), BF16),
        grid=(b, MLSTM_HEADS // hps),
        in_specs=[
            pl.BlockSpec(memory_space=pltpu.SMEM),
            pl.BlockSpec((1, s, dk), lambda i, h: (i, 0, h)),
            pl.BlockSpec((1, s, dk), lambda i, h: (i, 0, k_blk0 + h)),
            pl.BlockSpec((1, s, dv), lambda i, h: (i, 0, v_blk0 + h)),
            pl.BlockSpec((1, s, dv), lambda i, h: (i, 0, h)),
            pl.BlockSpec((1, s, dv), lambda i, h: (i, 0, z_blk0 + h)),
            pl.BlockSpec((N_GATE_SETS, hps, 1, n_chunks, MLSTM_CHUNK), lambda i, h: (0, h, i, 0, 0)),
            pl.BlockSpec((1, dv), lambda i, h: (0, h)),
        ],
        out_specs=pl.BlockSpec((1, s, dv), lambda i, h: (i, 0, h)),
        scratch_shapes=[
            pltpu.VMEM((n_streams, s, MLSTM_V_DIM), F32),
            pltpu.VMEM((n_streams, MLSTM_QK_DIM, MLSTM_V_DIM), F32),
            pltpu.VMEM((n_streams, MLSTM_QK_DIM, LANES), F32),
            pltpu.VMEM((n_streams, 4, n_chunks, MLSTM_CHUNK), F32),
            pltpu.VMEM((n_streams, MLSTM_CHUNK, MLSTM_CHUNK), F32),
            pltpu.VMEM((2, n_streams, 2, MLSTM_CHUNK, MLSTM_CHUNK), BF16),
            pltpu.VMEM((2, n_streams, 2, MLSTM_CHUNK, MLSTM_CHUNK), F32),
        ],
        compiler_params=_params("parallel", "parallel"),
        name="mlstm",
    )(gate_bias, qkv, qkv, qkv, oz, oz, gates_t, norm_gain)


def _row_tile(t, want):
    return want if t % want == 0 else t


def kernel(x, norm_gain, final_gain, even_w_in, even_w_out, q_norm_gain, k_norm_gain, conv_w, conv_b,
           lru_wa, lru_ba, lru_wx, lru_bx, lru_lambda, odd_w_in, odd_gate_bias, odd_norm_gain, odd_w_out):
    b, s, d = x.shape
    t = b * s
    assert s % (SCAN_SEGMENTS * SUBLANES) == 0 and s % (2 * MLSTM_CHUNK) == 0 and s % GRID_W == 0
    x2 = x.reshape(t, d)

    hn0 = _rmsnorm(x2, norm_gain[0][None], tm=_row_tile(t, 512))
    proj0 = _in_proj0(hn0, even_w_in[0], F32, tm=_row_tile(t, 1024), tn=768,
                      half_from_col=EVEN_IN - LRU_WIDTH).reshape(b, s, EVEN_IN)
    cos_t, sin_t = _rope_tables(s)
    attn = _attention(proj0, q_norm_gain[0][None], k_norm_gain[0][None], cos_t, sin_t,
                      tq=_row_tile(s, 256))
    lru = _lru(proj0, 0.5 * conv_w[0], 0.5 * conv_b[0][None],
               _lru_gate_weights(lru_wa[0], lru_ba[0], lru_wx[0], lru_bx[0]), lru_lambda[0])
    qkv_w = 2 * MLSTM_QK_WIDTH + MLSTM_WIDTH
    oz_w = 2 * MLSTM_WIDTH
    w1t = odd_w_in[0].T
    x1, hn1, gates_t = _out_proj([attn.reshape(t, ATTN_WIDTH), lru.reshape(t, LRU_WIDTH)],
                                 even_w_out[0].astype(BF16), x2, norm_gain[1][None], _row_tile(t, 512),
                                 "out_proj0", x_out=True, wg_t=w1t[qkv_w + oz_w:].astype(BF16))

    tm1 = _row_tile(t, 1024)
    tn1 = 1024
    qkv = _matmul(hn1, w1t, qkv_w, 0, BF16, tm1, tn1, "in_proj1_qkv").reshape(b, s, qkv_w)
    oz = _matmul(hn1, w1t, oz_w, qkv_w // tn1, F32, tm1, tn1, "in_proj1_oz", w_scale=0.5).reshape(b, s, oz_w)
    gates_t = gates_t.reshape(N_GATE_SETS, MLSTM_HEADS, b, s // MLSTM_CHUNK, MLSTM_CHUNK)
    mix1 = _mlstm(qkv, oz, gates_t, odd_gate_bias[0], odd_norm_gain[0][None])
    (out,) = _out_proj([mix1.reshape(t, MLSTM_WIDTH)], odd_w_out[0].astype(BF16), x1, final_gain[None],
                       _row_tile(t, 512), "out_proj1", x_out=False)
    return out.reshape(b, s, d)
```
